```python
import math
import jax
import jax.numpy as jnp
from jax import lax
import numpy as np

D_MODEL = 1024
BATCH = 4
SEQ = 8192
DEPTH = 1
DEC_BATCH = 16
DEC_SEQ = 2048
PAST_LEN = 128

D_MIX = D_MODEL
RW_HEADS = 8
RW_N = 64
RW_C = RW_HEADS * RW_N
LORA_W = 64
LORA_A = 64
LORA_G = 128
DECAY_SCALE = 0.606531
GN_EPS = 64e-5
DA_HEADS = 4
DA_DH = 64
DA_C = DA_HEADS * 2 * DA_DH
ROT_DIM = DA_DH // 4
ROPE_THETA = 500000.0
Q_BLOCK = 128
CONV_W = 3
D_FF = 2816
NORM_EPS = 1e-6
RW_COLS = 3 * RW_C + 2 * LORA_W + 2 * LORA_A + LORA_G
IN_COLS = RW_COLS + 3 * DA_C

kernel_name = 'hybrid_rwkv7_diffattn_macaron_encoder'


def rms_norm(x, g, eps=NORM_EPS):
    xf = x.astype(jnp.float32)
    y = xf * lax.rsqrt(jnp.mean(xf * xf, axis=-1, keepdims=True) + eps)
    return (y * g.astype(jnp.float32)).astype(x.dtype)


def swiglu_ffn(x, norm_g, w_gu, w_down):
    h = rms_norm(x, norm_g)
    gate, up = jnp.split(h @ w_gu, 2, axis=-1)
    return (jax.nn.silu(gate) * up) @ w_down


def centred_short_conv(u, taps):
    prev = jnp.pad(u[:, :-1], ((0, 0), (1, 0), (0, 0)))
    nxt = jnp.pad(u[:, 1:], ((0, 0), (0, 1), (0, 0)))
    return prev * taps[0] + u * taps[1] + nxt * taps[2]


def rope_tables(seq_len):
    inv_freq = ROPE_THETA ** (-jnp.arange(0, ROT_DIM, 2, dtype=jnp.float32) / ROT_DIM)
    ang = jnp.arange(seq_len, dtype=jnp.float32)[:, None] * inv_freq[None, :]
    return jnp.cos(ang), jnp.sin(ang)


def partial_rope(x, cos, sin):
    c = cos[None, :, None, None, :]
    s = sin[None, :, None, None, :]
    half = ROT_DIM // 2
    x1 = x[..., :half]
    x2 = x[..., half:ROT_DIM]
    return jnp.concatenate([x1 * c - x2 * s, x2 * c + x1 * s, x[..., ROT_DIM:]], axis=-1)


def wkv7_scan(r, w, k, v, kk, a, reverse):
    B, _, H, N = r.shape
    xs = tuple(jnp.moveaxis(t, 1, 0) for t in (r, w, k, v, kk, a))

    def step(state, inp):
        r_t, w_t, k_t, v_t, kk_t, a_t = inp
        sa = jnp.einsum('bhij,bhj->bhi', state, -kk_t)
        state = (state * w_t[:, :, None, :]
                 + sa[..., None] * (kk_t * a_t)[:, :, None, :]
                 + v_t[..., None] * k_t[:, :, None, :])
        y_t = jnp.einsum('bhij,bhj->bhi', state, r_t)
        return state, y_t

    s0 = jnp.zeros((B, H, N, N), jnp.float32)
    _, ys = lax.scan(step, s0, xs, reverse=reverse)
    return jnp.moveaxis(ys, 0, 1)


def rwkv7_bidir_mixer(u, w0, w_up, a0, a_up, g_up, k_k, k_a, r_k, ln_g, ln_b):
    B, S, _ = u.shape
    uf = u.astype(jnp.float32)
    o_w = 3 * RW_C
    o_a = o_w + 2 * LORA_W
    o_g = o_a + 2 * LORA_A
    r = uf[..., 0:RW_C]
    k = uf[..., RW_C:2 * RW_C]
    v = uf[..., 2 * RW_C:3 * RW_C]
    wd = uf[..., o_w:o_a].reshape(B, S, 2, LORA_W)
    ad = uf[..., o_a:o_g].reshape(B, S, 2, LORA_A)
    gd = uf[..., o_g:RW_COLS]
    w = jnp.exp(-DECAY_SCALE * jax.nn.sigmoid(w0 + jnp.einsum('bsdr,drc->bsdc', jnp.tanh(wd), w_up)))
    a = jax.nn.sigmoid(a0 + jnp.einsum('bsdr,drc->bsdc', ad, a_up))
    g = jax.nn.sigmoid(gd) @ g_up

    def heads(t):
        return t.reshape(t.shape[:-1] + (RW_HEADS, RW_N))

    kk = heads(k * k_k)
    kk = kk / jnp.maximum(jnp.sqrt(jnp.sum(kk * kk, axis=-1, keepdims=True)), 1e-12)
    k_dir = k[:, :, None, :] * (1.0 + (a - 1.0) * k_a)
    rh = heads(r)
    vh = heads(v)
    kh = heads(k_dir)
    wh = heads(w)
    ah = heads(a)
    y = (wkv7_scan(rh, wh[:, :, 0], kh[:, :, 0], vh, kk, ah[:, :, 0], False)
         + wkv7_scan(rh, wh[:, :, 1], kh[:, :, 1], vh, kk, ah[:, :, 1], True))
    mu = jnp.mean(y, axis=-1, keepdims=True)
    var = jnp.mean(jnp.square(y - mu), axis=-1, keepdims=True)
    y = ((y - mu) * lax.rsqrt(var + GN_EPS)).reshape(B, S, RW_C) * ln_g + ln_b
    coef = jnp.sum(jnp.sum(rh[:, :, None] * kh * r_k, axis=-1, keepdims=True), axis=2)
    bonus = (coef * vh).reshape(B, S, RW_C)
    return ((y + bonus) * g).astype(u.dtype)


def diff_attention(q, k, v, q_norm, k_norm, lq1, lk1, lq2, lk2, subln, cos, sin, lam_init):
    B, S, _ = q.shape
    f32 = jnp.float32
    q = q.astype(f32).reshape(B, S, DA_HEADS, 2, DA_DH)
    k = k.astype(f32).reshape(B, S, DA_HEADS, 2, DA_DH)
    v = v.astype(f32).reshape(B, S, DA_HEADS, 2 * DA_DH)
    q = partial_rope(rms_norm(q, q_norm), cos, sin) * (DA_DH ** -0.5)
    k = partial_rope(rms_norm(k, k_norm), cos, sin)
    lam = (jnp.exp(jnp.sum(lq1.astype(f32) * lk1.astype(f32)))
           - jnp.exp(jnp.sum(lq2.astype(f32) * lk2.astype(f32)))) + lam_init
    n_blk = S // Q_BLOCK
    q_blocks = jnp.moveaxis(q.reshape(B, n_blk, Q_BLOCK, DA_HEADS, 2, DA_DH), 1, 0)

    def attend(q_blk):
        s = jnp.einsum('bqhcd,bkhcd->bhcqk', q_blk, k)
        p = jax.nn.softmax(s, axis=-1)
        diff = p[:, :, 0] - lam * p[:, :, 1]
        return jnp.einsum('bhqk,bkhe->bqhe', diff, v)

    o = lax.map(attend, q_blocks)
    o = jnp.moveaxis(o, 0, 1).reshape(B, S, DA_HEADS, 2 * DA_DH)
    o = rms_norm(o, subln) * (1.0 - lam_init)
    return o.reshape(B, S, DA_C)


def encoder_layer(x, cos, sin, lam_init,
                  ffn1_norm, ffn1_w_gu, ffn1_w_down,
                  mix_norm, w_in, conv_w,
                  rw_w0, rw_w_up, rw_a0, rw_a_up, rw_g_up, rw_k_k, rw_k_a, rw_r_k, rw_ln_g, rw_ln_b,
                  da_q_norm, da_k_norm, da_lq1, da_lk1, da_lq2, da_lk2, da_subln,
                  w_out, ffn2_norm, ffn2_w_gu, ffn2_w_down, final_norm):
    x = x + 0.5 * swiglu_ffn(x, ffn1_norm, ffn1_w_gu, ffn1_w_down)
    h = rms_norm(x, mix_norm)
    proj = h @ w_in
    rw_in = centred_short_conv(proj[..., :RW_COLS], conv_w)
    da_q, da_k, da_v = jnp.split(proj[..., RW_COLS:], 3, axis=-1)
    y_rw = rwkv7_bidir_mixer(rw_in, rw_w0, rw_w_up, rw_a0, rw_a_up, rw_g_up,
                             rw_k_k, rw_k_a, rw_r_k, rw_ln_g, rw_ln_b)
    y_da = diff_attention(da_q, da_k, da_v, da_q_norm, da_k_norm, da_lq1, da_lk1, da_lq2, da_lk2,
                          da_subln, cos, sin, lam_init)
    mixed = jnp.concatenate([y_rw.astype(x.dtype), y_da.astype(x.dtype)], axis=-1) @ w_out
    x = x + mixed.astype(x.dtype)
    x = x + 0.5 * swiglu_ffn(x, ffn2_norm, ffn2_w_gu, ffn2_w_down)
    return rms_norm(x, final_norm)


def setup_inputs(seed: int = 0) -> dict:
    key = jax.random.key(seed)
    ks = jax.random.split(key, 36)
    f32 = jnp.float32
    L = DEPTH

    def nrm(k, shape, scale):
        return jax.random.normal(k, shape, f32) * scale

    return {
        'x_prompt': nrm(ks[0], (BATCH, SEQ, D_MODEL), 1.0),
        'x_sample': nrm(ks[1], (DEC_BATCH, DEC_SEQ, D_MODEL), 1.0),
        'ffn1_norm': 1.0 + nrm(ks[2], (L, D_MODEL), 0.02),
        'ffn1_w_gu': nrm(ks[3], (L, D_MODEL, 2 * D_FF), D_MODEL ** -0.5),
        'ffn1_w_down': nrm(ks[4], (L, D_FF, D_MODEL), D_FF ** -0.5),
        'mix_norm': 1.0 + nrm(ks[5], (L, D_MODEL), 0.02),
        'w_in': nrm(ks[6], (L, D_MODEL, IN_COLS), D_MODEL ** -0.5),
        'conv_w': jnp.array([0.25, 0.5, 0.25], f32)[None, :, None] + nrm(ks[7], (L, CONV_W, RW_COLS), 0.05),
        'rw_w0': nrm(ks[8], (L, 2, RW_C), 0.5),
        'rw_w_up': nrm(ks[9], (L, 2, LORA_W, RW_C), 0.1 * LORA_W ** -0.5),
        'rw_a0': nrm(ks[10], (L, 2, RW_C), 0.5),
        'rw_a_up': nrm(ks[11], (L, 2, LORA_A, RW_C), 0.1 * LORA_A ** -0.5),
        'rw_g_up': nrm(ks[12], (L, LORA_G, RW_C), LORA_G ** -0.5),
        'rw_k_k': 0.85 + nrm(ks[13], (L, RW_C), 0.02),
        'rw_k_a': 1.0 + nrm(ks[14], (L, RW_C), 0.02),
        'rw_r_k': nrm(ks[15], (L, RW_HEADS, RW_N), 0.1),
        'rw_ln_g': 1.0 + nrm(ks[16], (L, RW_C), 0.02),
        'rw_ln_b': nrm(ks[17], (L, RW_C), 0.01),
        'da_q_norm': 1.0 + nrm(ks[18], (L, DA_DH), 0.02),
        'da_k_norm': 1.0 + nrm(ks[19], (L, DA_DH), 0.02),
        'da_lq1': nrm(ks[20], (L, DA_DH), 0.1),
        'da_lk1': nrm(ks[21], (L, DA_DH), 0.1),
        'da_lq2': nrm(ks[22], (L, DA_DH), 0.1),
        'da_lk2': nrm(ks[23], (L, DA_DH), 0.1),
        'da_subln': 1.0 + nrm(ks[24], (L, 2 * DA_DH), 0.02),
        'w_out': nrm(ks[25], (L, D_MIX, D_MODEL), D_MIX ** -0.5),
        'ffn2_norm': 1.0 + nrm(ks[26], (L, D_MODEL), 0.02),
        'ffn2_w_gu': nrm(ks[27], (L, D_MODEL, 2 * D_FF), D_MODEL ** -0.5),
        'ffn2_w_down': nrm(ks[28], (L, D_FF, D_MODEL), D_FF ** -0.5),
        'final_norm': 1.0 + nrm(ks[29], (L, D_MODEL), 0.02),
    }


def reference(x_prompt, x_sample, ffn1_norm, ffn1_w_gu, ffn1_w_down, mix_norm, w_in, conv_w,
              rw_w0, rw_w_up, rw_a0, rw_a_up, rw_g_up, rw_k_k, rw_k_a, rw_r_k, rw_ln_g, rw_ln_b,
              da_q_norm, da_k_norm, da_lq1, da_lk1, da_lq2, da_lk2, da_subln,
              w_out, ffn2_norm, ffn2_w_gu, ffn2_w_down, final_norm):
    cos_p, sin_p = rope_tables(x_prompt.shape[1])
    cos_s, sin_s = rope_tables(x_sample.shape[1])
    y_prompt = x_prompt
    y_sample = x_sample
    for l in range(DEPTH):
        lam_init = 0.8 - 0.6 * math.exp(-0.3 * l)
        params = (ffn1_norm[l], ffn1_w_gu[l], ffn1_w_down[l], mix_norm[l], w_in[l], conv_w[l],
                  rw_w0[l], rw_w_up[l], rw_a0[l], rw_a_up[l], rw_g_up[l], rw_k_k[l], rw_k_a[l],
                  rw_r_k[l], rw_ln_g[l], rw_ln_b[l],
                  da_q_norm[l], da_k_norm[l], da_lq1[l], da_lk1[l], da_lq2[l], da_lk2[l], da_subln[l],
                  w_out[l], ffn2_norm[l], ffn2_w_gu[l], ffn2_w_down[l], final_norm[l])
        y_prompt = encoder_layer(y_prompt, cos_p, sin_p, lam_init, *params)
        y_sample = encoder_layer(y_sample, cos_s, sin_s, lam_init, *params)
    return (y_prompt, y_sample)
```

```python
import functools
import math

import jax
import jax.numpy as jnp
from jax import lax
from jax.experimental import pallas as pl
from jax.experimental.pallas import tpu as pltpu

F32 = jnp.float32
BF16 = jnp.bfloat16

D_MODEL = 1024
D_FF = 2816
RW_HEADS = 8
RW_N = 64
RW_C = RW_HEADS * RW_N
LORA_W = 64
LORA_A = 64
LORA_G = 128
DECAY_SCALE = 0.606531
GN_EPS = 64e-5
DA_HEADS = 4
DA_DH = 64
DA_C = DA_HEADS * 2 * DA_DH
ROT_DIM = DA_DH // 4
ROPE_THETA = 500000.0
NORM_EPS = 1e-6
RW_COLS = 3 * RW_C + 2 * LORA_W + 2 * LORA_A + LORA_G
LAM_INIT = 0.8 - 0.6 * math.exp(-0.3 * 0)

V7X_VMEM_BYTES = 64 * 1024 * 1024
SUBLANES = 8

FF_CHUNK = 256
N_FF_CHUNKS = D_FF // FF_CHUNK
TOKEN_TILE = 512
PREP_TILE = 256
CHUNK = 64
ATT_TQ = 256
ATT_TK = 512


def _params(sem, vmem_mb):
    return pltpu.CompilerParams(dimension_semantics=sem, vmem_limit_bytes=vmem_mb * 1024 * 1024)


def _dot(a, b):
    return jnp.dot(a, b, preferred_element_type=F32)


def _dot_nt(a, b):
    return lax.dot_general(a, b, (((1,), (1,)), ((), ())), preferred_element_type=F32)


def _dot_tn(a, b):
    return lax.dot_general(a, b, (((0,), (0,)), ((), ())), preferred_element_type=F32)


def _split2(x):
    hi = x.astype(BF16)
    lo = (x - hi.astype(F32)).astype(BF16)
    return hi, lo


def _split3(x):
    hi = x.astype(BF16)
    r1 = x - hi.astype(F32)
    mid = r1.astype(BF16)
    lo = (r1 - mid.astype(F32)).astype(BF16)
    return hi, mid, lo


def _group_sum(x, bd):
    hi, lo = _split2(x)
    return _dot(hi, bd) + _dot(lo, bd)


def _rms(x, g, eps=NORM_EPS):
    return x * lax.rsqrt(jnp.mean(x * x, axis=-1, keepdims=True) + eps) * g


def _ffn_kernel(x_ref, g_ref, wg_ref, wu_ref, wd_ref, *rest, final):
    if final:
        fg_ref, o_ref = rest
    else:
        (o_ref,) = rest
    x = x_ref[...]
    h = _rms(x, g_ref[...]).astype(BF16)
    acc = jnp.zeros_like(x)
    for j in range(N_FF_CHUNKS):
        gate = _dot(h, wg_ref[j])
        up = _dot(h, wu_ref[j])
        act = (gate * jax.nn.sigmoid(gate) * up).astype(BF16)
        acc = acc + _dot(act, wd_ref[j])
    y = x + 0.5 * acc
    if final:
        y = _rms(y, fg_ref[...])
    o_ref[...] = y


def _ffn(x, norm_g, w_gu, w_down, final_g=None):
    t = x.shape[0]
    tm = min(TOKEN_TILE, t)
    wg = w_gu[:, :D_FF].reshape(D_MODEL, N_FF_CHUNKS, FF_CHUNK).transpose(1, 0, 2).astype(BF16)
    wu = w_gu[:, D_FF:].reshape(D_MODEL, N_FF_CHUNKS, FF_CHUNK).transpose(1, 0, 2).astype(BF16)
    wd = w_down.reshape(N_FF_CHUNKS, FF_CHUNK, D_MODEL).astype(BF16)
    const3 = lambda i: (0, 0, 0)
    const2 = lambda i: (0, 0)
    in_specs = [
        pl.BlockSpec((tm, D_MODEL), lambda i: (i, 0)),
        pl.BlockSpec((1, D_MODEL), const2),
        pl.BlockSpec((N_FF_CHUNKS, D_MODEL, FF_CHUNK), const3),
        pl.BlockSpec((N_FF_CHUNKS, D_MODEL, FF_CHUNK), const3),
        pl.BlockSpec((N_FF_CHUNKS, FF_CHUNK, D_MODEL), const3),
    ]
    args = [x, norm_g.reshape(1, D_MODEL), wg, wu, wd]
    if final_g is not None:
        in_specs.append(pl.BlockSpec((1, D_MODEL), const2))
        args.append(final_g.reshape(1, D_MODEL))
    return pl.pallas_call(
        functools.partial(_ffn_kernel, final=final_g is not None),
        grid=(t // tm,),
        in_specs=in_specs,
        out_specs=pl.BlockSpec((tm, D_MODEL), lambda i: (i, 0)),
        out_shape=jax.ShapeDtypeStruct((t, D_MODEL), F32),
        compiler_params=_params(("parallel",), 56),
        name="ffn_final" if final_g is not None else "ffn",
    )(*args)


def _proj_kernel(x_ref, g_ref, wrw_ref, wda_ref, rw_ref, da_ref):
    h = _rms(x_ref[...], g_ref[...]).astype(BF16)
    rw_ref[...] = _dot(h, wrw_ref[...])
    da_ref[...] = _dot(h, wda_ref[...])


def _proj(x, norm_g, w_in):
    t = x.shape[0]
    tm = min(TOKEN_TILE, t)
    w_rw = w_in[:, :RW_COLS].astype(BF16)
    w_da = w_in[:, RW_COLS:].astype(BF16)
    const2 = lambda i: (0, 0)
    return pl.pallas_call(
        _proj_kernel,
        grid=(t // tm,),
        in_specs=[
            pl.BlockSpec((tm, D_MODEL), lambda i: (i, 0)),
            pl.BlockSpec((1, D_MODEL), const2),
            pl.BlockSpec((D_MODEL, RW_COLS), const2),
            pl.BlockSpec((D_MODEL, 3 * DA_C), const2),
        ],
        out_specs=[
            pl.BlockSpec((tm, RW_COLS), lambda i: (i, 0)),
            pl.BlockSpec((tm, 3 * DA_C), lambda i: (i, 0)),
        ],
        out_shape=[
            jax.ShapeDtypeStruct((t, RW_COLS), F32),
            jax.ShapeDtypeStruct((t, 3 * DA_C), F32),
        ],
        compiler_params=_params(("parallel",), 48),
        name="proj",
    )(x, norm_g.reshape(1, D_MODEL), w_rw, w_da)


def _rwprep_kernel(u_ref, up_ref, un_ref, conv_ref, w0_ref, wup_ref, a0_ref, aup_ref, gup_ref,
                   kk_ref, ka_ref, rk_ref, bd_ref,
                   r_o, v_o, kk_o, g_o, bonus_o, lw_o, kd_o, b_o, *, tm, s_len):
    i = pl.program_id(0)
    u = u_ref[...]
    start = i * tm
    first = (start % s_len) == 0
    last = ((start + tm) % s_len) == 0
    prev_row = jnp.where(first, 0.0, up_ref[SUBLANES - 1:SUBLANES, :])
    next_row = jnp.where(last, 0.0, un_ref[0:1, :])
    row = lax.broadcasted_iota(jnp.int32, (tm, 1), 0)
    prev = jnp.where(row == 0, prev_row, pltpu.roll(u, 1, 0))
    nxt = jnp.where(row == tm - 1, next_row, pltpu.roll(u, tm - 1, 0))
    c = prev * conv_ref[0:1, :] + u * conv_ref[1:2, :] + nxt * conv_ref[2:3, :]

    o_w = 3 * RW_C
    o_a = o_w + 2 * LORA_W
    o_g = o_a + 2 * LORA_A
    r = c[:, 0:RW_C]
    k = c[:, RW_C:2 * RW_C]
    v = c[:, 2 * RW_C:3 * RW_C]
    tw = jnp.tanh(c[:, o_w:o_a]).astype(BF16)
    ad = c[:, o_a:o_g].astype(BF16)
    gd = c[:, o_g:RW_COLS]
    bd = bd_ref[...]

    kkv = k * kk_ref[...]
    norm = jnp.sqrt(_group_sum(kkv * kkv, bd))
    kk = kkv / jnp.maximum(norm, 1e-12)
    g = _dot(jax.nn.sigmoid(gd).astype(BF16), gup_ref[...])

    ksum = jnp.zeros_like(k)
    for d in range(2):
        lw = -DECAY_SCALE * jax.nn.sigmoid(w0_ref[d:d + 1, :] + _dot(tw, wup_ref[d]))
        a = jax.nn.sigmoid(a0_ref[d:d + 1, :] + _dot(ad, aup_ref[d]))
        kd = k * (1.0 + (a - 1.0) * ka_ref[...])
        lw_o[d] = lw
        kd_o[d] = kd
        b_o[d] = kk * a
        ksum = ksum + kd
    coef = _group_sum(r * ksum * rk_ref[...], bd)
    r_o[...] = r
    v_o[...] = v
    kk_o[...] = kk
    g_o[...] = g
    bonus_o[...] = coef * v


def _pad_dir_rows(w_up):
    r = w_up.shape[1]
    z = jnp.zeros_like(w_up[0])
    return jnp.stack([jnp.concatenate([w_up[0], z], 0), jnp.concatenate([z, w_up[1]], 0)]).astype(BF16)


def _rwprep(proj_rw, s_len, conv_w, w0, w_up, a0, a_up, g_up, k_k, k_a, r_k, bd):
    t = proj_rw.shape[0]
    tm = min(PREP_TILE, s_len)
    nb8 = t // SUBLANES
    per = tm // SUBLANES
    const2 = lambda i: (0, 0)
    const3 = lambda i: (0, 0, 0)
    tok = pl.BlockSpec((tm, RW_C), lambda i: (i, 0))
    tok2 = pl.BlockSpec((2, tm, RW_C), lambda i: (0, i, 0))
    vec = pl.BlockSpec((1, RW_C), const2)
    one = jax.ShapeDtypeStruct((t, RW_C), F32)
    two = jax.ShapeDtypeStruct((2, t, RW_C), F32)
    return pl.pallas_call(
        functools.partial(_rwprep_kernel, tm=tm, s_len=s_len),
        grid=(t // tm,),
        in_specs=[
            pl.BlockSpec((tm, RW_COLS), lambda i: (i, 0)),
            pl.BlockSpec((SUBLANES, RW_COLS), lambda i: (jnp.maximum(i * per - 1, 0), 0)),
            pl.BlockSpec((SUBLANES, RW_COLS), lambda i: (jnp.minimum((i + 1) * per, nb8 - 1), 0)),
            pl.BlockSpec((3, RW_COLS), const2),
            pl.BlockSpec((2, RW_C), const2),
            pl.BlockSpec((2, 2 * LORA_W, RW_C), const3),
            pl.BlockSpec((2, RW_C), const2),
            pl.BlockSpec((2, 2 * LORA_A, RW_C), const3),
            pl.BlockSpec((LORA_G, RW_C), const2),
            vec, vec, vec,
            pl.BlockSpec((RW_C, RW_C), const2),
        ],
        out_specs=[tok, tok, tok, tok, tok, tok2, tok2, tok2],
        out_shape=[one, one, one, one, one, two, two, two],
        compiler_params=_params(("parallel",), 48),
        name="rwprep",
    )(proj_rw, proj_rw, proj_rw, conv_w, w0, _pad_dir_rows(w_up), a0, _pad_dir_rows(a_up),
      g_up.astype(BF16), k_k.reshape(1, RW_C), k_a.reshape(1, RW_C), r_k.reshape(1, RW_C), bd)


def _scan_kernel(r_ref, v_ref, kk_ref, lw_ref, kd_ref, b_ref, y_ref, st_ref, *, chunk):
    L = chunk
    d = pl.program_id(1)
    ci_idx = pl.program_id(2)

    @pl.when(ci_idx == 0)
    def _():
        st_ref[...] = jnp.zeros_like(st_ref)

    lw = lw_ref[...]
    row = lax.broadcasted_iota(jnp.int32, (L, L), 0)
    col = lax.broadcasted_iota(jnp.int32, (L, L), 1)
    ahead = (row - col) * (1 - 2 * d)
    incl = ahead >= 0
    strict = ahead > 0
    eye = (row == col).astype(F32)
    tri = incl.astype(BF16)
    hi, mid, lo = _split3(lw)
    c_in = _dot(tri, hi) + _dot(tri, mid) + _dot(tri, lo)
    c_ex = c_in - lw
    c_tot = jnp.sum(lw, axis=0, keepdims=True)
    e_neg = jnp.exp(-c_in)
    e_end = jnp.exp(c_tot - c_in)
    kk = kk_ref[...]
    b = b_ref[...]
    kd = kd_ref[...]
    a_t = (-kk * jnp.exp(c_ex)).astype(BF16)
    r_t = (r_ref[...] * jnp.exp(c_in)).astype(BF16)
    b_t = (b * e_neg).astype(BF16)
    k_t = (kd * e_neg).astype(BF16)
    b_p = (b * e_end).astype(BF16)
    k_p = (kd * e_end).astype(BF16)
    p_end = jnp.exp(c_tot)
    v = v_ref[...].astype(BF16)

    for h in range(RW_HEADS):
        sl = slice(h * RW_N, (h + 1) * RW_N)
        a_h, r_h, v_h = a_t[:, sl], r_t[:, sl], v[:, sl]
        lhs = jnp.concatenate([a_h, r_h], axis=0)
        rhs = jnp.concatenate([b_t[:, sl], k_t[:, sl]], axis=0)
        x = _dot_nt(lhs, rhs)
        a_ab = jnp.where(strict, x[:L, :L], 0.0)
        a_ak = jnp.where(strict, x[:L, L:], 0.0).astype(BF16)
        a_rb = jnp.where(incl, x[L:, :L], 0.0).astype(BF16)
        a_rk = jnp.where(incl, x[L:, L:], 0.0).astype(BF16)
        inv = eye + a_ab
        pw = a_ab
        for _ in range(int(math.log2(L)) - 1):
            pwb = pw.astype(BF16)
            pw = _dot(pwb, pwb)
            inv = inv + _dot(inv.astype(BF16), pw.astype(BF16))
        inv = inv.astype(BF16)
        w_t = _dot(inv, a_h).astype(BF16)
        u_t = _dot(inv, _dot(a_ak, v_h).astype(BF16))
        st = st_ref[h]
        stb = st.astype(BF16)
        u = (_dot_nt(w_t, stb) + u_t).astype(BF16)
        y_ref[:, sl] = _dot_nt(r_h, stb) + _dot(a_rb, u) + _dot(a_rk, v_h)
        st_ref[h] = st * p_end[:, sl] + _dot_tn(u, b_p[:, sl]) + _dot_tn(v_h, k_p[:, sl])


def _scan(r, v, kk, lw, kd, b, bsz, s_len):
    L = CHUNK
    nc = s_len // L
    shp = lambda a: a.reshape(bsz, s_len, RW_C)
    shp2 = lambda a: a.reshape(2, bsz, s_len, RW_C)
    cidx = lambda d, c: jnp.where(d == 0, c, nc - 1 - c)
    shared = pl.BlockSpec((None, L, RW_C), lambda bi, d, c: (bi, cidx(d, c), 0))
    perdir = pl.BlockSpec((None, None, L, RW_C), lambda bi, d, c: (d, bi, cidx(d, c), 0))
    y = pl.pallas_call(
        functools.partial(_scan_kernel, chunk=L),
        grid=(bsz, 2, nc),
        in_specs=[shared, shared, shared, perdir, perdir, perdir],
        out_specs=perdir,
        out_shape=jax.ShapeDtypeStruct((2, bsz, s_len, RW_C), F32),
        scratch_shapes=[pltpu.VMEM((RW_HEADS, RW_N, RW_N), F32)],
        compiler_params=_params(("parallel", "parallel", "arbitrary"), 32),
        name="scan",
    )(shp(r), shp(v), shp(kk), shp2(lw), shp2(kd), shp2(b))
    return y.reshape(2, bsz * s_len, RW_C)


def _daprep_kernel(p_ref, cf_ref, sn_ref, sp_ref, qn_ref, kn_ref, bd_ref, q_o, k_o, v_o):
    bd = bd_ref[...]
    cf, sn, sp = cf_ref[...], sn_ref[...], sp_ref[...]

    def norm_rope(x, g):
        ms = _group_sum(x * x, bd) * (1.0 / DA_DH)
        xn = x * lax.rsqrt(ms + NORM_EPS) * g
        half = ROT_DIM // 2
        return xn * cf + pltpu.roll(xn, DA_C - half, 1) * sn + pltpu.roll(xn, half, 1) * sp

    q = norm_rope(p_ref[:, 0:DA_C], qn_ref[...]) * (DA_DH ** -0.5)
    k = norm_rope(p_ref[:, DA_C:2 * DA_C], kn_ref[...])
    q_o[...] = q.astype(BF16)
    k_o[...] = k.astype(BF16)
    v_o[...] = p_ref[:, 2 * DA_C:3 * DA_C].astype(BF16)


def _rope_lane_tables(s_len):
    half = ROT_DIM // 2
    inv_freq = ROPE_THETA ** (-jnp.arange(0, ROT_DIM, 2, dtype=F32) / ROT_DIM)
    ang = jnp.arange(s_len, dtype=F32)[:, None] * inv_freq[None, :]
    cos, sin = jnp.cos(ang), jnp.sin(ang)
    j = jnp.arange(DA_C) % DA_DH
    f = j % half
    cf = jnp.where(j < ROT_DIM, cos[:, f], 1.0)
    sn = jnp.where(j < half, -sin[:, f], 0.0)
    sp = jnp.where((j >= half) & (j < ROT_DIM), sin[:, f], 0.0)
    return cf, sn, sp


def _daprep(proj_da, s_len, q_norm, k_norm, bd):
    t = proj_da.shape[0]
    tm = min(PREP_TILE, s_len)
    per_seq = s_len // tm
    cf, sn, sp = _rope_lane_tables(s_len)
    const2 = lambda i: (0, 0)
    tok = pl.BlockSpec((tm, DA_C), lambda i: (i, 0))
    tab = pl.BlockSpec((tm, DA_C), lambda i: (i % per_seq, 0))
    vec = pl.BlockSpec((1, DA_C), const2)
    out = jax.ShapeDtypeStruct((t, DA_C), BF16)
    reps = DA_C // DA_DH
    return pl.pallas_call(
        _daprep_kernel,
        grid=(t // tm,),
        in_specs=[pl.BlockSpec((tm, 3 * DA_C), lambda i: (i, 0)), tab, tab, tab, vec, vec,
                  pl.BlockSpec((DA_C, DA_C), const2)],
        out_specs=[tok, tok, tok],
        out_shape=[out, out, out],
        compiler_params=_params(("parallel",), 32),
        name="daprep",
    )(proj_da, cf, sn, sp, jnp.tile(q_norm, reps).reshape(1, DA_C), jnp.tile(k_norm, reps).reshape(1, DA_C), bd)


def _attn_kernel(q_ref, k_ref, v_ref, lq1_ref, lk1_ref, lq2_ref, lk2_ref, sub_ref, o_ref,
                 qs_ref, m_ref, l_ref, acc_ref, *, tq):
    ki = pl.program_id(3)

    @pl.when(ki == 0)
    def _():
        q = q_ref[...]
        lane = lax.broadcasted_iota(jnp.int32, q.shape, 1)
        zero = jnp.zeros_like(q)
        qs_ref[0:tq, :] = jnp.where(lane < DA_DH, q, zero)
        qs_ref[tq:2 * tq, :] = jnp.where(lane >= DA_DH, q, zero)
        m_ref[...] = jnp.full_like(m_ref, -jnp.inf)
        l_ref[...] = jnp.zeros_like(l_ref)
        acc_ref[...] = jnp.zeros_like(acc_ref)

    s = _dot_nt(qs_ref[...], k_ref[...])
    m_prev = m_ref[...]
    m_new = jnp.maximum(m_prev, jnp.max(s, axis=1, keepdims=True))
    p = jnp.exp(s - m_new)
    corr = jnp.exp(m_prev - m_new)
    l_ref[...] = corr * l_ref[...] + jnp.sum(p, axis=1, keepdims=True)
    acc_ref[...] = corr * acc_ref[...] + _dot(p.astype(BF16), v_ref[...])
    m_ref[...] = m_new

    @pl.when(ki == pl.num_programs(3) - 1)
    def _():
        o2 = acc_ref[...] / l_ref[...]
        lam = (jnp.exp(jnp.sum(lq1_ref[...] * lk1_ref[...], axis=1, keepdims=True))
               - jnp.exp(jnp.sum(lq2_ref[...] * lk2_ref[...], axis=1, keepdims=True))) + LAM_INIT
        o = o2[0:tq, :] - lam * o2[tq:2 * tq, :]
        o_ref[...] = _rms(o, sub_ref[...]) * (1.0 - LAM_INIT)


def _attn(q, k, v, bsz, s_len, lq1, lk1, lq2, lk2, subln):
    tq = min(ATT_TQ, s_len)
    tk = min(ATT_TK, s_len)
    hw = 2 * DA_DH
    shp = lambda a: a.reshape(bsz, s_len, DA_C)
    lvec = pl.BlockSpec((1, DA_DH), lambda b, h, qi, ki: (0, 0))
    o = pl.pallas_call(
        functools.partial(_attn_kernel, tq=tq),
        grid=(bsz, DA_HEADS, s_len // tq, s_len // tk),
        in_specs=[
            pl.BlockSpec((None, tq, hw), lambda b, h, qi, ki: (b, qi, h)),
            pl.BlockSpec((None, tk, hw), lambda b, h, qi, ki: (b, ki, h)),
            pl.BlockSpec((None, tk, hw), lambda b, h, qi, ki: (b, ki, h)),
            lvec, lvec, lvec, lvec,
            pl.BlockSpec((1, hw), lambda b, h, qi, ki: (0, 0)),
        ],
        out_specs=pl.BlockSpec((None, tq, hw), lambda b, h, qi, ki: (b, qi, h)),
        out_shape=jax.ShapeDtypeStruct((bsz, s_len, DA_C), F32),
        scratch_shapes=[
            pltpu.VMEM((2 * tq, hw), BF16),
            pltpu.VMEM((2 * tq, 1), F32),
            pltpu.VMEM((2 * tq, 1), F32),
            pltpu.VMEM((2 * tq, hw), F32),
        ],
        compiler_params=_params(("parallel", "parallel", "parallel", "arbitrary"), 32),
        name="attn",
    )(shp(q), shp(k), shp(v), lq1.reshape(1, DA_DH), lk1.reshape(1, DA_DH), lq2.reshape(1, DA_DH),
      lk2.reshape(1, DA_DH), subln.reshape(1, hw))
    return o.reshape(bsz * s_len, DA_C)


def _mixout_kernel(x_ref, y_ref, g_ref, bonus_ref, da_ref, lng_ref, lnb_ref, bd_ref, wrw_ref, wda_ref, o_ref):
    bd = bd_ref[...]
    y = y_ref[0] + y_ref[1]
    mu = _group_sum(y, bd) * (1.0 / RW_N)
    yc = y - mu
    var = _group_sum(yc * yc, bd) * (1.0 / RW_N)
    yn = yc * lax.rsqrt(var + GN_EPS) * lng_ref[...] + lnb_ref[...]
    y_rw = ((yn + bonus_ref[...]) * g_ref[...]).astype(BF16)
    mixed = _dot(y_rw, wrw_ref[...]) + _dot(da_ref[...].astype(BF16), wda_ref[...])
    o_ref[...] = x_ref[...] + mixed


def _mixout(x1, y, g, bonus, o_da, ln_g, ln_b, bd, w_out):
    t = x1.shape[0]
    tm = min(TOKEN_TILE, t)
    const2 = lambda i: (0, 0)
    tok = pl.BlockSpec((tm, RW_C), lambda i: (i, 0))
    vec = pl.BlockSpec((1, RW_C), const2)
    return pl.pallas_call(
        _mixout_kernel,
        grid=(t // tm,),
        in_specs=[
            pl.BlockSpec((tm, D_MODEL), lambda i: (i, 0)),
            pl.BlockSpec((2, tm, RW_C), lambda i: (0, i, 0)),
            tok, tok, tok, vec, vec,
            pl.BlockSpec((RW_C, RW_C), const2),
            pl.BlockSpec((RW_C, D_MODEL), const2),
            pl.BlockSpec((DA_C, D_MODEL), const2),
        ],
        out_specs=pl.BlockSpec((tm, D_MODEL), lambda i: (i, 0)),
        out_shape=jax.ShapeDtypeStruct((t, D_MODEL), F32),
        compiler_params=_params(("parallel",), 48),
        name="mixout",
    )(x1, y, g, bonus, o_da, ln_g.reshape(1, RW_C), ln_b.reshape(1, RW_C), bd,
      w_out[:RW_C].astype(BF16), w_out[RW_C:].astype(BF16))


def _encoder_layer(x, p):
    bsz, s_len, _ = x.shape
    t = bsz * s_len
    xf = x.reshape(t, D_MODEL)
    lane = jnp.arange(RW_C) // RW_N
    bd = (lane[:, None] == lane[None, :]).astype(BF16)

    x1 = _ffn(xf, p["ffn1_norm"], p["ffn1_w_gu"], p["ffn1_w_down"])
    proj_rw, proj_da = _proj(x1, p["mix_norm"], p["w_in"])
    r, v, kk, g, bonus, lw, kd, b = _rwprep(
        proj_rw, s_len, p["conv_w"], p["rw_w0"], p["rw_w_up"], p["rw_a0"], p["rw_a_up"], p["rw_g_up"],
        p["rw_k_k"], p["rw_k_a"], p["rw_r_k"], bd)
    y = _scan(r, v, kk, lw, kd, b, bsz, s_len)
    q, k, va = _daprep(proj_da, s_len, p["da_q_norm"], p["da_k_norm"], bd)
    o_da = _attn(q, k, va, bsz, s_len, p["da_lq1"], p["da_lk1"], p["da_lq2"], p["da_lk2"], p["da_subln"])
    x2 = _mixout(x1, y, g, bonus, o_da, p["rw_ln_g"], p["rw_ln_b"], bd, p["w_out"])
    out = _ffn(x2, p["ffn2_norm"], p["ffn2_w_gu"], p["ffn2_w_down"], p["final_norm"])
    return out.reshape(bsz, s_len, D_MODEL)


def kernel(x_prompt, x_sample, ffn1_norm, ffn1_w_gu, ffn1_w_down, mix_norm, w_in, conv_w, rw_w0, rw_w_up, rw_a0, rw_a_up, rw_g_up, rw_k_k, rw_k_a, rw_r_k, rw_ln_g, rw_ln_b, da_q_norm, da_k_norm, da_lq1, da_lk1, da_lq2, da_lk2, da_subln, w_out, ffn2_norm, ffn2_w_gu, ffn2_w_down, final_norm):
    names = ("ffn1_norm", "ffn1_w_gu", "ffn1_w_down", "mix_norm", "w_in", "conv_w", "rw_w0", "rw_w_up", "rw_a0",
             "rw_a_up", "rw_g_up", "rw_k_k", "rw_k_a", "rw_r_k", "rw_ln_g", "rw_ln_b", "da_q_norm", "da_k_norm",
             "da_lq1", "da_lk1", "da_lq2", "da_lk2", "da_subln", "w_out", "ffn2_norm", "ffn2_w_gu", "ffn2_w_down",
             "final_norm")
    vals = (ffn1_norm, ffn1_w_gu, ffn1_w_down, mix_norm, w_in, conv_w, rw_w0, rw_w_up, rw_a0, rw_a_up, rw_g_up,
            rw_k_k, rw_k_a, rw_r_k, rw_ln_g, rw_ln_b, da_q_norm, da_k_norm, da_lq1, da_lk1, da_lq2, da_lk2,
            da_subln, w_out, ffn2_norm, ffn2_w_gu, ffn2_w_down, final_norm)
    assert all(a.shape[0] == 1 for a in vals), "single-layer (depth 1) parameter stacks expected"
    p = {n: a[0] for n, a in zip(names, vals)}
    return (_encoder_layer(x_prompt, p), _encoder_layer(x_sample, p))
```

```python
import functools
import math

import jax
import jax.numpy as jnp
from jax import lax
from jax.experimental import pallas as pl
from jax.experimental.pallas import tpu as pltpu

F32 = jnp.float32
BF16 = jnp.bfloat16

D_MODEL = 1024
D_FF = 2816
RW_HEADS = 8
RW_N = 64
RW_C = RW_HEADS * RW_N
LORA_W = 64
LORA_A = 64
LORA_G = 128
DECAY_SCALE = 0.606531
GN_EPS = 64e-5
DA_HEADS = 4
DA_DH = 64
DA_C = DA_HEADS * 2 * DA_DH
ROT_DIM = DA_DH // 4
ROPE_THETA = 500000.0
NORM_EPS = 1e-6
RW_COLS = 3 * RW_C + 2 * LORA_W + 2 * LORA_A + LORA_G
LAM_INIT = 0.8 - 0.6 * math.exp(-0.3 * 0)

V7X_VMEM_BYTES = 64 * 1024 * 1024
SUBLANES = 8

FF_CHUNK = 256
N_FF_CHUNKS = D_FF // FF_CHUNK
TOKEN_TILE = 512
PREP_TILE = 256
CHUNK = 64
ATT_TQ = 256
ATT_TK = 512


def _params(sem, vmem_mb):
    return pltpu.CompilerParams(dimension_semantics=sem, vmem_limit_bytes=vmem_mb * 1024 * 1024)


def _dot(a, b):
    return jnp.dot(a, b, preferred_element_type=F32)


def _dot_nt(a, b):
    return lax.dot_general(a, b, (((1,), (1,)), ((), ())), preferred_element_type=F32)


def _dot_tn(a, b):
    return lax.dot_general(a, b, (((0,), (0,)), ((), ())), preferred_element_type=F32)


def _split2(x):
    hi = x.astype(BF16)
    lo = (x - hi.astype(F32)).astype(BF16)
    return hi, lo


def _split3(x):
    hi = x.astype(BF16)
    r1 = x - hi.astype(F32)
    mid = r1.astype(BF16)
    lo = (r1 - mid.astype(F32)).astype(BF16)
    return hi, mid, lo


def _group_sum(x, bd):
    hi, lo = _split2(x)
    return _dot(hi, bd) + _dot(lo, bd)


def _rms(x, g, eps=NORM_EPS):
    return x * lax.rsqrt(jnp.mean(x * x, axis=-1, keepdims=True) + eps) * g


def _ffn_kernel(x_ref, g_ref, wg_ref, wu_ref, wd_ref, *rest, final):
    if final:
        fg_ref, o_ref = rest
    else:
        (o_ref,) = rest
    x = x_ref[...]
    h = _rms(x, g_ref[...]).astype(BF16)
    acc = jnp.zeros_like(x)
    for j in range(N_FF_CHUNKS):
        gate = _dot(h, wg_ref[j])
        up = _dot(h, wu_ref[j])
        act = (gate * jax.nn.sigmoid(gate) * up).astype(BF16)
        acc = acc + _dot(act, wd_ref[j])
    y = x + 0.5 * acc
    if final:
        y = _rms(y, fg_ref[...])
    o_ref[...] = y


def _ffn(x, norm_g, w_gu, w_down, final_g=None):
    t = x.shape[0]
    tm = min(TOKEN_TILE, t)
    wg = w_gu[:, :D_FF].reshape(D_MODEL, N_FF_CHUNKS, FF_CHUNK).transpose(1, 0, 2).astype(BF16)
    wu = w_gu[:, D_FF:].reshape(D_MODEL, N_FF_CHUNKS, FF_CHUNK).transpose(1, 0, 2).astype(BF16)
    wd = w_down.reshape(N_FF_CHUNKS, FF_CHUNK, D_MODEL).astype(BF16)
    const3 = lambda i: (0, 0, 0)
    const2 = lambda i: (0, 0)
    in_specs = [
        pl.BlockSpec((tm, D_MODEL), lambda i: (i, 0)),
        pl.BlockSpec((1, D_MODEL), const2),
        pl.BlockSpec((N_FF_CHUNKS, D_MODEL, FF_CHUNK), const3),
        pl.BlockSpec((N_FF_CHUNKS, D_MODEL, FF_CHUNK), const3),
        pl.BlockSpec((N_FF_CHUNKS, FF_CHUNK, D_MODEL), const3),
    ]
    args = [x, norm_g.reshape(1, D_MODEL), wg, wu, wd]
    if final_g is not None:
        in_specs.append(pl.BlockSpec((1, D_MODEL), const2))
        args.append(final_g.reshape(1, D_MODEL))
    return pl.pallas_call(
        functools.partial(_ffn_kernel, final=final_g is not None),
        grid=(t // tm,),
        in_specs=in_specs,
        out_specs=pl.BlockSpec((tm, D_MODEL), lambda i: (i, 0)),
        out_shape=jax.ShapeDtypeStruct((t, D_MODEL), F32),
        compiler_params=_params(("parallel",), 56),
        name="ffn_final" if final_g is not None else "ffn",
    )(*args)


def _proj_kernel(x_ref, g_ref, wrw_ref, wda_ref, rw_ref, da_ref):
    h = _rms(x_ref[...], g_ref[...]).astype(BF16)
    rw_ref[...] = _dot(h, wrw_ref[...])
    da_ref[...] = _dot(h, wda_ref[...])


def _proj(x, norm_g, w_in):
    t = x.shape[0]
    tm = min(TOKEN_TILE, t)
    w_rw = w_in[:, :RW_COLS].astype(BF16)
    w_da = w_in[:, RW_COLS:].astype(BF16)
    const2 = lambda i: (0, 0)
    return pl.pallas_call(
        _proj_kernel,
        grid=(t // tm,),
        in_specs=[
            pl.BlockSpec((tm, D_MODEL), lambda i: (i, 0)),
            pl.BlockSpec((1, D_MODEL), const2),
            pl.BlockSpec((D_MODEL, RW_COLS), const2),
            pl.BlockSpec((D_MODEL, 3 * DA_C), const2),
        ],
        out_specs=[
            pl.BlockSpec((tm, RW_COLS), lambda i: (i, 0)),
            pl.BlockSpec((tm, 3 * DA_C), lambda i: (i, 0)),
        ],
        out_shape=[
            jax.ShapeDtypeStruct((t, RW_COLS), F32),
            jax.ShapeDtypeStruct((t, 3 * DA_C), F32),
        ],
        compiler_params=_params(("parallel",), 48),
        name="proj",
    )(x, norm_g.reshape(1, D_MODEL), w_rw, w_da)


def _rwprep_kernel(u_ref, up_ref, un_ref, conv_ref, w0_ref, wup_ref, a0_ref, aup_ref, gup_ref,
                   kk_ref, ka_ref, rk_ref, bd_ref,
                   r_o, v_o, kk_o, g_o, bonus_o, lw_o, kd_o, b_o, *, tm, s_len):
    i = pl.program_id(0)
    u = u_ref[...]
    start = i * tm
    first = (start % s_len) == 0
    last = ((start + tm) % s_len) == 0
    prev_row = jnp.where(first, 0.0, up_ref[SUBLANES - 1:SUBLANES, :])
    next_row = jnp.where(last, 0.0, un_ref[0:1, :])
    row = lax.broadcasted_iota(jnp.int32, (tm, 1), 0)
    prev = jnp.where(row == 0, prev_row, pltpu.roll(u, 1, 0))
    nxt = jnp.where(row == tm - 1, next_row, pltpu.roll(u, tm - 1, 0))
    c = prev * conv_ref[0:1, :] + u * conv_ref[1:2, :] + nxt * conv_ref[2:3, :]

    o_w = 3 * RW_C
    o_a = o_w + 2 * LORA_W
    o_g = o_a + 2 * LORA_A
    r = c[:, 0:RW_C]
    k = c[:, RW_C:2 * RW_C]
    v = c[:, 2 * RW_C:3 * RW_C]
    tw = jnp.tanh(c[:, o_w:o_a]).astype(BF16)
    ad = c[:, o_a:o_g].astype(BF16)
    gd = c[:, o_g:RW_COLS]
    bd = bd_ref[...]

    kkv = k * kk_ref[...]
    norm = jnp.sqrt(_group_sum(kkv * kkv, bd))
    kk = kkv / jnp.maximum(norm, 1e-12)
    g = _dot(jax.nn.sigmoid(gd).astype(BF16), gup_ref[...])

    ksum = jnp.zeros_like(k)
    for d in range(2):
        lw = -DECAY_SCALE * jax.nn.sigmoid(w0_ref[d:d + 1, :] + _dot(tw, wup_ref[d]))
        a = jax.nn.sigmoid(a0_ref[d:d + 1, :] + _dot(ad, aup_ref[d]))
        kd = k * (1.0 + (a - 1.0) * ka_ref[...])
        lw_o[d] = lw
        kd_o[d] = kd
        b_o[d] = kk * a
        ksum = ksum + kd
    coef = _group_sum(r * ksum * rk_ref[...], bd)
    r_o[...] = r
    v_o[...] = v
    kk_o[...] = kk
    g_o[...] = g
    bonus_o[...] = coef * v


def _pad_dir_rows(w_up):
    r = w_up.shape[1]
    z = jnp.zeros_like(w_up[0])
    return jnp.stack([jnp.concatenate([w_up[0], z], 0), jnp.concatenate([z, w_up[1]], 0)]).astype(BF16)


def _rwprep(proj_rw, s_len, conv_w, w0, w_up, a0, a_up, g_up, k_k, k_a, r_k, bd):
    t = proj_rw.shape[0]
    tm = min(PREP_TILE, s_len)
    nb8 = t // SUBLANES
    per = tm // SUBLANES
    const2 = lambda i: (0, 0)
    const3 = lambda i: (0, 0, 0)
    tok = pl.BlockSpec((tm, RW_C), lambda i: (i, 0))
    tok2 = pl.BlockSpec((2, tm, RW_C), lambda i: (0, i, 0))
    vec = pl.BlockSpec((1, RW_C), const2)
    one = jax.ShapeDtypeStruct((t, RW_C), F32)
    two = jax.ShapeDtypeStruct((2, t, RW_C), F32)
    return pl.pallas_call(
        functools.partial(_rwprep_kernel, tm=tm, s_len=s_len),
        grid=(t // tm,),
        in_specs=[
            pl.BlockSpec((tm, RW_COLS), lambda i: (i, 0)),
            pl.BlockSpec((SUBLANES, RW_COLS), lambda i: (jnp.maximum(i * per - 1, 0), 0)),
            pl.BlockSpec((SUBLANES, RW_COLS), lambda i: (jnp.minimum((i + 1) * per, nb8 - 1), 0)),
            pl.BlockSpec((3, RW_COLS), const2),
            pl.BlockSpec((2, RW_C), const2),
            pl.BlockSpec((2, 2 * LORA_W, RW_C), const3),
            pl.BlockSpec((2, RW_C), const2),
            pl.BlockSpec((2, 2 * LORA_A, RW_C), const3),
            pl.BlockSpec((LORA_G, RW_C), const2),
            vec, vec, vec,
            pl.BlockSpec((RW_C, RW_C), const2),
        ],
        out_specs=[tok, tok, tok, tok, tok, tok2, tok2, tok2],
        out_shape=[one, one, one, one, one, two, two, two],
        compiler_params=_params(("parallel",), 48),
        name="rwprep",
    )(proj_rw, proj_rw, proj_rw, conv_w, w0, _pad_dir_rows(w_up), a0, _pad_dir_rows(a_up),
      g_up.astype(BF16), k_k.reshape(1, RW_C), k_a.reshape(1, RW_C), r_k.reshape(1, RW_C), bd)


def _scan_kernel(rf_ref, vf_ref, kkf_ref, lwf_ref, kdf_ref, bf_ref, rr_ref, vr_ref, kkr_ref, lwr_ref, kdr_ref, br_ref,
                 yf_ref, yr_ref, h_ref, *, chunk):
    L = chunk
    L2 = 2 * L
    PW = 2 * RW_N
    n_pairs = RW_C // PW

    @pl.when(pl.program_id(1) == 0)
    def _():
        h_ref[...] = jnp.zeros_like(h_ref)

    row = lax.broadcasted_iota(jnp.int32, (L, L), 0)
    col = lax.broadcasted_iota(jnp.int32, (L, L), 1)
    row2 = lax.broadcasted_iota(jnp.int32, (L2, L2), 0)
    col2 = lax.broadcasted_iota(jnp.int32, (L2, L2), 1)
    same = (row2 < L) == (col2 < L)
    eye2 = (row2 == col2).astype(F32)
    first_head = lax.broadcasted_iota(jnp.int32, (L, PW), 1) < RW_N

    def stack(z):
        return jnp.concatenate([jnp.where(first_head, z, 0.0), jnp.where(first_head, 0.0, z)], axis=0)

    chains = []
    dirs = ((rf_ref, vf_ref, kkf_ref, lwf_ref, kdf_ref, bf_ref, yf_ref, 1),
            (rr_ref, vr_ref, kkr_ref, lwr_ref, kdr_ref, br_ref, yr_ref, -1))
    for d, (r_ref, v_ref, kk_ref, lw_ref, kd_ref, b_ref, y_ref, sgn) in enumerate(dirs):
        incl2 = same & ((row2 - col2) * sgn >= 0)
        strict2 = same & ((row2 - col2) * sgn > 0)
        tri = ((row - col) * sgn >= 0).astype(BF16)
        lw = lw_ref[...]
        hi, mid, lo = _split3(lw)
        c_in = _dot(tri, hi) + _dot(tri, mid) + _dot(tri, lo)
        c_tot = jnp.sum(lw, axis=0, keepdims=True)
        e_neg = jnp.exp(-c_in)
        e_end = jnp.exp(c_tot - c_in)
        kk = kk_ref[...]
        b = b_ref[...]
        kd = kd_ref[...]
        a_t = -kk * jnp.exp(c_in - lw)
        r_t = r_ref[...] * jnp.exp(c_in)
        b_t = b * e_neg
        k_t = kd * e_neg
        b_p = b * e_end
        k_p = kd * e_end
        p_end = jnp.exp(c_tot)
        v = v_ref[...]
        for p in range(n_pairs):
            ls = slice(p * PW, (p + 1) * PW)
            chains.append(dict(
                d=d, p=p, ls=ls, y_ref=y_ref, incl=incl2, strict=strict2,
                aa=stack(a_t[:, ls]).astype(BF16), rr=stack(r_t[:, ls]).astype(BF16),
                bb=stack(b_t[:, ls]).astype(BF16), kk=stack(k_t[:, ls]).astype(BF16),
                vv=stack(v[:, ls]).astype(BF16),
                bpT=jnp.transpose(stack(b_p[:, ls])).astype(BF16),
                kpT=jnp.transpose(stack(k_p[:, ls])).astype(BF16),
                pe_col=jnp.transpose(jnp.broadcast_to(p_end[:, ls], (PW, PW)))))

    for c in chains:
        x = _dot_nt(jnp.concatenate([c["aa"], c["rr"]], axis=0), jnp.concatenate([c["bb"], c["kk"]], axis=0))
        c["a_ab"] = jnp.where(c["strict"], x[:L2, :L2], 0.0)
        c["a_ak"] = jnp.where(c["strict"], x[:L2, L2:], 0.0).astype(BF16)
        c["a_rb"] = jnp.where(c["incl"], x[L2:, :L2], 0.0).astype(BF16)
        c["a_rk"] = jnp.where(c["incl"], x[L2:, L2:], 0.0).astype(BF16)
        c["inv"] = eye2 + c["a_ab"]
        c["pw"] = c["a_ab"]
    for _ in range(int(math.log2(L)) - 1):
        for c in chains:
            pwb = c["pw"].astype(BF16)
            c["pw"] = _dot(pwb, pwb)
        for c in chains:
            c["inv"] = c["inv"] + _dot(c["inv"].astype(BF16), c["pw"].astype(BF16))
    for c in chains:
        c["akv"] = _dot(c["a_ak"], c["vv"]).astype(BF16)
    for c in chains:
        inv = c["inv"].astype(BF16)
        c["w_t"] = _dot(inv, c["aa"]).astype(BF16)
        c["u_t"] = _dot(inv, c["akv"])
    for c in chains:
        c["h"] = h_ref[c["d"], c["p"]]
        c["hb"] = c["h"].astype(BF16)
        c["uu"] = (_dot(c["w_t"], c["hb"]) + c["u_t"]).astype(BF16)
    for c in chains:
        h_ref[c["d"], c["p"]] = c["pe_col"] * c["h"] + _dot(c["bpT"], c["uu"]) + _dot(c["kpT"], c["vv"])
    for c in chains:
        y2 = _dot(c["rr"], c["hb"]) + _dot(c["a_rb"], c["uu"]) + _dot(c["a_rk"], c["vv"])
        c["y_ref"][:, c["ls"]] = y2[:L, :] + y2[L:, :]


def _scan(r, v, kk, lw, kd, b, bsz, s_len):
    L = CHUNK
    nc = s_len // L
    shp = lambda a: a.reshape(bsz, s_len, RW_C)
    shp2 = lambda a: a.reshape(2, bsz, s_len, RW_C)
    f_sh = pl.BlockSpec((None, L, RW_C), lambda bi, c: (bi, c, 0))
    r_sh = pl.BlockSpec((None, L, RW_C), lambda bi, c: (bi, nc - 1 - c, 0))
    f_pd = pl.BlockSpec((None, None, L, RW_C), lambda bi, c: (0, bi, c, 0))
    r_pd = pl.BlockSpec((None, None, L, RW_C), lambda bi, c: (1, bi, nc - 1 - c, 0))
    out = jax.ShapeDtypeStruct((bsz, s_len, RW_C), F32)
    r3, v3, kk3, lw4, kd4, b4 = shp(r), shp(v), shp(kk), shp2(lw), shp2(kd), shp2(b)
    yf, yr = pl.pallas_call(
        functools.partial(_scan_kernel, chunk=L),
        grid=(bsz, nc),
        in_specs=[f_sh, f_sh, f_sh, f_pd, f_pd, f_pd, r_sh, r_sh, r_sh, r_pd, r_pd, r_pd],
        out_specs=[f_sh, r_sh],
        out_shape=[out, out],
        scratch_shapes=[pltpu.VMEM((2, RW_C // (2 * RW_N), 2 * RW_N, 2 * RW_N), F32)],
        compiler_params=_params(("parallel", "arbitrary"), 32),
        name="scan",
    )(r3, v3, kk3, lw4, kd4, b4, r3, v3, kk3, lw4, kd4, b4)
    return yf.reshape(bsz * s_len, RW_C), yr.reshape(bsz * s_len, RW_C)


def _daprep_kernel(p_ref, cf_ref, sn_ref, sp_ref, qn_ref, kn_ref, bd_ref, q_o, k_o, v_o):
    bd = bd_ref[...]
    cf, sn, sp = cf_ref[...], sn_ref[...], sp_ref[...]

    def norm_rope(x, g):
        ms = _group_sum(x * x, bd) * (1.0 / DA_DH)
        xn = x * lax.rsqrt(ms + NORM_EPS) * g
        half = ROT_DIM // 2
        return xn * cf + pltpu.roll(xn, DA_C - half, 1) * sn + pltpu.roll(xn, half, 1) * sp

    q = norm_rope(p_ref[:, 0:DA_C], qn_ref[...]) * (DA_DH ** -0.5)
    k = norm_rope(p_ref[:, DA_C:2 * DA_C], kn_ref[...])
    q_o[...] = q.astype(BF16)
    k_o[...] = k.astype(BF16)
    v_o[...] = p_ref[:, 2 * DA_C:3 * DA_C].astype(BF16)


def _rope_lane_tables(s_len):
    half = ROT_DIM // 2
    inv_freq = ROPE_THETA ** (-jnp.arange(0, ROT_DIM, 2, dtype=F32) / ROT_DIM)
    ang = jnp.arange(s_len, dtype=F32)[:, None] * inv_freq[None, :]
    cos, sin = jnp.cos(ang), jnp.sin(ang)
    j = jnp.arange(DA_C) % DA_DH
    f = j % half
    cf = jnp.where(j < ROT_DIM, cos[:, f], 1.0)
    sn = jnp.where(j < half, -sin[:, f], 0.0)
    sp = jnp.where((j >= half) & (j < ROT_DIM), sin[:, f], 0.0)
    return cf, sn, sp


def _daprep(proj_da, s_len, q_norm, k_norm, bd):
    t = proj_da.shape[0]
    tm = min(PREP_TILE, s_len)
    per_seq = s_len // tm
    cf, sn, sp = _rope_lane_tables(s_len)
    const2 = lambda i: (0, 0)
    tok = pl.BlockSpec((tm, DA_C), lambda i: (i, 0))
    tab = pl.BlockSpec((tm, DA_C), lambda i: (i % per_seq, 0))
    vec = pl.BlockSpec((1, DA_C), const2)
    out = jax.ShapeDtypeStruct((t, DA_C), BF16)
    reps = DA_C // DA_DH
    return pl.pallas_call(
        _daprep_kernel,
        grid=(t // tm,),
        in_specs=[pl.BlockSpec((tm, 3 * DA_C), lambda i: (i, 0)), tab, tab, tab, vec, vec,
                  pl.BlockSpec((DA_C, DA_C), const2)],
        out_specs=[tok, tok, tok],
        out_shape=[out, out, out],
        compiler_params=_params(("parallel",), 32),
        name="daprep",
    )(proj_da, cf, sn, sp, jnp.tile(q_norm, reps).reshape(1, DA_C), jnp.tile(k_norm, reps).reshape(1, DA_C), bd)


def _attn_kernel(q_ref, k_ref, v_ref, lq1_ref, lk1_ref, lq2_ref, lk2_ref, sub_ref, o_ref,
                 qs_ref, m_ref, l_ref, acc_ref, *, tq):
    ki = pl.program_id(3)

    @pl.when(ki == 0)
    def _():
        q = q_ref[...]
        lane = lax.broadcasted_iota(jnp.int32, q.shape, 1)
        zero = jnp.zeros_like(q)
        qs_ref[0:tq, :] = jnp.where(lane < DA_DH, q, zero)
        qs_ref[tq:2 * tq, :] = jnp.where(lane >= DA_DH, q, zero)
        m_ref[...] = jnp.full_like(m_ref, -jnp.inf)
        l_ref[...] = jnp.zeros_like(l_ref)
        acc_ref[...] = jnp.zeros_like(acc_ref)

    s = _dot_nt(qs_ref[...], k_ref[...])
    m_prev = m_ref[...]
    m_new = jnp.maximum(m_prev, jnp.max(s, axis=1, keepdims=True))
    p = jnp.exp(s - m_new)
    corr = jnp.exp(m_prev - m_new)
    l_ref[...] = corr * l_ref[...] + jnp.sum(p, axis=1, keepdims=True)
    acc_ref[...] = corr * acc_ref[...] + _dot(p.astype(BF16), v_ref[...])
    m_ref[...] = m_new

    @pl.when(ki == pl.num_programs(3) - 1)
    def _():
        o2 = acc_ref[...] / l_ref[...]
        lam = (jnp.exp(jnp.sum(lq1_ref[...] * lk1_ref[...], axis=1, keepdims=True))
               - jnp.exp(jnp.sum(lq2_ref[...] * lk2_ref[...], axis=1, keepdims=True))) + LAM_INIT
        o = o2[0:tq, :] - lam * o2[tq:2 * tq, :]
        o_ref[...] = _rms(o, sub_ref[...]) * (1.0 - LAM_INIT)


def _attn(q, k, v, bsz, s_len, lq1, lk1, lq2, lk2, subln):
    tq = min(ATT_TQ, s_len)
    tk = min(ATT_TK, s_len)
    hw = 2 * DA_DH
    shp = lambda a: a.reshape(bsz, s_len, DA_C)
    lvec = pl.BlockSpec((1, DA_DH), lambda b, h, qi, ki: (0, 0))
    o = pl.pallas_call(
        functools.partial(_attn_kernel, tq=tq),
        grid=(bsz, DA_HEADS, s_len // tq, s_len // tk),
        in_specs=[
            pl.BlockSpec((None, tq, hw), lambda b, h, qi, ki: (b, qi, h)),
            pl.BlockSpec((None, tk, hw), lambda b, h, qi, ki: (b, ki, h)),
            pl.BlockSpec((None, tk, hw), lambda b, h, qi, ki: (b, ki, h)),
            lvec, lvec, lvec, lvec,
            pl.BlockSpec((1, hw), lambda b, h, qi, ki: (0, 0)),
        ],
        out_specs=pl.BlockSpec((None, tq, hw), lambda b, h, qi, ki: (b, qi, h)),
        out_shape=jax.ShapeDtypeStruct((bsz, s_len, DA_C), F32),
        scratch_shapes=[
            pltpu.VMEM((2 * tq, hw), BF16),
            pltpu.VMEM((2 * tq, 1), F32),
            pltpu.VMEM((2 * tq, 1), F32),
            pltpu.VMEM((2 * tq, hw), F32),
        ],
        compiler_params=_params(("parallel", "parallel", "parallel", "arbitrary"), 32),
        name="attn",
    )(shp(q), shp(k), shp(v), lq1.reshape(1, DA_DH), lk1.reshape(1, DA_DH), lq2.reshape(1, DA_DH),
      lk2.reshape(1, DA_DH), subln.reshape(1, hw))
    return o.reshape(bsz * s_len, DA_C)


def _mixout_kernel(x_ref, yf_ref, yr_ref, g_ref, bonus_ref, da_ref, lng_ref, lnb_ref, bd_ref, wrw_ref, wda_ref, o_ref):
    bd = bd_ref[...]
    y = yf_ref[...] + yr_ref[...]
    mu = _group_sum(y, bd) * (1.0 / RW_N)
    yc = y - mu
    var = _group_sum(yc * yc, bd) * (1.0 / RW_N)
    yn = yc * lax.rsqrt(var + GN_EPS) * lng_ref[...] + lnb_ref[...]
    y_rw = ((yn + bonus_ref[...]) * g_ref[...]).astype(BF16)
    mixed = _dot(y_rw, wrw_ref[...]) + _dot(da_ref[...].astype(BF16), wda_ref[...])
    o_ref[...] = x_ref[...] + mixed


def _mixout(x1, yf, yr, g, bonus, o_da, ln_g, ln_b, bd, w_out):
    t = x1.shape[0]
    tm = min(TOKEN_TILE, t)
    const2 = lambda i: (0, 0)
    tok = pl.BlockSpec((tm, RW_C), lambda i: (i, 0))
    vec = pl.BlockSpec((1, RW_C), const2)
    return pl.pallas_call(
        _mixout_kernel,
        grid=(t // tm,),
        in_specs=[
            pl.BlockSpec((tm, D_MODEL), lambda i: (i, 0)),
            tok, tok, tok, tok, tok, vec, vec,
            pl.BlockSpec((RW_C, RW_C), const2),
            pl.BlockSpec((RW_C, D_MODEL), const2),
            pl.BlockSpec((DA_C, D_MODEL), const2),
        ],
        out_specs=pl.BlockSpec((tm, D_MODEL), lambda i: (i, 0)),
        out_shape=jax.ShapeDtypeStruct((t, D_MODEL), F32),
        compiler_params=_params(("parallel",), 48),
        name="mixout",
    )(x1, yf, yr, g, bonus, o_da, ln_g.reshape(1, RW_C), ln_b.reshape(1, RW_C), bd,
      w_out[:RW_C].astype(BF16), w_out[RW_C:].astype(BF16))


def _encoder_layer(x, p):
    bsz, s_len, _ = x.shape
    t = bsz * s_len
    xf = x.reshape(t, D_MODEL)
    lane = jnp.arange(RW_C) // RW_N
    bd = (lane[:, None] == lane[None, :]).astype(BF16)

    x1 = _ffn(xf, p["ffn1_norm"], p["ffn1_w_gu"], p["ffn1_w_down"])
    proj_rw, proj_da = _proj(x1, p["mix_norm"], p["w_in"])
    r, v, kk, g, bonus, lw, kd, b = _rwprep(
        proj_rw, s_len, p["conv_w"], p["rw_w0"], p["rw_w_up"], p["rw_a0"], p["rw_a_up"], p["rw_g_up"],
        p["rw_k_k"], p["rw_k_a"], p["rw_r_k"], bd)
    yf, yr = _scan(r, v, kk, lw, kd, b, bsz, s_len)
    q, k, va = _daprep(proj_da, s_len, p["da_q_norm"], p["da_k_norm"], bd)
    o_da = _attn(q, k, va, bsz, s_len, p["da_lq1"], p["da_lk1"], p["da_lq2"], p["da_lk2"], p["da_subln"])
    x2 = _mixout(x1, yf, yr, g, bonus, o_da, p["rw_ln_g"], p["rw_ln_b"], bd, p["w_out"])
    out = _ffn(x2, p["ffn2_norm"], p["ffn2_w_gu"], p["ffn2_w_down"], p["final_norm"])
    return out.reshape(bsz, s_len, D_MODEL)


def kernel(x_prompt, x_sample, ffn1_norm, ffn1_w_gu, ffn1_w_down, mix_norm, w_in, conv_w, rw_w0, rw_w_up, rw_a0, rw_a_up, rw_g_up, rw_k_k, rw_k_a, rw_r_k, rw_ln_g, rw_ln_b, da_q_norm, da_k_norm, da_lq1, da_lk1, da_lq2, da_lk2, da_subln, w_out, ffn2_norm, ffn2_w_gu, ffn2_w_down, final_norm):
    names = ("ffn1_norm", "ffn1_w_gu", "ffn1_w_down", "mix_norm", "w_in", "conv_w", "rw_w0", "rw_w_up", "rw_a0",
             "rw_a_up", "rw_g_up", "rw_k_k", "rw_k_a", "rw_r_k", "rw_ln_g", "rw_ln_b", "da_q_norm", "da_k_norm",
             "da_lq1", "da_lk1", "da_lq2", "da_lk2", "da_subln", "w_out", "ffn2_norm", "ffn2_w_gu", "ffn2_w_down",
             "final_norm")
    vals = (ffn1_norm, ffn1_w_gu, ffn1_w_down, mix_norm, w_in, conv_w, rw_w0, rw_w_up, rw_a0, rw_a_up, rw_g_up,
            rw_k_k, rw_k_a, rw_r_k, rw_ln_g, rw_ln_b, da_q_norm, da_k_norm, da_lq1, da_lk1, da_lq2, da_lk2,
            da_subln, w_out, ffn2_norm, ffn2_w_gu, ffn2_w_down, final_norm)
    assert all(a.shape[0] == 1 for a in vals), "single-layer (depth 1) parameter stacks expected"
    p = {n: a[0] for n, a in zip(names, vals)}
    return (_encoder_layer(x_prompt, p), _encoder_layer(x_sample, p))
```

```python
import functools
import math

import jax
import jax.numpy as jnp
from jax import lax
from jax.experimental import pallas as pl
from jax.experimental.pallas import tpu as pltpu

F32 = jnp.float32
BF16 = jnp.bfloat16

D_MODEL = 1024
D_FF = 2816
RW_HEADS = 8
RW_N = 64
RW_C = RW_HEADS * RW_N
LORA_W = 64
LORA_A = 64
LORA_G = 128
DECAY_SCALE = 0.606531
GN_EPS = 64e-5
DA_HEADS = 4
DA_DH = 64
DA_C = DA_HEADS * 2 * DA_DH
ROT_DIM = DA_DH // 4
ROPE_THETA = 500000.0
NORM_EPS = 1e-6
RW_COLS = 3 * RW_C + 2 * LORA_W + 2 * LORA_A + LORA_G
LAM_INIT = 0.8 - 0.6 * math.exp(-0.3 * 0)
LOG2E = math.log2(math.e)

V7X_VMEM_BYTES = 64 * 1024 * 1024
SUBLANES = 8

FF_CHUNK = 256
N_FF_CHUNKS = D_FF // FF_CHUNK
TOKEN_TILE = 512
PREP_TILE = 256
CHUNK = 64
ATT_TQ = 256
ATT_TK = 512


def _params(sem, vmem_mb):
    return pltpu.CompilerParams(dimension_semantics=sem, vmem_limit_bytes=vmem_mb * 1024 * 1024)


def _dot(a, b):
    return jnp.dot(a, b, preferred_element_type=F32)


def _dot_nt(a, b):
    return lax.dot_general(a, b, (((1,), (1,)), ((), ())), preferred_element_type=F32)


def _dot_tn(a, b):
    return lax.dot_general(a, b, (((0,), (0,)), ((), ())), preferred_element_type=F32)


def _split2(x):
    hi = x.astype(BF16)
    lo = (x - hi.astype(F32)).astype(BF16)
    return hi, lo


def _split3(x):
    hi = x.astype(BF16)
    r1 = x - hi.astype(F32)
    mid = r1.astype(BF16)
    lo = (r1 - mid.astype(F32)).astype(BF16)
    return hi, mid, lo


def _group_sum(x, bd):
    hi, lo = _split2(x)
    return _dot(hi, bd) + _dot(lo, bd)


def _rms(x, g, eps=NORM_EPS):
    return x * lax.rsqrt(jnp.mean(x * x, axis=-1, keepdims=True) + eps) * g


def _ffn_kernel(x_ref, g_ref, wg_ref, wu_ref, wd_ref, *rest, final):
    if final:
        fg_ref, o_ref = rest
    else:
        (o_ref,) = rest
    x = x_ref[...]
    h = _rms(x, g_ref[...]).astype(BF16)
    acc = jnp.zeros_like(x)
    for j in range(N_FF_CHUNKS):
        gate = _dot(h, wg_ref[j])
        up = _dot(h, wu_ref[j])
        act = (gate * jax.nn.sigmoid(gate) * up).astype(BF16)
        acc = acc + _dot(act, wd_ref[j])
    y = x + 0.5 * acc
    if final:
        y = _rms(y, fg_ref[...])
    o_ref[...] = y


def _ffn(x, norm_g, w_gu, w_down, final_g=None):
    t = x.shape[0]
    tm = min(TOKEN_TILE, t)
    wg = w_gu[:, :D_FF].reshape(D_MODEL, N_FF_CHUNKS, FF_CHUNK).transpose(1, 0, 2).astype(BF16)
    wu = w_gu[:, D_FF:].reshape(D_MODEL, N_FF_CHUNKS, FF_CHUNK).transpose(1, 0, 2).astype(BF16)
    wd = w_down.reshape(N_FF_CHUNKS, FF_CHUNK, D_MODEL).astype(BF16)
    const3 = lambda i: (0, 0, 0)
    const2 = lambda i: (0, 0)
    in_specs = [
        pl.BlockSpec((tm, D_MODEL), lambda i: (i, 0)),
        pl.BlockSpec((1, D_MODEL), const2),
        pl.BlockSpec((N_FF_CHUNKS, D_MODEL, FF_CHUNK), const3),
        pl.BlockSpec((N_FF_CHUNKS, D_MODEL, FF_CHUNK), const3),
        pl.BlockSpec((N_FF_CHUNKS, FF_CHUNK, D_MODEL), const3),
    ]
    args = [x, norm_g.reshape(1, D_MODEL), wg, wu, wd]
    if final_g is not None:
        in_specs.append(pl.BlockSpec((1, D_MODEL), const2))
        args.append(final_g.reshape(1, D_MODEL))
    return pl.pallas_call(
        functools.partial(_ffn_kernel, final=final_g is not None),
        grid=(t // tm,),
        in_specs=in_specs,
        out_specs=pl.BlockSpec((tm, D_MODEL), lambda i: (i, 0)),
        out_shape=jax.ShapeDtypeStruct((t, D_MODEL), F32),
        compiler_params=_params(("parallel",), 56),
        name="ffn_final" if final_g is not None else "ffn",
    )(*args)


def _proj_kernel(x_ref, g_ref, wrw_ref, wda_ref, rw_ref, da_ref):
    h = _rms(x_ref[...], g_ref[...]).astype(BF16)
    rw_ref[...] = _dot(h, wrw_ref[...])
    da_ref[...] = _dot(h, wda_ref[...])


def _proj(x, norm_g, w_in):
    t = x.shape[0]
    tm = min(TOKEN_TILE, t)
    w_rw = w_in[:, :RW_COLS].astype(BF16)
    w_da = w_in[:, RW_COLS:].astype(BF16)
    const2 = lambda i: (0, 0)
    return pl.pallas_call(
        _proj_kernel,
        grid=(t // tm,),
        in_specs=[
            pl.BlockSpec((tm, D_MODEL), lambda i: (i, 0)),
            pl.BlockSpec((1, D_MODEL), const2),
            pl.BlockSpec((D_MODEL, RW_COLS), const2),
            pl.BlockSpec((D_MODEL, 3 * DA_C), const2),
        ],
        out_specs=[
            pl.BlockSpec((tm, RW_COLS), lambda i: (i, 0)),
            pl.BlockSpec((tm, 3 * DA_C), lambda i: (i, 0)),
        ],
        out_shape=[
            jax.ShapeDtypeStruct((t, RW_COLS), F32),
            jax.ShapeDtypeStruct((t, 3 * DA_C), F32),
        ],
        compiler_params=_params(("parallel",), 48),
        name="proj",
    )(x, norm_g.reshape(1, D_MODEL), w_rw, w_da)


def _rwprep_kernel(u_ref, up_ref, un_ref, conv_ref, w0_ref, wup_ref, a0_ref, aup_ref, gup_ref,
                   kk_ref, ka_ref, rk_ref, bd_ref,
                   r_o, v_o, kk_o, g_o, bonus_o, lw_o, kd_o, b_o, *, tm, s_len):
    i = pl.program_id(0)
    u = u_ref[...]
    start = i * tm
    first = (start % s_len) == 0
    last = ((start + tm) % s_len) == 0
    prev_row = jnp.where(first, 0.0, up_ref[SUBLANES - 1:SUBLANES, :])
    next_row = jnp.where(last, 0.0, un_ref[0:1, :])
    row = lax.broadcasted_iota(jnp.int32, (tm, 1), 0)
    prev = jnp.where(row == 0, prev_row, pltpu.roll(u, 1, 0))
    nxt = jnp.where(row == tm - 1, next_row, pltpu.roll(u, tm - 1, 0))
    c = prev * conv_ref[0:1, :] + u * conv_ref[1:2, :] + nxt * conv_ref[2:3, :]

    o_w = 3 * RW_C
    o_a = o_w + 2 * LORA_W
    o_g = o_a + 2 * LORA_A
    r = c[:, 0:RW_C]
    k = c[:, RW_C:2 * RW_C]
    v = c[:, 2 * RW_C:3 * RW_C]
    tw = jnp.tanh(c[:, o_w:o_a]).astype(BF16)
    ad = c[:, o_a:o_g].astype(BF16)
    gd = c[:, o_g:RW_COLS]
    bd = bd_ref[...]

    kkv = k * kk_ref[...]
    norm = jnp.sqrt(_group_sum(kkv * kkv, bd))
    kk = kkv / jnp.maximum(norm, 1e-12)
    g = _dot(jax.nn.sigmoid(gd).astype(BF16), gup_ref[...])

    ksum = jnp.zeros_like(k)
    for d in range(2):
        lw = -DECAY_SCALE * jax.nn.sigmoid(w0_ref[d:d + 1, :] + _dot(tw, wup_ref[d]))
        a = jax.nn.sigmoid(a0_ref[d:d + 1, :] + _dot(ad, aup_ref[d]))
        kd = k * (1.0 + (a - 1.0) * ka_ref[...])
        lw_o[d] = lw
        kd_o[d] = kd
        b_o[d] = kk * a
        ksum = ksum + kd
    coef = _group_sum(r * ksum * rk_ref[...], bd)
    r_o[...] = r
    v_o[...] = v
    kk_o[...] = kk
    g_o[...] = g
    bonus_o[...] = coef * v


def _pad_dir_rows(w_up):
    r = w_up.shape[1]
    z = jnp.zeros_like(w_up[0])
    return jnp.stack([jnp.concatenate([w_up[0], z], 0), jnp.concatenate([z, w_up[1]], 0)]).astype(BF16)


def _rwprep(proj_rw, s_len, conv_w, w0, w_up, a0, a_up, g_up, k_k, k_a, r_k, bd):
    t = proj_rw.shape[0]
    tm = min(PREP_TILE, s_len)
    nb8 = t // SUBLANES
    per = tm // SUBLANES
    const2 = lambda i: (0, 0)
    const3 = lambda i: (0, 0, 0)
    tok = pl.BlockSpec((tm, RW_C), lambda i: (i, 0))
    tok2 = pl.BlockSpec((2, tm, RW_C), lambda i: (0, i, 0))
    vec = pl.BlockSpec((1, RW_C), const2)
    one = jax.ShapeDtypeStruct((t, RW_C), F32)
    two = jax.ShapeDtypeStruct((2, t, RW_C), F32)
    return pl.pallas_call(
        functools.partial(_rwprep_kernel, tm=tm, s_len=s_len),
        grid=(t // tm,),
        in_specs=[
            pl.BlockSpec((tm, RW_COLS), lambda i: (i, 0)),
            pl.BlockSpec((SUBLANES, RW_COLS), lambda i: (jnp.maximum(i * per - 1, 0), 0)),
            pl.BlockSpec((SUBLANES, RW_COLS), lambda i: (jnp.minimum((i + 1) * per, nb8 - 1), 0)),
            pl.BlockSpec((3, RW_COLS), const2),
            pl.BlockSpec((2, RW_C), const2),
            pl.BlockSpec((2, 2 * LORA_W, RW_C), const3),
            pl.BlockSpec((2, RW_C), const2),
            pl.BlockSpec((2, 2 * LORA_A, RW_C), const3),
            pl.BlockSpec((LORA_G, RW_C), const2),
            vec, vec, vec,
            pl.BlockSpec((RW_C, RW_C), const2),
        ],
        out_specs=[tok, tok, tok, tok, tok, tok2, tok2, tok2],
        out_shape=[one, one, one, one, one, two, two, two],
        compiler_params=_params(("parallel",), 48),
        name="rwprep",
    )(proj_rw, proj_rw, proj_rw, conv_w, w0, _pad_dir_rows(w_up), a0, _pad_dir_rows(a_up),
      g_up.astype(BF16), k_k.reshape(1, RW_C), k_a.reshape(1, RW_C), r_k.reshape(1, RW_C), bd)


def _scan_kernel(rf_ref, vf_ref, kkf_ref, lwf_ref, kdf_ref, bf_ref, rr_ref, vr_ref, kkr_ref, lwr_ref, kdr_ref, br_ref,
                 yf_ref, yr_ref, h_ref, *, chunk):
    L = chunk
    L2 = 2 * L
    PW = 2 * RW_N
    n_pairs = RW_C // PW

    @pl.when(pl.program_id(1) == 0)
    def _():
        h_ref[...] = jnp.zeros_like(h_ref)

    row = lax.broadcasted_iota(jnp.int32, (L, L), 0)
    col = lax.broadcasted_iota(jnp.int32, (L, L), 1)
    row2 = lax.broadcasted_iota(jnp.int32, (L2, L2), 0)
    col2 = lax.broadcasted_iota(jnp.int32, (L2, L2), 1)
    same = (row2 < L) == (col2 < L)
    eye2 = (row2 == col2).astype(F32)
    first_head = lax.broadcasted_iota(jnp.int32, (L, PW), 1) < RW_N

    def stack(z):
        return jnp.concatenate([jnp.where(first_head, z, 0.0), jnp.where(first_head, 0.0, z)], axis=0)

    chains = []
    dirs = ((rf_ref, vf_ref, kkf_ref, lwf_ref, kdf_ref, bf_ref, yf_ref, 1),
            (rr_ref, vr_ref, kkr_ref, lwr_ref, kdr_ref, br_ref, yr_ref, -1))
    for d, (r_ref, v_ref, kk_ref, lw_ref, kd_ref, b_ref, y_ref, sgn) in enumerate(dirs):
        incl2 = same & ((row2 - col2) * sgn >= 0)
        strict2 = same & ((row2 - col2) * sgn > 0)
        tri = ((row - col) * sgn >= 0).astype(BF16)
        lw = lw_ref[...]
        hi, mid, lo = _split3(lw)
        c_in = _dot(tri, hi) + _dot(tri, mid) + _dot(tri, lo)
        c_tot = jnp.sum(lw, axis=0, keepdims=True)
        e_neg = jnp.exp(-c_in)
        e_end = jnp.exp(c_tot - c_in)
        kk = kk_ref[...]
        b = b_ref[...]
        kd = kd_ref[...]
        a_t = -kk * jnp.exp(c_in - lw)
        r_t = r_ref[...] * jnp.exp(c_in)
        b_t = b * e_neg
        k_t = kd * e_neg
        b_p = b * e_end
        k_p = kd * e_end
        p_end = jnp.exp(c_tot)
        v = v_ref[...]
        for p in range(n_pairs):
            ls = slice(p * PW, (p + 1) * PW)
            chains.append(dict(
                d=d, p=p, ls=ls, y_ref=y_ref, incl=incl2, strict=strict2,
                aa=stack(a_t[:, ls]).astype(BF16), rr=stack(r_t[:, ls]).astype(BF16),
                bb=stack(b_t[:, ls]).astype(BF16), kk=stack(k_t[:, ls]).astype(BF16),
                vv=stack(v[:, ls]).astype(BF16),
                bpT=jnp.transpose(stack(b_p[:, ls])).astype(BF16),
                kpT=jnp.transpose(stack(k_p[:, ls])).astype(BF16),
                pe_col=jnp.transpose(jnp.broadcast_to(p_end[:, ls], (PW, PW)))))

    for c in chains:
        x = _dot_nt(jnp.concatenate([c["aa"], c["rr"]], axis=0), jnp.concatenate([c["bb"], c["kk"]], axis=0))
        c["a_ab"] = jnp.where(c["strict"], x[:L2, :L2], 0.0)
        c["a_ak"] = jnp.where(c["strict"], x[:L2, L2:], 0.0).astype(BF16)
        c["a_rb"] = jnp.where(c["incl"], x[L2:, :L2], 0.0).astype(BF16)
        c["a_rk"] = jnp.where(c["incl"], x[L2:, L2:], 0.0).astype(BF16)
        c["inv"] = eye2 + c["a_ab"]
        c["pw"] = c["a_ab"]
    for _ in range(int(math.log2(L)) - 1):
        for c in chains:
            pwb = c["pw"].astype(BF16)
            c["pw"] = _dot(pwb, pwb)
        for c in chains:
            c["inv"] = c["inv"] + _dot(c["inv"].astype(BF16), c["pw"].astype(BF16))
    for c in chains:
        c["akv"] = _dot(c["a_ak"], c["vv"]).astype(BF16)
    for c in chains:
        inv = c["inv"].astype(BF16)
        c["w_t"] = _dot(inv, c["aa"]).astype(BF16)
        c["u_t"] = _dot(inv, c["akv"])
    for c in chains:
        c["h"] = h_ref[c["d"], c["p"]]
        c["hb"] = c["h"].astype(BF16)
        c["uu"] = (_dot(c["w_t"], c["hb"]) + c["u_t"]).astype(BF16)
    for c in chains:
        h_ref[c["d"], c["p"]] = c["pe_col"] * c["h"] + _dot(c["bpT"], c["uu"]) + _dot(c["kpT"], c["vv"])
    for c in chains:
        y2 = _dot(c["rr"], c["hb"]) + _dot(c["a_rb"], c["uu"]) + _dot(c["a_rk"], c["vv"])
        c["y_ref"][:, c["ls"]] = y2[:L, :] + y2[L:, :]


def _scan(r, v, kk, lw, kd, b, bsz, s_len):
    L = CHUNK
    nc = s_len // L
    shp = lambda a: a.reshape(bsz, s_len, RW_C)
    shp2 = lambda a: a.reshape(2, bsz, s_len, RW_C)
    f_sh = pl.BlockSpec((None, L, RW_C), lambda bi, c: (bi, c, 0))
    r_sh = pl.BlockSpec((None, L, RW_C), lambda bi, c: (bi, nc - 1 - c, 0))
    f_pd = pl.BlockSpec((None, None, L, RW_C), lambda bi, c: (0, bi, c, 0))
    r_pd = pl.BlockSpec((None, None, L, RW_C), lambda bi, c: (1, bi, nc - 1 - c, 0))
    out = jax.ShapeDtypeStruct((bsz, s_len, RW_C), F32)
    r3, v3, kk3, lw4, kd4, b4 = shp(r), shp(v), shp(kk), shp2(lw), shp2(kd), shp2(b)
    yf, yr = pl.pallas_call(
        functools.partial(_scan_kernel, chunk=L),
        grid=(bsz, nc),
        in_specs=[f_sh, f_sh, f_sh, f_pd, f_pd, f_pd, r_sh, r_sh, r_sh, r_pd, r_pd, r_pd],
        out_specs=[f_sh, r_sh],
        out_shape=[out, out],
        scratch_shapes=[pltpu.VMEM((2, RW_C // (2 * RW_N), 2 * RW_N, 2 * RW_N), F32)],
        compiler_params=_params(("parallel", "arbitrary"), 32),
        name="scan",
    )(r3, v3, kk3, lw4, kd4, b4, r3, v3, kk3, lw4, kd4, b4)
    return yf.reshape(bsz * s_len, RW_C), yr.reshape(bsz * s_len, RW_C)


def _daprep_kernel(p_ref, cf_ref, sn_ref, sp_ref, qn_ref, kn_ref, bd_ref, q_o, k_o, v_o):
    bd = bd_ref[...]
    cf, sn, sp = cf_ref[...], sn_ref[...], sp_ref[...]

    def norm_rope(x, g):
        ms = _group_sum(x * x, bd) * (1.0 / DA_DH)
        xn = x * lax.rsqrt(ms + NORM_EPS) * g
        half = ROT_DIM // 2
        return xn * cf + pltpu.roll(xn, DA_C - half, 1) * sn + pltpu.roll(xn, half, 1) * sp

    q = norm_rope(p_ref[:, 0:DA_C], qn_ref[...]) * (DA_DH ** -0.5 * LOG2E)
    k = norm_rope(p_ref[:, DA_C:2 * DA_C], kn_ref[...])
    q_o[...] = q.astype(BF16)
    k_o[...] = k.astype(BF16)
    v_o[...] = jnp.transpose(p_ref[:, 2 * DA_C:3 * DA_C]).astype(BF16)


def _rope_lane_tables(s_len):
    half = ROT_DIM // 2
    inv_freq = ROPE_THETA ** (-jnp.arange(0, ROT_DIM, 2, dtype=F32) / ROT_DIM)
    ang = jnp.arange(s_len, dtype=F32)[:, None] * inv_freq[None, :]
    cos, sin = jnp.cos(ang), jnp.sin(ang)
    j = jnp.arange(DA_C) % DA_DH
    f = j % half
    cf = jnp.where(j < ROT_DIM, cos[:, f], 1.0)
    sn = jnp.where(j < half, -sin[:, f], 0.0)
    sp = jnp.where((j >= half) & (j < ROT_DIM), sin[:, f], 0.0)
    return cf, sn, sp


def _daprep(proj_da, bsz, s_len, q_norm, k_norm, bd):
    t = proj_da.shape[0]
    tm = min(ATT_TK, s_len)
    per_seq = s_len // tm
    cf, sn, sp = _rope_lane_tables(s_len)
    const2 = lambda i: (0, 0)
    tok = pl.BlockSpec((tm, DA_C), lambda i: (i, 0))
    tab = pl.BlockSpec((tm, DA_C), lambda i: (i % per_seq, 0))
    vec = pl.BlockSpec((1, DA_C), const2)
    out = jax.ShapeDtypeStruct((t, DA_C), BF16)
    reps = DA_C // DA_DH
    return pl.pallas_call(
        _daprep_kernel,
        grid=(t // tm,),
        in_specs=[pl.BlockSpec((tm, 3 * DA_C), lambda i: (i, 0)), tab, tab, tab, vec, vec,
                  pl.BlockSpec((DA_C, DA_C), const2)],
        out_specs=[tok, tok, pl.BlockSpec((None, None, DA_C, tm), lambda i: (i // per_seq, i % per_seq, 0, 0))],
        out_shape=[out, out, jax.ShapeDtypeStruct((bsz, per_seq, DA_C, tm), BF16)],
        compiler_params=_params(("parallel",), 32),
        name="daprep",
    )(proj_da, cf, sn, sp, jnp.tile(q_norm, reps).reshape(1, DA_C), jnp.tile(k_norm, reps).reshape(1, DA_C), bd)


def _attn_kernel(q_ref, k_ref, vt_ref, lq1_ref, lk1_ref, lq2_ref, lk2_ref, sub_ref, o_ref, acc_ref, *, tq, tk):
    q = q_ref[...]
    lane = lax.broadcasted_iota(jnp.int32, q.shape, 1)
    zero = jnp.zeros_like(q)
    qs = jnp.concatenate([jnp.where(lane < DA_DH, q, zero), jnp.where(lane >= DA_DH, q, zero)], axis=0)
    acc_ref[...] = jnp.zeros_like(acc_ref)

    def body(i, carry):
        m_prev, l_prev = carry
        kblk = k_ref[pl.ds(pl.multiple_of(i * tk, tk), tk), :]
        s = _dot_nt(kblk, qs)
        m_new = jnp.maximum(m_prev, jnp.max(s, axis=0, keepdims=True))
        p = jnp.exp2(s - m_new)
        corr = jnp.exp2(m_prev - m_new)
        acc_ref[...] = corr * acc_ref[...] + _dot(vt_ref[i], p.astype(BF16))
        return m_new, corr * l_prev + jnp.sum(p, axis=0, keepdims=True)

    init = (jnp.full((1, 2 * tq), -jnp.inf, F32), jnp.zeros((1, 2 * tq), F32))
    _, l = lax.fori_loop(0, k_ref.shape[0] // tk, body, init, unroll=2)

    o2 = acc_ref[...] / l
    lam = (jnp.exp(jnp.sum(lq1_ref[...] * lk1_ref[...], axis=1, keepdims=True))
           - jnp.exp(jnp.sum(lq2_ref[...] * lk2_ref[...], axis=1, keepdims=True))) + LAM_INIT
    o = jnp.transpose(o2[:, 0:tq] - lam * o2[:, tq:2 * tq])
    o_ref[...] = _rms(o, sub_ref[...]) * (1.0 - LAM_INIT)


def _attn(q, k, vt, bsz, s_len, lq1, lk1, lq2, lk2, subln):
    tq = min(ATT_TQ, s_len)
    tk = min(ATT_TK, s_len)
    nk = s_len // tk
    hw = 2 * DA_DH
    shp = lambda a: a.reshape(bsz, s_len, DA_C)
    lvec = pl.BlockSpec((1, DA_DH), lambda b, h, qi: (0, 0))
    o = pl.pallas_call(
        functools.partial(_attn_kernel, tq=tq, tk=tk),
        grid=(bsz, DA_HEADS, s_len // tq),
        in_specs=[
            pl.BlockSpec((None, tq, hw), lambda b, h, qi: (b, qi, h)),
            pl.BlockSpec((None, s_len, hw), lambda b, h, qi: (b, 0, h)),
            pl.BlockSpec((None, nk, hw, tk), lambda b, h, qi: (b, 0, h, 0)),
            lvec, lvec, lvec, lvec,
            pl.BlockSpec((1, hw), lambda b, h, qi: (0, 0)),
        ],
        out_specs=pl.BlockSpec((None, tq, hw), lambda b, h, qi: (b, qi, h)),
        out_shape=jax.ShapeDtypeStruct((bsz, s_len, DA_C), F32),
        scratch_shapes=[pltpu.VMEM((hw, 2 * tq), F32)],
        compiler_params=_params(("parallel", "parallel", "arbitrary"), 32),
        name="attn",
    )(shp(q), shp(k), vt, lq1.reshape(1, DA_DH), lk1.reshape(1, DA_DH), lq2.reshape(1, DA_DH),
      lk2.reshape(1, DA_DH), subln.reshape(1, hw))
    return o.reshape(bsz * s_len, DA_C)


def _mixout_kernel(x_ref, yf_ref, yr_ref, g_ref, bonus_ref, da_ref, lng_ref, lnb_ref, bd_ref, wrw_ref, wda_ref, o_ref):
    bd = bd_ref[...]
    y = yf_ref[...] + yr_ref[...]
    mu = _group_sum(y, bd) * (1.0 / RW_N)
    yc = y - mu
    var = _group_sum(yc * yc, bd) * (1.0 / RW_N)
    yn = yc * lax.rsqrt(var + GN_EPS) * lng_ref[...] + lnb_ref[...]
    y_rw = ((yn + bonus_ref[...]) * g_ref[...]).astype(BF16)
    mixed = _dot(y_rw, wrw_ref[...]) + _dot(da_ref[...].astype(BF16), wda_ref[...])
    o_ref[...] = x_ref[...] + mixed


def _mixout(x1, yf, yr, g, bonus, o_da, ln_g, ln_b, bd, w_out):
    t = x1.shape[0]
    tm = min(TOKEN_TILE, t)
    const2 = lambda i: (0, 0)
    tok = pl.BlockSpec((tm, RW_C), lambda i: (i, 0))
    vec = pl.BlockSpec((1, RW_C), const2)
    return pl.pallas_call(
        _mixout_kernel,
        grid=(t // tm,),
        in_specs=[
            pl.BlockSpec((tm, D_MODEL), lambda i: (i, 0)),
            tok, tok, tok, tok, tok, vec, vec,
            pl.BlockSpec((RW_C, RW_C), const2),
            pl.BlockSpec((RW_C, D_MODEL), const2),
            pl.BlockSpec((DA_C, D_MODEL), const2),
        ],
        out_specs=pl.BlockSpec((tm, D_MODEL), lambda i: (i, 0)),
        out_shape=jax.ShapeDtypeStruct((t, D_MODEL), F32),
        compiler_params=_params(("parallel",), 48),
        name="mixout",
    )(x1, yf, yr, g, bonus, o_da, ln_g.reshape(1, RW_C), ln_b.reshape(1, RW_C), bd,
      w_out[:RW_C].astype(BF16), w_out[RW_C:].astype(BF16))


def _encoder_layer(x, p):
    bsz, s_len, _ = x.shape
    t = bsz * s_len
    xf = x.reshape(t, D_MODEL)
    lane = jnp.arange(RW_C) // RW_N
    bd = (lane[:, None] == lane[None, :]).astype(BF16)

    x1 = _ffn(xf, p["ffn1_norm"], p["ffn1_w_gu"], p["ffn1_w_down"])
    proj_rw, proj_da = _proj(x1, p["mix_norm"], p["w_in"])
    r, v, kk, g, bonus, lw, kd, b = _rwprep(
        proj_rw, s_len, p["conv_w"], p["rw_w0"], p["rw_w_up"], p["rw_a0"], p["rw_a_up"], p["rw_g_up"],
        p["rw_k_k"], p["rw_k_a"], p["rw_r_k"], bd)
    yf, yr = _scan(r, v, kk, lw, kd, b, bsz, s_len)
    q, k, va = _daprep(proj_da, bsz, s_len, p["da_q_norm"], p["da_k_norm"], bd)
    o_da = _attn(q, k, va, bsz, s_len, p["da_lq1"], p["da_lk1"], p["da_lq2"], p["da_lk2"], p["da_subln"])
    x2 = _mixout(x1, yf, yr, g, bonus, o_da, p["rw_ln_g"], p["rw_ln_b"], bd, p["w_out"])
    out = _ffn(x2, p["ffn2_norm"], p["ffn2_w_gu"], p["ffn2_w_down"], p["final_norm"])
    return out.reshape(bsz, s_len, D_MODEL)


def kernel(x_prompt, x_sample, ffn1_norm, ffn1_w_gu, ffn1_w_down, mix_norm, w_in, conv_w, rw_w0, rw_w_up, rw_a0, rw_a_up, rw_g_up, rw_k_k, rw_k_a, rw_r_k, rw_ln_g, rw_ln_b, da_q_norm, da_k_norm, da_lq1, da_lk1, da_lq2, da_lk2, da_subln, w_out, ffn2_norm, ffn2_w_gu, ffn2_w_down, final_norm):
    names = ("ffn1_norm", "ffn1_w_gu", "ffn1_w_down", "mix_norm", "w_in", "conv_w", "rw_w0", "rw_w_up", "rw_a0",
             "rw_a_up", "rw_g_up", "rw_k_k", "rw_k_a", "rw_r_k", "rw_ln_g", "rw_ln_b", "da_q_norm", "da_k_norm",
             "da_lq1", "da_lk1", "da_lq2", "da_lk2", "da_subln", "w_out", "ffn2_norm", "ffn2_w_gu", "ffn2_w_down",
             "final_norm")
    vals = (ffn1_norm, ffn1_w_gu, ffn1_w_down, mix_norm, w_in, conv_w, rw_w0, rw_w_up, rw_a0, rw_a_up, rw_g_up,
            rw_k_k, rw_k_a, rw_r_k, rw_ln_g, rw_ln_b, da_q_norm, da_k_norm, da_lq1, da_lk1, da_lq2, da_lk2,
            da_subln, w_out, ffn2_norm, ffn2_w_gu, ffn2_w_down, final_norm)
    assert all(a.shape[0] == 1 for a in vals), "single-layer (depth 1) parameter stacks expected"
    p = {n: a[0] for n, a in zip(names, vals)}
    return (_encoder_layer(x_prompt, p), _encoder_layer(x_sample, p))
```

```python
import functools
import math

import jax
import jax.numpy as jnp
from jax import lax
from jax.experimental import pallas as pl
from jax.experimental.pallas import tpu as pltpu

F32 = jnp.float32
BF16 = jnp.bfloat16

D_MODEL = 1024
D_FF = 2816
RW_HEADS = 8
RW_N = 64
RW_C = RW_HEADS * RW_N
LORA_W = 64
LORA_A = 64
LORA_G = 128
DECAY_SCALE = 0.606531
GN_EPS = 64e-5
DA_HEADS = 4
DA_DH = 64
DA_C = DA_HEADS * 2 * DA_DH
ROT_DIM = DA_DH // 4
ROPE_THETA = 500000.0
NORM_EPS = 1e-6
RW_COLS = 3 * RW_C + 2 * LORA_W + 2 * LORA_A + LORA_G
LAM_INIT = 0.8 - 0.6 * math.exp(-0.3 * 0)
LOG2E = math.log2(math.e)

V7X_VMEM_BYTES = 64 * 1024 * 1024
SUBLANES = 8

FF_CHUNK = 256
N_FF_CHUNKS = D_FF // FF_CHUNK
TOKEN_TILE = 512
PREP_TILE = 256
CHUNK = 64
ATT_TQ = 512
ATT_TK = 512


def _params(sem, vmem_mb):
    return pltpu.CompilerParams(dimension_semantics=sem, vmem_limit_bytes=vmem_mb * 1024 * 1024)


def _dot(a, b):
    return jnp.dot(a, b, preferred_element_type=F32)


def _dot_nt(a, b):
    return lax.dot_general(a, b, (((1,), (1,)), ((), ())), preferred_element_type=F32)


def _dot_tn(a, b):
    return lax.dot_general(a, b, (((0,), (0,)), ((), ())), preferred_element_type=F32)


def _split2(x):
    hi = x.astype(BF16)
    lo = (x - hi.astype(F32)).astype(BF16)
    return hi, lo


def _split3(x):
    hi = x.astype(BF16)
    r1 = x - hi.astype(F32)
    mid = r1.astype(BF16)
    lo = (r1 - mid.astype(F32)).astype(BF16)
    return hi, mid, lo


def _group_sum(x, bd):
    hi, lo = _split2(x)
    return _dot(hi, bd) + _dot(lo, bd)


def _rms(x, g, eps=NORM_EPS):
    return x * lax.rsqrt(jnp.mean(x * x, axis=-1, keepdims=True) + eps) * g


def _ffn_kernel(x_ref, g_ref, wg_ref, wu_ref, wd_ref, *rest, final):
    if final:
        fg_ref, o_ref = rest
    else:
        (o_ref,) = rest
    x = x_ref[...]
    h = _rms(x, g_ref[...]).astype(BF16)
    acc = jnp.zeros_like(x)
    for j in range(N_FF_CHUNKS):
        gate = _dot(h, wg_ref[j])
        up = _dot(h, wu_ref[j])
        act = (gate * jax.nn.sigmoid(gate) * up).astype(BF16)
        acc = acc + _dot(act, wd_ref[j])
    y = x + 0.5 * acc
    if final:
        y = _rms(y, fg_ref[...])
    o_ref[...] = y


def _ffn(x, norm_g, w_gu, w_down, final_g=None):
    t = x.shape[0]
    tm = min(TOKEN_TILE, t)
    wg = w_gu[:, :D_FF].reshape(D_MODEL, N_FF_CHUNKS, FF_CHUNK).transpose(1, 0, 2).astype(BF16)
    wu = w_gu[:, D_FF:].reshape(D_MODEL, N_FF_CHUNKS, FF_CHUNK).transpose(1, 0, 2).astype(BF16)
    wd = w_down.reshape(N_FF_CHUNKS, FF_CHUNK, D_MODEL).astype(BF16)
    const3 = lambda i: (0, 0, 0)
    const2 = lambda i: (0, 0)
    in_specs = [
        pl.BlockSpec((tm, D_MODEL), lambda i: (i, 0)),
        pl.BlockSpec((1, D_MODEL), const2),
        pl.BlockSpec((N_FF_CHUNKS, D_MODEL, FF_CHUNK), const3),
        pl.BlockSpec((N_FF_CHUNKS, D_MODEL, FF_CHUNK), const3),
        pl.BlockSpec((N_FF_CHUNKS, FF_CHUNK, D_MODEL), const3),
    ]
    args = [x, norm_g.reshape(1, D_MODEL), wg, wu, wd]
    if final_g is not None:
        in_specs.append(pl.BlockSpec((1, D_MODEL), const2))
        args.append(final_g.reshape(1, D_MODEL))
    return pl.pallas_call(
        functools.partial(_ffn_kernel, final=final_g is not None),
        grid=(t // tm,),
        in_specs=in_specs,
        out_specs=pl.BlockSpec((tm, D_MODEL), lambda i: (i, 0)),
        out_shape=jax.ShapeDtypeStruct((t, D_MODEL), F32),
        compiler_params=_params(("parallel",), 56),
        name="ffn_final" if final_g is not None else "ffn",
    )(*args)


def _proj_kernel(x_ref, g_ref, wrw_ref, wda_ref, rw_ref, da_ref):
    h = _rms(x_ref[...], g_ref[...]).astype(BF16)
    rw_ref[...] = _dot(h, wrw_ref[...])
    da_ref[...] = _dot(h, wda_ref[...])


def _proj(x, norm_g, w_in):
    t = x.shape[0]
    tm = min(TOKEN_TILE, t)
    w_rw = w_in[:, :RW_COLS].astype(BF16)
    w_da = w_in[:, RW_COLS:].astype(BF16)
    const2 = lambda i: (0, 0)
    return pl.pallas_call(
        _proj_kernel,
        grid=(t // tm,),
        in_specs=[
            pl.BlockSpec((tm, D_MODEL), lambda i: (i, 0)),
            pl.BlockSpec((1, D_MODEL), const2),
            pl.BlockSpec((D_MODEL, RW_COLS), const2),
            pl.BlockSpec((D_MODEL, 3 * DA_C), const2),
        ],
        out_specs=[
            pl.BlockSpec((tm, RW_COLS), lambda i: (i, 0)),
            pl.BlockSpec((tm, 3 * DA_C), lambda i: (i, 0)),
        ],
        out_shape=[
            jax.ShapeDtypeStruct((t, RW_COLS), F32),
            jax.ShapeDtypeStruct((t, 3 * DA_C), F32),
        ],
        compiler_params=_params(("parallel",), 48),
        name="proj",
    )(x, norm_g.reshape(1, D_MODEL), w_rw, w_da)


def _rwprep_kernel(u_ref, up_ref, un_ref, conv_ref, w0_ref, wup_ref, a0_ref, aup_ref, gup_ref,
                   kk_ref, ka_ref, rk_ref, bd_ref,
                   r_o, v_o, kk_o, g_o, bonus_o, lw_o, kd_o, b_o, *, tm, s_len):
    i = pl.program_id(0)
    u = u_ref[...]
    start = i * tm
    first = (start % s_len) == 0
    last = ((start + tm) % s_len) == 0
    prev_row = jnp.where(first, 0.0, up_ref[SUBLANES - 1:SUBLANES, :])
    next_row = jnp.where(last, 0.0, un_ref[0:1, :])
    row = lax.broadcasted_iota(jnp.int32, (tm, 1), 0)
    prev = jnp.where(row == 0, prev_row, pltpu.roll(u, 1, 0))
    nxt = jnp.where(row == tm - 1, next_row, pltpu.roll(u, tm - 1, 0))
    c = prev * conv_ref[0:1, :] + u * conv_ref[1:2, :] + nxt * conv_ref[2:3, :]

    o_w = 3 * RW_C
    o_a = o_w + 2 * LORA_W
    o_g = o_a + 2 * LORA_A
    r = c[:, 0:RW_C]
    k = c[:, RW_C:2 * RW_C]
    v = c[:, 2 * RW_C:3 * RW_C]
    tw = jnp.tanh(c[:, o_w:o_a]).astype(BF16)
    ad = c[:, o_a:o_g].astype(BF16)
    gd = c[:, o_g:RW_COLS]
    bd = bd_ref[...]

    kkv = k * kk_ref[...]
    norm = jnp.sqrt(_group_sum(kkv * kkv, bd))
    kk = kkv / jnp.maximum(norm, 1e-12)
    g = _dot(jax.nn.sigmoid(gd).astype(BF16), gup_ref[...])

    ksum = jnp.zeros_like(k)
    for d in range(2):
        lw = -DECAY_SCALE * jax.nn.sigmoid(w0_ref[d:d + 1, :] + _dot(tw, wup_ref[d]))
        a = jax.nn.sigmoid(a0_ref[d:d + 1, :] + _dot(ad, aup_ref[d]))
        kd = k * (1.0 + (a - 1.0) * ka_ref[...])
        lw_o[d] = lw
        kd_o[d] = kd
        b_o[d] = kk * a
        ksum = ksum + kd
    coef = _group_sum(r * ksum * rk_ref[...], bd)
    r_o[...] = r
    v_o[...] = v
    kk_o[...] = kk
    g_o[...] = g
    bonus_o[...] = coef * v


def _pad_dir_rows(w_up):
    r = w_up.shape[1]
    z = jnp.zeros_like(w_up[0])
    return jnp.stack([jnp.concatenate([w_up[0], z], 0), jnp.concatenate([z, w_up[1]], 0)]).astype(BF16)


def _rwprep(proj_rw, s_len, conv_w, w0, w_up, a0, a_up, g_up, k_k, k_a, r_k, bd):
    t = proj_rw.shape[0]
    tm = min(PREP_TILE, s_len)
    nb8 = t // SUBLANES
    per = tm // SUBLANES
    const2 = lambda i: (0, 0)
    const3 = lambda i: (0, 0, 0)
    tok = pl.BlockSpec((tm, RW_C), lambda i: (i, 0))
    tok2 = pl.BlockSpec((2, tm, RW_C), lambda i: (0, i, 0))
    vec = pl.BlockSpec((1, RW_C), const2)
    one = jax.ShapeDtypeStruct((t, RW_C), F32)
    two = jax.ShapeDtypeStruct((2, t, RW_C), F32)
    return pl.pallas_call(
        functools.partial(_rwprep_kernel, tm=tm, s_len=s_len),
        grid=(t // tm,),
        in_specs=[
            pl.BlockSpec((tm, RW_COLS), lambda i: (i, 0)),
            pl.BlockSpec((SUBLANES, RW_COLS), lambda i: (jnp.maximum(i * per - 1, 0), 0)),
            pl.BlockSpec((SUBLANES, RW_COLS), lambda i: (jnp.minimum((i + 1) * per, nb8 - 1), 0)),
            pl.BlockSpec((3, RW_COLS), const2),
            pl.BlockSpec((2, RW_C), const2),
            pl.BlockSpec((2, 2 * LORA_W, RW_C), const3),
            pl.BlockSpec((2, RW_C), const2),
            pl.BlockSpec((2, 2 * LORA_A, RW_C), const3),
            pl.BlockSpec((LORA_G, RW_C), const2),
            vec, vec, vec,
            pl.BlockSpec((RW_C, RW_C), const2),
        ],
        out_specs=[tok, tok, tok, tok, tok, tok2, tok2, tok2],
        out_shape=[one, one, one, one, one, two, two, two],
        compiler_params=_params(("parallel",), 48),
        name="rwprep",
    )(proj_rw, proj_rw, proj_rw, conv_w, w0, _pad_dir_rows(w_up), a0, _pad_dir_rows(a_up),
      g_up.astype(BF16), k_k.reshape(1, RW_C), k_a.reshape(1, RW_C), r_k.reshape(1, RW_C), bd)


def _scan_kernel(rf_ref, vf_ref, kkf_ref, lwf_ref, kdf_ref, bf_ref, rr_ref, vr_ref, kkr_ref, lwr_ref, kdr_ref, br_ref,
                 yf_ref, yr_ref, h_ref, *, chunk):
    L = chunk
    L2 = 2 * L
    PW = 2 * RW_N
    n_pairs = RW_C // PW

    @pl.when(pl.program_id(1) == 0)
    def _():
        h_ref[...] = jnp.zeros_like(h_ref)

    row = lax.broadcasted_iota(jnp.int32, (L, L), 0)
    col = lax.broadcasted_iota(jnp.int32, (L, L), 1)
    row2 = lax.broadcasted_iota(jnp.int32, (L2, L2), 0)
    col2 = lax.broadcasted_iota(jnp.int32, (L2, L2), 1)
    same = (row2 < L) == (col2 < L)
    eye2 = (row2 == col2).astype(F32)
    first_head = lax.broadcasted_iota(jnp.int32, (L, PW), 1) < RW_N

    def stack(z):
        return jnp.concatenate([jnp.where(first_head, z, 0.0), jnp.where(first_head, 0.0, z)], axis=0)

    chains = []
    dirs = ((rf_ref, vf_ref, kkf_ref, lwf_ref, kdf_ref, bf_ref, yf_ref, 1),
            (rr_ref, vr_ref, kkr_ref, lwr_ref, kdr_ref, br_ref, yr_ref, -1))
    for d, (r_ref, v_ref, kk_ref, lw_ref, kd_ref, b_ref, y_ref, sgn) in enumerate(dirs):
        incl2 = same & ((row2 - col2) * sgn >= 0)
        strict2 = same & ((row2 - col2) * sgn > 0)
        tri = ((row - col) * sgn >= 0).astype(BF16)
        lw = lw_ref[...]
        hi, mid, lo = _split3(lw)
        c_in = _dot(tri, hi) + _dot(tri, mid) + _dot(tri, lo)
        c_tot = jnp.sum(lw, axis=0, keepdims=True)
        e_neg = jnp.exp(-c_in)
        e_end = jnp.exp(c_tot - c_in)
        kk = kk_ref[...]
        b = b_ref[...]
        kd = kd_ref[...]
        a_t = -kk * jnp.exp(c_in - lw)
        r_t = r_ref[...] * jnp.exp(c_in)
        b_t = b * e_neg
        k_t = kd * e_neg
        b_p = b * e_end
        k_p = kd * e_end
        p_end = jnp.exp(c_tot)
        v = v_ref[...]
        for p in range(n_pairs):
            ls = slice(p * PW, (p + 1) * PW)
            chains.append(dict(
                d=d, p=p, ls=ls, y_ref=y_ref, incl=incl2, strict=strict2,
                aa=stack(a_t[:, ls]).astype(BF16), rr=stack(r_t[:, ls]).astype(BF16),
                bb=stack(b_t[:, ls]).astype(BF16), kk=stack(k_t[:, ls]).astype(BF16),
                vv=stack(v[:, ls]).astype(BF16),
                bpT=jnp.transpose(stack(b_p[:, ls])).astype(BF16),
                kpT=jnp.transpose(stack(k_p[:, ls])).astype(BF16),
                pe_col=jnp.transpose(jnp.broadcast_to(p_end[:, ls], (PW, PW)))))

    for c in chains:
        x = _dot_nt(jnp.concatenate([c["aa"], c["rr"]], axis=0), jnp.concatenate([c["bb"], c["kk"]], axis=0))
        c["a_ab"] = jnp.where(c["strict"], x[:L2, :L2], 0.0)
        c["a_ak"] = jnp.where(c["strict"], x[:L2, L2:], 0.0).astype(BF16)
        c["a_rb"] = jnp.where(c["incl"], x[L2:, :L2], 0.0).astype(BF16)
        c["a_rk"] = jnp.where(c["incl"], x[L2:, L2:], 0.0).astype(BF16)
        c["inv"] = eye2 + c["a_ab"]
        c["pw"] = c["a_ab"]
    for _ in range(int(math.log2(L)) - 1):
        for c in chains:
            pwb = c["pw"].astype(BF16)
            c["pw"] = _dot(pwb, pwb)
        for c in chains:
            c["inv"] = c["inv"] + _dot(c["inv"].astype(BF16), c["pw"].astype(BF16))
    for c in chains:
        c["akv"] = _dot(c["a_ak"], c["vv"]).astype(BF16)
    for c in chains:
        inv = c["inv"].astype(BF16)
        c["w_t"] = _dot(inv, c["aa"]).astype(BF16)
        c["u_t"] = _dot(inv, c["akv"])
    for c in chains:
        c["h"] = h_ref[c["d"], c["p"]]
        c["hb"] = c["h"].astype(BF16)
        c["uu"] = (_dot(c["w_t"], c["hb"]) + c["u_t"]).astype(BF16)
    for c in chains:
        h_ref[c["d"], c["p"]] = c["pe_col"] * c["h"] + _dot(c["bpT"], c["uu"]) + _dot(c["kpT"], c["vv"])
    for c in chains:
        y2 = _dot(c["rr"], c["hb"]) + _dot(c["a_rb"], c["uu"]) + _dot(c["a_rk"], c["vv"])
        c["y_ref"][:, c["ls"]] = y2[:L, :] + y2[L:, :]


def _scan(r, v, kk, lw, kd, b, bsz, s_len):
    L = CHUNK
    nc = s_len // L
    shp = lambda a: a.reshape(bsz, s_len, RW_C)
    shp2 = lambda a: a.reshape(2, bsz, s_len, RW_C)
    f_sh = pl.BlockSpec((None, L, RW_C), lambda bi, c: (bi, c, 0))
    r_sh = pl.BlockSpec((None, L, RW_C), lambda bi, c: (bi, nc - 1 - c, 0))
    f_pd = pl.BlockSpec((None, None, L, RW_C), lambda bi, c: (0, bi, c, 0))
    r_pd = pl.BlockSpec((None, None, L, RW_C), lambda bi, c: (1, bi, nc - 1 - c, 0))
    out = jax.ShapeDtypeStruct((bsz, s_len, RW_C), F32)
    r3, v3, kk3, lw4, kd4, b4 = shp(r), shp(v), shp(kk), shp2(lw), shp2(kd), shp2(b)
    yf, yr = pl.pallas_call(
        functools.partial(_scan_kernel, chunk=L),
        grid=(bsz, nc),
        in_specs=[f_sh, f_sh, f_sh, f_pd, f_pd, f_pd, r_sh, r_sh, r_sh, r_pd, r_pd, r_pd],
        out_specs=[f_sh, r_sh],
        out_shape=[out, out],
        scratch_shapes=[pltpu.VMEM((2, RW_C // (2 * RW_N), 2 * RW_N, 2 * RW_N), F32)],
        compiler_params=_params(("parallel", "arbitrary"), 32),
        name="scan",
    )(r3, v3, kk3, lw4, kd4, b4, r3, v3, kk3, lw4, kd4, b4)
    return yf.reshape(bsz * s_len, RW_C), yr.reshape(bsz * s_len, RW_C)


def _daprep_kernel(p_ref, cf_ref, sn_ref, sp_ref, qn_ref, kn_ref, bd_ref, q_o, k_o, v_o):
    bd = bd_ref[...]
    cf, sn, sp = cf_ref[...], sn_ref[...], sp_ref[...]

    def norm_rope(x, g):
        ms = _group_sum(x * x, bd) * (1.0 / DA_DH)
        xn = x * lax.rsqrt(ms + NORM_EPS) * g
        half = ROT_DIM // 2
        return xn * cf + pltpu.roll(xn, DA_C - half, 1) * sn + pltpu.roll(xn, half, 1) * sp

    q = norm_rope(p_ref[:, 0:DA_C], qn_ref[...]) * (DA_DH ** -0.5 * LOG2E)
    k = norm_rope(p_ref[:, DA_C:2 * DA_C], kn_ref[...])
    q_o[...] = q.astype(BF16)
    k_o[...] = k.astype(BF16)
    v_o[...] = jnp.transpose(p_ref[:, 2 * DA_C:3 * DA_C]).astype(BF16)


def _rope_lane_tables(s_len):
    half = ROT_DIM // 2
    inv_freq = ROPE_THETA ** (-jnp.arange(0, ROT_DIM, 2, dtype=F32) / ROT_DIM)
    ang = jnp.arange(s_len, dtype=F32)[:, None] * inv_freq[None, :]
    cos, sin = jnp.cos(ang), jnp.sin(ang)
    j = jnp.arange(DA_C) % DA_DH
    f = j % half
    cf = jnp.where(j < ROT_DIM, cos[:, f], 1.0)
    sn = jnp.where(j < half, -sin[:, f], 0.0)
    sp = jnp.where((j >= half) & (j < ROT_DIM), sin[:, f], 0.0)
    return cf, sn, sp


def _daprep(proj_da, bsz, s_len, q_norm, k_norm, bd):
    t = proj_da.shape[0]
    tm = min(ATT_TK, s_len)
    per_seq = s_len // tm
    cf, sn, sp = _rope_lane_tables(s_len)
    const2 = lambda i: (0, 0)
    tok = pl.BlockSpec((tm, DA_C), lambda i: (i, 0))
    tab = pl.BlockSpec((tm, DA_C), lambda i: (i % per_seq, 0))
    vec = pl.BlockSpec((1, DA_C), const2)
    out = jax.ShapeDtypeStruct((t, DA_C), BF16)
    reps = DA_C // DA_DH
    return pl.pallas_call(
        _daprep_kernel,
        grid=(t // tm,),
        in_specs=[pl.BlockSpec((tm, 3 * DA_C), lambda i: (i, 0)), tab, tab, tab, vec, vec,
                  pl.BlockSpec((DA_C, DA_C), const2)],
        out_specs=[tok, tok, pl.BlockSpec((None, None, DA_C, tm), lambda i: (i // per_seq, i % per_seq, 0, 0))],
        out_shape=[out, out, jax.ShapeDtypeStruct((bsz, per_seq, DA_C, tm), BF16)],
        compiler_params=_params(("parallel",), 32),
        name="daprep",
    )(proj_da, cf, sn, sp, jnp.tile(q_norm, reps).reshape(1, DA_C), jnp.tile(k_norm, reps).reshape(1, DA_C), bd)


def _attn_kernel(q_ref, k_ref, vt_ref, lq1_ref, lk1_ref, lq2_ref, lk2_ref, sub_ref, o_ref,
                 acc_ref, sa_ref, sb_ref, pa_ref, pb_ref, *, tq, tk):
    q = q_ref[...]
    lane = lax.broadcasted_iota(jnp.int32, q.shape, 1)
    zero = jnp.zeros_like(q)
    qs = jnp.concatenate([jnp.where(lane < DA_DH, q, zero), jnp.where(lane >= DA_DH, q, zero)], axis=0)
    nk = k_ref.shape[0] // tk
    acc_ref[...] = jnp.zeros_like(acc_ref)
    pb_ref[...] = jnp.zeros_like(pb_ref)

    def scores(i):
        return _dot_nt(k_ref[pl.ds(pl.multiple_of(i * tk, tk), tk), :], qs)

    def softmax(s_ref, p_ref, m_prev, l_prev):
        s = s_ref[...]
        m_new = jnp.maximum(m_prev, jnp.max(s, axis=0, keepdims=True))
        p = jnp.exp2(s - m_new)
        p_ref[...] = p.astype(BF16)
        corr = jnp.exp2(m_prev - m_new)
        return m_new, corr * l_prev + jnp.sum(p, axis=0, keepdims=True), corr

    def accumulate(p_ref, i, corr):
        acc_ref[...] = corr * acc_ref[...] + _dot(vt_ref[i], p_ref[...])

    def pair(j, carry, more):
        m, l, corr_b = carry
        sb_ref[...] = scores(2 * j + 1)
        accumulate(pb_ref, jnp.maximum(2 * j - 1, 0), corr_b)
        m, l, corr_a = softmax(sa_ref, pa_ref, m, l)
        if more:
            sa_ref[...] = scores(2 * j + 2)
        accumulate(pa_ref, 2 * j, corr_a)
        m, l, corr_b = softmax(sb_ref, pb_ref, m, l)
        return m, l, corr_b

    sa_ref[...] = scores(0)
    ones = jnp.ones((1, 2 * tq), F32)
    carry = (jnp.full((1, 2 * tq), -jnp.inf, F32), jnp.zeros((1, 2 * tq), F32), ones)
    carry = lax.fori_loop(0, nk // 2 - 1, functools.partial(pair, more=True), carry)
    _, l, corr_b = pair(nk // 2 - 1, carry, more=False)
    accumulate(pb_ref, nk - 1, corr_b)

    o2 = acc_ref[...] / l
    lam = (jnp.exp(jnp.sum(lq1_ref[...] * lk1_ref[...], axis=1, keepdims=True))
           - jnp.exp(jnp.sum(lq2_ref[...] * lk2_ref[...], axis=1, keepdims=True))) + LAM_INIT
    o = jnp.transpose(o2[:, 0:tq] - lam * o2[:, tq:2 * tq])
    o_ref[...] = _rms(o, sub_ref[...]) * (1.0 - LAM_INIT)


def _attn(q, k, vt, bsz, s_len, lq1, lk1, lq2, lk2, subln):
    tq = min(ATT_TQ, s_len)
    tk = min(ATT_TK, s_len)
    nk = s_len // tk
    assert nk % 2 == 0, "the kv loop consumes blocks in pairs"
    hw = 2 * DA_DH
    shp = lambda a: a.reshape(bsz, s_len, DA_C)
    lvec = pl.BlockSpec((1, DA_DH), lambda b, h, qi: (0, 0))
    o = pl.pallas_call(
        functools.partial(_attn_kernel, tq=tq, tk=tk),
        grid=(bsz, DA_HEADS, s_len // tq),
        in_specs=[
            pl.BlockSpec((None, tq, hw), lambda b, h, qi: (b, qi, h)),
            pl.BlockSpec((None, s_len, hw), lambda b, h, qi: (b, 0, h)),
            pl.BlockSpec((None, nk, hw, tk), lambda b, h, qi: (b, 0, h, 0)),
            lvec, lvec, lvec, lvec,
            pl.BlockSpec((1, hw), lambda b, h, qi: (0, 0)),
        ],
        out_specs=pl.BlockSpec((None, tq, hw), lambda b, h, qi: (b, qi, h)),
        out_shape=jax.ShapeDtypeStruct((bsz, s_len, DA_C), F32),
        scratch_shapes=[pltpu.VMEM((hw, 2 * tq), F32),
                        pltpu.VMEM((tk, 2 * tq), F32), pltpu.VMEM((tk, 2 * tq), F32),
                        pltpu.VMEM((tk, 2 * tq), BF16), pltpu.VMEM((tk, 2 * tq), BF16)],
        compiler_params=_params(("parallel", "parallel", "arbitrary"), 32),
        name="attn",
    )(shp(q), shp(k), vt, lq1.reshape(1, DA_DH), lk1.reshape(1, DA_DH), lq2.reshape(1, DA_DH),
      lk2.reshape(1, DA_DH), subln.reshape(1, hw))
    return o.reshape(bsz * s_len, DA_C)


def _mixout_kernel(x_ref, yf_ref, yr_ref, g_ref, bonus_ref, da_ref, lng_ref, lnb_ref, bd_ref, wrw_ref, wda_ref, o_ref):
    bd = bd_ref[...]
    y = yf_ref[...] + yr_ref[...]
    mu = _group_sum(y, bd) * (1.0 / RW_N)
    yc = y - mu
    var = _group_sum(yc * yc, bd) * (1.0 / RW_N)
    yn = yc * lax.rsqrt(var + GN_EPS) * lng_ref[...] + lnb_ref[...]
    y_rw = ((yn + bonus_ref[...]) * g_ref[...]).astype(BF16)
    mixed = _dot(y_rw, wrw_ref[...]) + _dot(da_ref[...].astype(BF16), wda_ref[...])
    o_ref[...] = x_ref[...] + mixed


def _mixout(x1, yf, yr, g, bonus, o_da, ln_g, ln_b, bd, w_out):
    t = x1.shape[0]
    tm = min(TOKEN_TILE, t)
    const2 = lambda i: (0, 0)
    tok = pl.BlockSpec((tm, RW_C), lambda i: (i, 0))
    vec = pl.BlockSpec((1, RW_C), const2)
    return pl.pallas_call(
        _mixout_kernel,
        grid=(t // tm,),
        in_specs=[
            pl.BlockSpec((tm, D_MODEL), lambda i: (i, 0)),
            tok, tok, tok, tok, tok, vec, vec,
            pl.BlockSpec((RW_C, RW_C), const2),
            pl.BlockSpec((RW_C, D_MODEL), const2),
            pl.BlockSpec((DA_C, D_MODEL), const2),
        ],
        out_specs=pl.BlockSpec((tm, D_MODEL), lambda i: (i, 0)),
        out_shape=jax.ShapeDtypeStruct((t, D_MODEL), F32),
        compiler_params=_params(("parallel",), 48),
        name="mixout",
    )(x1, yf, yr, g, bonus, o_da, ln_g.reshape(1, RW_C), ln_b.reshape(1, RW_C), bd,
      w_out[:RW_C].astype(BF16), w_out[RW_C:].astype(BF16))


def _encoder_layer(x, p):
    bsz, s_len, _ = x.shape
    t = bsz * s_len
    xf = x.reshape(t, D_MODEL)
    lane = jnp.arange(RW_C) // RW_N
    bd = (lane[:, None] == lane[None, :]).astype(BF16)

    x1 = _ffn(xf, p["ffn1_norm"], p["ffn1_w_gu"], p["ffn1_w_down"])
    proj_rw, proj_da = _proj(x1, p["mix_norm"], p["w_in"])
    r, v, kk, g, bonus, lw, kd, b = _rwprep(
        proj_rw, s_len, p["conv_w"], p["rw_w0"], p["rw_w_up"], p["rw_a0"], p["rw_a_up"], p["rw_g_up"],
        p["rw_k_k"], p["rw_k_a"], p["rw_r_k"], bd)
    yf, yr = _scan(r, v, kk, lw, kd, b, bsz, s_len)
    q, k, va = _daprep(proj_da, bsz, s_len, p["da_q_norm"], p["da_k_norm"], bd)
    o_da = _attn(q, k, va, bsz, s_len, p["da_lq1"], p["da_lk1"], p["da_lq2"], p["da_lk2"], p["da_subln"])
    x2 = _mixout(x1, yf, yr, g, bonus, o_da, p["rw_ln_g"], p["rw_ln_b"], bd, p["w_out"])
    out = _ffn(x2, p["ffn2_norm"], p["ffn2_w_gu"], p["ffn2_w_down"], p["final_norm"])
    return out.reshape(bsz, s_len, D_MODEL)


def kernel(x_prompt, x_sample, ffn1_norm, ffn1_w_gu, ffn1_w_down, mix_norm, w_in, conv_w, rw_w0, rw_w_up, rw_a0, rw_a_up, rw_g_up, rw_k_k, rw_k_a, rw_r_k, rw_ln_g, rw_ln_b, da_q_norm, da_k_norm, da_lq1, da_lk1, da_lq2, da_lk2, da_subln, w_out, ffn2_norm, ffn2_w_gu, ffn2_w_down, final_norm):
    names = ("ffn1_norm", "ffn1_w_gu", "ffn1_w_down", "mix_norm", "w_in", "conv_w", "rw_w0", "rw_w_up", "rw_a0",
             "rw_a_up", "rw_g_up", "rw_k_k", "rw_k_a", "rw_r_k", "rw_ln_g", "rw_ln_b", "da_q_norm", "da_k_norm",
             "da_lq1", "da_lk1", "da_lq2", "da_lk2", "da_subln", "w_out", "ffn2_norm", "ffn2_w_gu", "ffn2_w_down",
             "final_norm")
    vals = (ffn1_norm, ffn1_w_gu, ffn1_w_down, mix_norm, w_in, conv_w, rw_w0, rw_w_up, rw_a0, rw_a_up, rw_g_up,
            rw_k_k, rw_k_a, rw_r_k, rw_ln_g, rw_ln_b, da_q_norm, da_k_norm, da_lq1, da_lk1, da_lq2, da_lk2,
            da_subln, w_out, ffn2_norm, ffn2_w_gu, ffn2_w_down, final_norm)
    assert all(a.shape[0] == 1 for a in vals), "single-layer (depth 1) parameter stacks expected"
    p = {n: a[0] for n, a in zip(names, vals)}
    return (_encoder_layer(x_prompt, p), _encoder_layer(x_sample, p))
```

```python
import functools
import math

import jax
import jax.numpy as jnp
from jax import lax
from jax.experimental import pallas as pl
from jax.experimental.pallas import tpu as pltpu

F32 = jnp.float32
BF16 = jnp.bfloat16

D_MODEL = 1024
D_FF = 2816
RW_HEADS = 8
RW_N = 64
RW_C = RW_HEADS * RW_N
LORA_W = 64
LORA_A = 64
LORA_G = 128
DECAY_SCALE = 0.606531
GN_EPS = 64e-5
DA_HEADS = 4
DA_DH = 64
DA_C = DA_HEADS * 2 * DA_DH
ROT_DIM = DA_DH // 4
ROPE_THETA = 500000.0
NORM_EPS = 1e-6
RW_COLS = 3 * RW_C + 2 * LORA_W + 2 * LORA_A + LORA_G
LAM_INIT = 0.8 - 0.6 * math.exp(-0.3 * 0)
LOG2E = math.log2(math.e)

V7X_VMEM_BYTES = 64 * 1024 * 1024
SUBLANES = 8

FF_CHUNK = 256
N_FF_CHUNKS = D_FF // FF_CHUNK
TOKEN_TILE = 512
PREP_TILE = 256
CHUNK = 64
SCAN_CHUNKS_PER_STEP = 4
ATT_TQ = 512
ONES_ROWS = 16
VT_ROWS = 2 * DA_DH + ONES_ROWS
ATT_TK = 512


def _params(sem, vmem_mb):
    return pltpu.CompilerParams(dimension_semantics=sem, vmem_limit_bytes=vmem_mb * 1024 * 1024)


def _dot(a, b):
    return jnp.dot(a, b, preferred_element_type=F32)


def _dot_nt(a, b):
    return lax.dot_general(a, b, (((1,), (1,)), ((), ())), preferred_element_type=F32)


def _dot_tn(a, b):
    return lax.dot_general(a, b, (((0,), (0,)), ((), ())), preferred_element_type=F32)


def _split2(x):
    hi = x.astype(BF16)
    lo = (x - hi.astype(F32)).astype(BF16)
    return hi, lo


def _split3(x):
    hi = x.astype(BF16)
    r1 = x - hi.astype(F32)
    mid = r1.astype(BF16)
    lo = (r1 - mid.astype(F32)).astype(BF16)
    return hi, mid, lo


def _group_sum(x, bd):
    hi, lo = _split2(x)
    return _dot(hi, bd) + _dot(lo, bd)


def _rms(x, g, eps=NORM_EPS):
    return x * lax.rsqrt(jnp.mean(x * x, axis=-1, keepdims=True) + eps) * g


def _ffn_kernel(x_ref, g_ref, wg_ref, wu_ref, wd_ref, *rest, final):
    if final:
        fg_ref, o_ref = rest
    else:
        (o_ref,) = rest
    x = x_ref[...]
    h = _rms(x, g_ref[...]).astype(BF16)
    acc = jnp.zeros_like(x)
    for j in range(N_FF_CHUNKS):
        gate = _dot(h, wg_ref[j])
        up = _dot(h, wu_ref[j])
        act = (gate * jax.nn.sigmoid(gate) * up).astype(BF16)
        acc = acc + _dot(act, wd_ref[j])
    y = x + 0.5 * acc
    if final:
        y = _rms(y, fg_ref[...])
    o_ref[...] = y


def _ffn(x, norm_g, w_gu, w_down, final_g=None):
    t = x.shape[0]
    tm = min(TOKEN_TILE, t)
    wg = w_gu[:, :D_FF].reshape(D_MODEL, N_FF_CHUNKS, FF_CHUNK).transpose(1, 0, 2).astype(BF16)
    wu = w_gu[:, D_FF:].reshape(D_MODEL, N_FF_CHUNKS, FF_CHUNK).transpose(1, 0, 2).astype(BF16)
    wd = w_down.reshape(N_FF_CHUNKS, FF_CHUNK, D_MODEL).astype(BF16)
    const3 = lambda i: (0, 0, 0)
    const2 = lambda i: (0, 0)
    in_specs = [
        pl.BlockSpec((tm, D_MODEL), lambda i: (i, 0)),
        pl.BlockSpec((1, D_MODEL), const2),
        pl.BlockSpec((N_FF_CHUNKS, D_MODEL, FF_CHUNK), const3),
        pl.BlockSpec((N_FF_CHUNKS, D_MODEL, FF_CHUNK), const3),
        pl.BlockSpec((N_FF_CHUNKS, FF_CHUNK, D_MODEL), const3),
    ]
    args = [x, norm_g.reshape(1, D_MODEL), wg, wu, wd]
    if final_g is not None:
        in_specs.append(pl.BlockSpec((1, D_MODEL), const2))
        args.append(final_g.reshape(1, D_MODEL))
    return pl.pallas_call(
        functools.partial(_ffn_kernel, final=final_g is not None),
        grid=(t // tm,),
        in_specs=in_specs,
        out_specs=pl.BlockSpec((tm, D_MODEL), lambda i: (i, 0)),
        out_shape=jax.ShapeDtypeStruct((t, D_MODEL), F32),
        compiler_params=_params(("parallel",), 56),
        name="ffn_final" if final_g is not None else "ffn",
    )(*args)


def _proj_kernel(x_ref, g_ref, wrw_ref, wda_ref, rw_ref, da_ref):
    h = _rms(x_ref[...], g_ref[...]).astype(BF16)
    rw_ref[...] = _dot(h, wrw_ref[...])
    da_ref[...] = _dot(h, wda_ref[...])


def _proj(x, norm_g, w_in):
    t = x.shape[0]
    tm = min(TOKEN_TILE, t)
    w_rw = w_in[:, :RW_COLS].astype(BF16)
    w_da = w_in[:, RW_COLS:].astype(BF16)
    const2 = lambda i: (0, 0)
    return pl.pallas_call(
        _proj_kernel,
        grid=(t // tm,),
        in_specs=[
            pl.BlockSpec((tm, D_MODEL), lambda i: (i, 0)),
            pl.BlockSpec((1, D_MODEL), const2),
            pl.BlockSpec((D_MODEL, RW_COLS), const2),
            pl.BlockSpec((D_MODEL, 3 * DA_C), const2),
        ],
        out_specs=[
            pl.BlockSpec((tm, RW_COLS), lambda i: (i, 0)),
            pl.BlockSpec((tm, 3 * DA_C), lambda i: (i, 0)),
        ],
        out_shape=[
            jax.ShapeDtypeStruct((t, RW_COLS), F32),
            jax.ShapeDtypeStruct((t, 3 * DA_C), F32),
        ],
        compiler_params=_params(("parallel",), 48),
        name="proj",
    )(x, norm_g.reshape(1, D_MODEL), w_rw, w_da)


def _rwprep_kernel(u_ref, up_ref, un_ref, conv_ref, w0_ref, wup_ref, a0_ref, aup_ref, gup_ref,
                   kk_ref, ka_ref, rk_ref, bd_ref,
                   r_o, v_o, kk_o, g_o, bonus_o, lw_o, kd_o, b_o, *, tm, s_len):
    i = pl.program_id(0)
    u = u_ref[...]
    start = i * tm
    first = (start % s_len) == 0
    last = ((start + tm) % s_len) == 0
    prev_row = jnp.where(first, 0.0, up_ref[SUBLANES - 1:SUBLANES, :])
    next_row = jnp.where(last, 0.0, un_ref[0:1, :])
    row = lax.broadcasted_iota(jnp.int32, (tm, 1), 0)
    prev = jnp.where(row == 0, prev_row, pltpu.roll(u, 1, 0))
    nxt = jnp.where(row == tm - 1, next_row, pltpu.roll(u, tm - 1, 0))
    c = prev * conv_ref[0:1, :] + u * conv_ref[1:2, :] + nxt * conv_ref[2:3, :]

    o_w = 3 * RW_C
    o_a = o_w + 2 * LORA_W
    o_g = o_a + 2 * LORA_A
    r = c[:, 0:RW_C]
    k = c[:, RW_C:2 * RW_C]
    v = c[:, 2 * RW_C:3 * RW_C]
    tw = jnp.tanh(c[:, o_w:o_a]).astype(BF16)
    ad = c[:, o_a:o_g].astype(BF16)
    gd = c[:, o_g:RW_COLS]
    bd = bd_ref[...]

    kkv = k * kk_ref[...]
    norm = jnp.sqrt(_group_sum(kkv * kkv, bd))
    kk = kkv / jnp.maximum(norm, 1e-12)
    g = _dot(jax.nn.sigmoid(gd).astype(BF16), gup_ref[...])

    ksum = jnp.zeros_like(k)
    for d in range(2):
        lw = -DECAY_SCALE * jax.nn.sigmoid(w0_ref[d:d + 1, :] + _dot(tw, wup_ref[d]))
        a = jax.nn.sigmoid(a0_ref[d:d + 1, :] + _dot(ad, aup_ref[d]))
        kd = k * (1.0 + (a - 1.0) * ka_ref[...])
        lw_o[d] = lw
        kd_o[d] = kd
        b_o[d] = kk * a
        ksum = ksum + kd
    coef = _group_sum(r * ksum * rk_ref[...], bd)
    r_o[...] = r
    v_o[...] = v
    kk_o[...] = kk
    g_o[...] = g
    bonus_o[...] = coef * v


def _pad_dir_rows(w_up):
    r = w_up.shape[1]
    z = jnp.zeros_like(w_up[0])
    return jnp.stack([jnp.concatenate([w_up[0], z], 0), jnp.concatenate([z, w_up[1]], 0)]).astype(BF16)


def _rwprep(proj_rw, s_len, conv_w, w0, w_up, a0, a_up, g_up, k_k, k_a, r_k, bd):
    t = proj_rw.shape[0]
    tm = min(PREP_TILE, s_len)
    nb8 = t // SUBLANES
    per = tm // SUBLANES
    const2 = lambda i: (0, 0)
    const3 = lambda i: (0, 0, 0)
    tok = pl.BlockSpec((tm, RW_C), lambda i: (i, 0))
    tok2 = pl.BlockSpec((2, tm, RW_C), lambda i: (0, i, 0))
    vec = pl.BlockSpec((1, RW_C), const2)
    one = jax.ShapeDtypeStruct((t, RW_C), F32)
    two = jax.ShapeDtypeStruct((2, t, RW_C), F32)
    return pl.pallas_call(
        functools.partial(_rwprep_kernel, tm=tm, s_len=s_len),
        grid=(t // tm,),
        in_specs=[
            pl.BlockSpec((tm, RW_COLS), lambda i: (i, 0)),
            pl.BlockSpec((SUBLANES, RW_COLS), lambda i: (jnp.maximum(i * per - 1, 0), 0)),
            pl.BlockSpec((SUBLANES, RW_COLS), lambda i: (jnp.minimum((i + 1) * per, nb8 - 1), 0)),
            pl.BlockSpec((3, RW_COLS), const2),
            pl.BlockSpec((2, RW_C), const2),
            pl.BlockSpec((2, 2 * LORA_W, RW_C), const3),
            pl.BlockSpec((2, RW_C), const2),
            pl.BlockSpec((2, 2 * LORA_A, RW_C), const3),
            pl.BlockSpec((LORA_G, RW_C), const2),
            vec, vec, vec,
            pl.BlockSpec((RW_C, RW_C), const2),
        ],
        out_specs=[tok, tok, tok, tok, tok, tok2, tok2, tok2],
        out_shape=[one, one, one, one, one, two, two, two],
        compiler_params=_params(("parallel",), 48),
        name="rwprep",
    )(proj_rw, proj_rw, proj_rw, conv_w, w0, _pad_dir_rows(w_up), a0, _pad_dir_rows(a_up),
      g_up.astype(BF16), k_k.reshape(1, RW_C), k_a.reshape(1, RW_C), r_k.reshape(1, RW_C), bd)


def _scan_kernel(rf_ref, vf_ref, kkf_ref, lwf_ref, kdf_ref, bf_ref, rr_ref, vr_ref, kkr_ref, lwr_ref, kdr_ref, br_ref,
                 yf_ref, yr_ref, h_ref, *, chunk):
    L = chunk
    L2 = 2 * L
    PW = 2 * RW_N
    n_pairs = RW_C // PW
    n_sub = lwf_ref.shape[0] // L
    assert L2 == PW, "interaction matrices and lane pairs share the 128-lane tile"

    @pl.when(pl.program_id(1) == 0)
    def _():
        h_ref[...] = jnp.zeros_like(h_ref)

    row = lax.broadcasted_iota(jnp.int32, (L, L), 0)
    col = lax.broadcasted_iota(jnp.int32, (L, L), 1)
    row2 = lax.broadcasted_iota(jnp.int32, (L, L2), 0)
    col2 = lax.broadcasted_iota(jnp.int32, (L, L2), 1) & (L - 1)
    eye2 = (row2 == col2).astype(F32)
    first_head = lax.broadcasted_iota(jnp.int32, (L, PW), 1) < RW_N

    def stack(z):
        zero = jnp.zeros_like(z)
        return jnp.concatenate([jnp.where(first_head, z, zero), jnp.where(first_head, zero, z)], axis=0)

    chains = []
    dirs = ((rf_ref, vf_ref, kkf_ref, lwf_ref, kdf_ref, bf_ref, yf_ref, 1),
            (rr_ref, vr_ref, kkr_ref, lwr_ref, kdr_ref, br_ref, yr_ref, -1))
    for d, (r_ref, v_ref, kk_ref, lw_ref, kd_ref, b_ref, y_ref, sgn) in enumerate(dirs):
        incl2 = (row2 - col2) * sgn >= 0
        strict2 = (row2 - col2) * sgn > 0
        tri = ((row - col) * sgn >= 0).astype(BF16)
        for step, j in enumerate(range(n_sub) if sgn > 0 else reversed(range(n_sub))):
            rows = slice(j * L, (j + 1) * L)
            lw = lw_ref[rows, :]
            hi, mid, lo = _split3(lw)
            c_in = _dot(tri, hi) + _dot(tri, mid) + _dot(tri, lo)
            c_tot = jnp.sum(lw, axis=0, keepdims=True)
            e_neg = jnp.exp(-c_in)
            e_end = jnp.exp(c_tot - c_in)
            kk = kk_ref[rows, :]
            b = b_ref[rows, :]
            kd = kd_ref[rows, :]
            a_t = -kk * jnp.exp(c_in - lw)
            r_t = r_ref[rows, :] * jnp.exp(c_in)
            b_t = b * e_neg
            k_t = kd * e_neg
            b_p = b * e_end
            k_p = kd * e_end
            p_end = jnp.exp(c_tot)
            v = v_ref[rows, :]
            for p in range(n_pairs):
                ls = slice(p * PW, (p + 1) * PW)
                a_n = a_t[:, ls].astype(BF16)
                chains.append(dict(
                    d=d, p=p, step=step, rows=rows, ls=ls, y_ref=y_ref, incl=incl2, strict=strict2,
                    ar=jnp.concatenate([a_n, r_t[:, ls].astype(BF16)], axis=0),
                    bk=jnp.concatenate([stack(b_t[:, ls].astype(BF16)), stack(k_t[:, ls].astype(BF16))], axis=0),
                    aa=stack(a_n), vv=stack(v[:, ls].astype(BF16)),
                    bpT=jnp.transpose(stack(b_p[:, ls])).astype(BF16),
                    kpT=jnp.transpose(stack(k_p[:, ls])).astype(BF16),
                    pe_col=jnp.transpose(jnp.broadcast_to(p_end[:, ls], (PW, PW)))))

    for c in chains:
        x = _dot_nt(c["ar"], c["bk"])
        c["a_ab"] = jnp.where(c["strict"], x[:L, :L2], 0.0)
        c["a_ak"] = jnp.where(c["strict"], x[:L, L2:], 0.0).astype(BF16)
        c["a_rb"] = jnp.where(c["incl"], x[L:, :L2], 0.0).astype(BF16)
        c["a_rk"] = jnp.where(c["incl"], x[L:, L2:], 0.0).astype(BF16)
        c["inv"] = eye2 + c["a_ab"]
        c["pw"] = c["a_ab"].astype(BF16)
    n_sq = int(math.log2(L)) - 1
    for k in range(n_sq + 1):
        for c in chains:
            pws = stack(c["pw"])
            if k == 0:
                c["pw"] = _dot(c["pw"], pws).astype(BF16)
            elif k < n_sq:
                both = _dot(jnp.concatenate([c["pw"], c["inv"].astype(BF16)], axis=0), pws)
                c["pw"] = both[:L, :].astype(BF16)
                c["inv"] = c["inv"] + both[L:, :]
            else:
                c["inv"] = c["inv"] + _dot(c["inv"].astype(BF16), pws)
    for c in chains:
        c["akv"] = stack(_dot(c["a_ak"], c["vv"]).astype(BF16))
    for c in chains:
        inv = c["inv"].astype(BF16)
        c["w_t"] = _dot(inv, c["aa"]).astype(BF16)
        c["u_t"] = _dot(inv, c["akv"])
    state = {(d, p): h_ref[d, p] for d in range(2) for p in range(n_pairs)}
    for step in range(n_sub):
        now = [c for c in chains if c["step"] == step]
        for c in now:
            c["h"] = state[c["d"], c["p"]]
            c["hb"] = c["h"].astype(BF16)
            c["uu"] = stack((_dot(c["w_t"], c["hb"]) + c["u_t"]).astype(BF16))
        for c in now:
            state[c["d"], c["p"]] = c["pe_col"] * c["h"] + _dot(c["bpT"], c["uu"]) + _dot(c["kpT"], c["vv"])
        for c in now:
            c["y_ref"][c["rows"], c["ls"]] = (_dot(c["ar"][L:, :], c["hb"]) + _dot(c["a_rb"], c["uu"])
                                              + _dot(c["a_rk"], c["vv"]))
    for (d, p), h in state.items():
        h_ref[d, p] = h


def _scan(r, v, kk, lw, kd, b, bsz, s_len):
    rows = CHUNK * SCAN_CHUNKS_PER_STEP
    nc = s_len // rows
    shp = lambda a: a.reshape(bsz, s_len, RW_C)
    shp2 = lambda a: a.reshape(2, bsz, s_len, RW_C)
    f_sh = pl.BlockSpec((None, rows, RW_C), lambda bi, c: (bi, c, 0))
    r_sh = pl.BlockSpec((None, rows, RW_C), lambda bi, c: (bi, nc - 1 - c, 0))
    f_pd = pl.BlockSpec((None, None, rows, RW_C), lambda bi, c: (0, bi, c, 0))
    r_pd = pl.BlockSpec((None, None, rows, RW_C), lambda bi, c: (1, bi, nc - 1 - c, 0))
    out = jax.ShapeDtypeStruct((bsz, s_len, RW_C), F32)
    r3, v3, kk3, lw4, kd4, b4 = shp(r), shp(v), shp(kk), shp2(lw), shp2(kd), shp2(b)
    yf, yr = pl.pallas_call(
        functools.partial(_scan_kernel, chunk=CHUNK),
        grid=(bsz, nc),
        in_specs=[f_sh, f_sh, f_sh, f_pd, f_pd, f_pd, r_sh, r_sh, r_sh, r_pd, r_pd, r_pd],
        out_specs=[f_sh, r_sh],
        out_shape=[out, out],
        scratch_shapes=[pltpu.VMEM((2, RW_C // (2 * RW_N), 2 * RW_N, 2 * RW_N), F32)],
        compiler_params=_params(("parallel", "arbitrary"), 32),
        name="scan",
    )(r3, v3, kk3, lw4, kd4, b4, r3, v3, kk3, lw4, kd4, b4)
    return yf.reshape(bsz * s_len, RW_C), yr.reshape(bsz * s_len, RW_C)


def _daprep_kernel(p_ref, cf_ref, sn_ref, sp_ref, qn_ref, kn_ref, bd_ref, q_o, k_o, v_o):
    bd = bd_ref[...]
    cf, sn, sp = cf_ref[...], sn_ref[...], sp_ref[...]

    def norm_rope(x, g):
        ms = _group_sum(x * x, bd) * (1.0 / DA_DH)
        xn = x * lax.rsqrt(ms + NORM_EPS) * g
        half = ROT_DIM // 2
        return xn * cf + pltpu.roll(xn, DA_C - half, 1) * sn + pltpu.roll(xn, half, 1) * sp

    q = norm_rope(p_ref[:, 0:DA_C], qn_ref[...]) * (DA_DH ** -0.5 * LOG2E)
    k = norm_rope(p_ref[:, DA_C:2 * DA_C], kn_ref[...])
    q_o[...] = q.astype(BF16)
    k_o[...] = k.astype(BF16)
    vt = jnp.transpose(p_ref[:, 2 * DA_C:3 * DA_C]).astype(BF16)
    hw = 2 * DA_DH
    for h in range(DA_HEADS):
        v_o[h, 0:hw, :] = vt[h * hw:(h + 1) * hw, :]
        v_o[h, hw:hw + ONES_ROWS, :] = jnp.ones((ONES_ROWS, vt.shape[1]), BF16)


def _rope_lane_tables(s_len):
    half = ROT_DIM // 2
    inv_freq = ROPE_THETA ** (-jnp.arange(0, ROT_DIM, 2, dtype=F32) / ROT_DIM)
    ang = jnp.arange(s_len, dtype=F32)[:, None] * inv_freq[None, :]
    cos, sin = jnp.cos(ang), jnp.sin(ang)
    j = jnp.arange(DA_C) % DA_DH
    f = j % half
    cf = jnp.where(j < ROT_DIM, cos[:, f], 1.0)
    sn = jnp.where(j < half, -sin[:, f], 0.0)
    sp = jnp.where((j >= half) & (j < ROT_DIM), sin[:, f], 0.0)
    return cf, sn, sp


def _daprep(proj_da, bsz, s_len, q_norm, k_norm, bd):
    t = proj_da.shape[0]
    tm = min(ATT_TK, s_len)
    per_seq = s_len // tm
    cf, sn, sp = _rope_lane_tables(s_len)
    const2 = lambda i: (0, 0)
    tok = pl.BlockSpec((tm, DA_C), lambda i: (i, 0))
    tab = pl.BlockSpec((tm, DA_C), lambda i: (i % per_seq, 0))
    vec = pl.BlockSpec((1, DA_C), const2)
    out = jax.ShapeDtypeStruct((t, DA_C), BF16)
    reps = DA_C // DA_DH
    return pl.pallas_call(
        _daprep_kernel,
        grid=(t // tm,),
        in_specs=[pl.BlockSpec((tm, 3 * DA_C), lambda i: (i, 0)), tab, tab, tab, vec, vec,
                  pl.BlockSpec((DA_C, DA_C), const2)],
        out_specs=[tok, tok, pl.BlockSpec((None, None, DA_HEADS, VT_ROWS, tm),
                                          lambda i: (i // per_seq, i % per_seq, 0, 0, 0))],
        out_shape=[out, out, jax.ShapeDtypeStruct((bsz, per_seq, DA_HEADS, VT_ROWS, tm), BF16)],
        compiler_params=_params(("parallel",), 32),
        name="daprep",
    )(proj_da, cf, sn, sp, jnp.tile(q_norm, reps).reshape(1, DA_C), jnp.tile(k_norm, reps).reshape(1, DA_C), bd)


def _attn_kernel(q_ref, k_ref, vt_ref, lq1_ref, lk1_ref, lq2_ref, lk2_ref, sub_ref, o_ref,
                 acc_ref, sa_ref, sb_ref, pa_ref, pb_ref, *, tq, tk):
    q = q_ref[...]
    lane = lax.broadcasted_iota(jnp.int32, q.shape, 1)
    zero = jnp.zeros_like(q)
    qs = jnp.concatenate([jnp.where(lane < DA_DH, q, zero), jnp.where(lane >= DA_DH, q, zero)], axis=0)
    nk = k_ref.shape[0] // tk
    acc_ref[...] = jnp.zeros_like(acc_ref)
    pb_ref[...] = jnp.zeros_like(pb_ref)

    def scores(i):
        return _dot_nt(k_ref[pl.ds(pl.multiple_of(i * tk, tk), tk), :], qs)

    def softmax(s_ref, p_ref, m_prev):
        s = s_ref[...]
        m_new = jnp.maximum(m_prev, jnp.max(s, axis=0, keepdims=True))
        p_ref[...] = jnp.exp2((s - m_new).astype(BF16))
        return m_new, jnp.exp2(m_prev - m_new)

    def accumulate(p_ref, i, corr):
        acc_ref[...] = corr * acc_ref[...] + _dot(vt_ref[i], p_ref[...])

    def pair(j, carry, more):
        m, corr_b = carry
        sb_ref[...] = scores(2 * j + 1)
        accumulate(pb_ref, jnp.maximum(2 * j - 1, 0), corr_b)
        m, corr_a = softmax(sa_ref, pa_ref, m)
        if more:
            sa_ref[...] = scores(2 * j + 2)
        accumulate(pa_ref, 2 * j, corr_a)
        m, corr_b = softmax(sb_ref, pb_ref, m)
        return m, corr_b

    sa_ref[...] = scores(0)
    carry = (jnp.full((1, 2 * tq), -jnp.inf, F32), jnp.ones((1, 2 * tq), F32))
    carry = lax.fori_loop(0, nk // 2 - 1, functools.partial(pair, more=True), carry)
    _, corr_b = pair(nk // 2 - 1, carry, more=False)
    accumulate(pb_ref, nk - 1, corr_b)

    hw = 2 * DA_DH
    o2 = acc_ref[0:hw, :] / acc_ref[hw:hw + 1, :]
    lam = (jnp.exp(jnp.sum(lq1_ref[...] * lk1_ref[...], axis=1, keepdims=True))
           - jnp.exp(jnp.sum(lq2_ref[...] * lk2_ref[...], axis=1, keepdims=True))) + LAM_INIT
    o = jnp.transpose(o2[:, 0:tq] - lam * o2[:, tq:2 * tq])
    o_ref[...] = _rms(o, sub_ref[...]) * (1.0 - LAM_INIT)


def _attn(q, k, vt, bsz, s_len, lq1, lk1, lq2, lk2, subln):
    tq = min(ATT_TQ, s_len)
    tk = min(ATT_TK, s_len)
    nk = s_len // tk
    assert nk % 2 == 0, "the kv loop consumes blocks in pairs"
    hw = 2 * DA_DH
    shp = lambda a: a.reshape(bsz, s_len, DA_C)
    lvec = pl.BlockSpec((1, DA_DH), lambda b, h, qi: (0, 0))
    o = pl.pallas_call(
        functools.partial(_attn_kernel, tq=tq, tk=tk),
        grid=(bsz, DA_HEADS, s_len // tq),
        in_specs=[
            pl.BlockSpec((None, tq, hw), lambda b, h, qi: (b, qi, h)),
            pl.BlockSpec((None, s_len, hw), lambda b, h, qi: (b, 0, h)),
            pl.BlockSpec((None, nk, None, VT_ROWS, tk), lambda b, h, qi: (b, 0, h, 0, 0)),
            lvec, lvec, lvec, lvec,
            pl.BlockSpec((1, hw), lambda b, h, qi: (0, 0)),
        ],
        out_specs=pl.BlockSpec((None, tq, hw), lambda b, h, qi: (b, qi, h)),
        out_shape=jax.ShapeDtypeStruct((bsz, s_len, DA_C), F32),
        scratch_shapes=[pltpu.VMEM((VT_ROWS, 2 * tq), F32),
                        pltpu.VMEM((tk, 2 * tq), F32), pltpu.VMEM((tk, 2 * tq), F32),
                        pltpu.VMEM((tk, 2 * tq), BF16), pltpu.VMEM((tk, 2 * tq), BF16)],
        compiler_params=_params(("parallel", "parallel", "arbitrary"), 32),
        name="attn",
    )(shp(q), shp(k), vt, lq1.reshape(1, DA_DH), lk1.reshape(1, DA_DH), lq2.reshape(1, DA_DH),
      lk2.reshape(1, DA_DH), subln.reshape(1, hw))
    return o.reshape(bsz * s_len, DA_C)


def _mixout_kernel(x_ref, yf_ref, yr_ref, g_ref, bonus_ref, da_ref, lng_ref, lnb_ref, bd_ref, wrw_ref, wda_ref, o_ref):
    bd = bd_ref[...]
    y = yf_ref[...] + yr_ref[...]
    mu = _group_sum(y, bd) * (1.0 / RW_N)
    yc = y - mu
    var = _group_sum(yc * yc, bd) * (1.0 / RW_N)
    yn = yc * lax.rsqrt(var + GN_EPS) * lng_ref[...] + lnb_ref[...]
    y_rw = ((yn + bonus_ref[...]) * g_ref[...]).astype(BF16)
    mixed = _dot(y_rw, wrw_ref[...]) + _dot(da_ref[...].astype(BF16), wda_ref[...])
    o_ref[...] = x_ref[...] + mixed


def _mixout(x1, yf, yr, g, bonus, o_da, ln_g, ln_b, bd, w_out):
    t = x1.shape[0]
    tm = min(TOKEN_TILE, t)
    const2 = lambda i: (0, 0)
    tok = pl.BlockSpec((tm, RW_C), lambda i: (i, 0))
    vec = pl.BlockSpec((1, RW_C), const2)
    return pl.pallas_call(
        _mixout_kernel,
        grid=(t // tm,),
        in_specs=[
            pl.BlockSpec((tm, D_MODEL), lambda i: (i, 0)),
            tok, tok, tok, tok, tok, vec, vec,
            pl.BlockSpec((RW_C, RW_C), const2),
            pl.BlockSpec((RW_C, D_MODEL), const2),
            pl.BlockSpec((DA_C, D_MODEL), const2),
        ],
        out_specs=pl.BlockSpec((tm, D_MODEL), lambda i: (i, 0)),
        out_shape=jax.ShapeDtypeStruct((t, D_MODEL), F32),
        compiler_params=_params(("parallel",), 48),
        name="mixout",
    )(x1, yf, yr, g, bonus, o_da, ln_g.reshape(1, RW_C), ln_b.reshape(1, RW_C), bd,
      w_out[:RW_C].astype(BF16), w_out[RW_C:].astype(BF16))


def _encoder_layer(x, p):
    bsz, s_len, _ = x.shape
    t = bsz * s_len
    xf = x.reshape(t, D_MODEL)
    lane = jnp.arange(RW_C) // RW_N
    bd = (lane[:, None] == lane[None, :]).astype(BF16)

    x1 = _ffn(xf, p["ffn1_norm"], p["ffn1_w_gu"], p["ffn1_w_down"])
    proj_rw, proj_da = _proj(x1, p["mix_norm"], p["w_in"])
    r, v, kk, g, bonus, lw, kd, b = _rwprep(
        proj_rw, s_len, p["conv_w"], p["rw_w0"], p["rw_w_up"], p["rw_a0"], p["rw_a_up"], p["rw_g_up"],
        p["rw_k_k"], p["rw_k_a"], p["rw_r_k"], bd)
    yf, yr = _scan(r, v, kk, lw, kd, b, bsz, s_len)
    q, k, va = _daprep(proj_da, bsz, s_len, p["da_q_norm"], p["da_k_norm"], bd)
    o_da = _attn(q, k, va, bsz, s_len, p["da_lq1"], p["da_lk1"], p["da_lq2"], p["da_lk2"], p["da_subln"])
    x2 = _mixout(x1, yf, yr, g, bonus, o_da, p["rw_ln_g"], p["rw_ln_b"], bd, p["w_out"])
    out = _ffn(x2, p["ffn2_norm"], p["ffn2_w_gu"], p["ffn2_w_down"], p["final_norm"])
    return out.reshape(bsz, s_len, D_MODEL)


def kernel(x_prompt, x_sample, ffn1_norm, ffn1_w_gu, ffn1_w_down, mix_norm, w_in, conv_w, rw_w0, rw_w_up, rw_a0, rw_a_up, rw_g_up, rw_k_k, rw_k_a, rw_r_k, rw_ln_g, rw_ln_b, da_q_norm, da_k_norm, da_lq1, da_lk1, da_lq2, da_lk2, da_subln, w_out, ffn2_norm, ffn2_w_gu, ffn2_w_down, final_norm):
    names = ("ffn1_norm", "ffn1_w_gu", "ffn1_w_down", "mix_norm", "w_in", "conv_w", "rw_w0", "rw_w_up", "rw_a0",
             "rw_a_up", "rw_g_up", "rw_k_k", "rw_k_a", "rw_r_k", "rw_ln_g", "rw_ln_b", "da_q_norm", "da_k_norm",
             "da_lq1", "da_lk1", "da_lq2", "da_lk2", "da_subln", "w_out", "ffn2_norm", "ffn2_w_gu", "ffn2_w_down",
             "final_norm")
    vals = (ffn1_norm, ffn1_w_gu, ffn1_w_down, mix_norm, w_in, conv_w, rw_w0, rw_w_up, rw_a0, rw_a_up, rw_g_up,
            rw_k_k, rw_k_a, rw_r_k, rw_ln_g, rw_ln_b, da_q_norm, da_k_norm, da_lq1, da_lk1, da_lq2, da_lk2,
            da_subln, w_out, ffn2_norm, ffn2_w_gu, ffn2_w_down, final_norm)
    assert all(a.shape[0] == 1 for a in vals), "single-layer (depth 1) parameter stacks expected"
    p = {n: a[0] for n, a in zip(names, vals)}
    return (_encoder_layer(x_prompt, p), _encoder_layer(x_sample, p))
```

```python
import functools
import math

import jax
import jax.numpy as jnp
from jax import lax
from jax.experimental import pallas as pl
from jax.experimental.pallas import tpu as pltpu

F32 = jnp.float32
BF16 = jnp.bfloat16

D_MODEL = 1024
D_FF = 2816
RW_HEADS = 8
RW_N = 64
RW_C = RW_HEADS * RW_N
LORA_W = 64
LORA_A = 64
LORA_G = 128
DECAY_SCALE = 0.606531
GN_EPS = 64e-5
DA_HEADS = 4
DA_DH = 64
DA_C = DA_HEADS * 2 * DA_DH
ROT_DIM = DA_DH // 4
ROPE_THETA = 500000.0
NORM_EPS = 1e-6
RW_COLS = 3 * RW_C + 2 * LORA_W + 2 * LORA_A + LORA_G
LAM_INIT = 0.8 - 0.6 * math.exp(-0.3 * 0)
LOG2E = math.log2(math.e)

V7X_VMEM_BYTES = 64 * 1024 * 1024
SUBLANES = 8

FF_CHUNK = 256
N_FF_CHUNKS = D_FF // FF_CHUNK
TOKEN_TILE = 512
PREP_TILE = 256
CHUNK = 64
SCAN_CHUNKS_PER_STEP = 4
ATT_TQ = 512
ONES_ROWS = 16
VT_ROWS = 2 * DA_DH + ONES_ROWS
ATT_TK = 512


def _params(sem, vmem_mb):
    return pltpu.CompilerParams(dimension_semantics=sem, vmem_limit_bytes=vmem_mb * 1024 * 1024)


def _dot(a, b):
    return jnp.dot(a, b, preferred_element_type=F32)


def _dot_nt(a, b):
    return lax.dot_general(a, b, (((1,), (1,)), ((), ())), preferred_element_type=F32)


def _dot_tn(a, b):
    return lax.dot_general(a, b, (((0,), (0,)), ((), ())), preferred_element_type=F32)


def _split2(x):
    hi = x.astype(BF16)
    lo = (x - hi.astype(F32)).astype(BF16)
    return hi, lo


def _split3(x):
    hi = x.astype(BF16)
    r1 = x - hi.astype(F32)
    mid = r1.astype(BF16)
    lo = (r1 - mid.astype(F32)).astype(BF16)
    return hi, mid, lo


def _group_sum(x, bd):
    hi, lo = _split2(x)
    return _dot(hi, bd) + _dot(lo, bd)


def _rms(x, g, eps=NORM_EPS):
    return x * lax.rsqrt(jnp.mean(x * x, axis=-1, keepdims=True) + eps) * g


def _ffn_kernel(x_ref, g_ref, wg_ref, wu_ref, wd_ref, *rest, final):
    if final:
        fg_ref, o_ref = rest
    else:
        (o_ref,) = rest
    x = x_ref[...]
    h = _rms(x, g_ref[...]).astype(BF16)
    acc = jnp.zeros_like(x)
    for j in range(N_FF_CHUNKS):
        gate = _dot(h, wg_ref[j])
        up = _dot(h, wu_ref[j])
        act = (gate * jax.nn.sigmoid(gate) * up).astype(BF16)
        acc = acc + _dot(act, wd_ref[j])
    y = x + 0.5 * acc
    if final:
        y = _rms(y, fg_ref[...])
    o_ref[...] = y


def _ffn(x, norm_g, w_gu, w_down, final_g=None):
    t = x.shape[0]
    tm = min(TOKEN_TILE, t)
    wg = w_gu[:, :D_FF].reshape(D_MODEL, N_FF_CHUNKS, FF_CHUNK).transpose(1, 0, 2).astype(BF16)
    wu = w_gu[:, D_FF:].reshape(D_MODEL, N_FF_CHUNKS, FF_CHUNK).transpose(1, 0, 2).astype(BF16)
    wd = w_down.reshape(N_FF_CHUNKS, FF_CHUNK, D_MODEL).astype(BF16)
    const3 = lambda i: (0, 0, 0)
    const2 = lambda i: (0, 0)
    in_specs = [
        pl.BlockSpec((tm, D_MODEL), lambda i: (i, 0)),
        pl.BlockSpec((1, D_MODEL), const2),
        pl.BlockSpec((N_FF_CHUNKS, D_MODEL, FF_CHUNK), const3),
        pl.BlockSpec((N_FF_CHUNKS, D_MODEL, FF_CHUNK), const3),
        pl.BlockSpec((N_FF_CHUNKS, FF_CHUNK, D_MODEL), const3),
    ]
    args = [x, norm_g.reshape(1, D_MODEL), wg, wu, wd]
    if final_g is not None:
        in_specs.append(pl.BlockSpec((1, D_MODEL), const2))
        args.append(final_g.reshape(1, D_MODEL))
    return pl.pallas_call(
        functools.partial(_ffn_kernel, final=final_g is not None),
        grid=(t // tm,),
        in_specs=in_specs,
        out_specs=pl.BlockSpec((tm, D_MODEL), lambda i: (i, 0)),
        out_shape=jax.ShapeDtypeStruct((t, D_MODEL), F32),
        compiler_params=_params(("parallel",), 56),
        name="ffn_final" if final_g is not None else "ffn",
    )(*args)


def _proj_kernel(x_ref, g_ref, wrw_ref, wda_ref, rw_ref, da_ref):
    h = _rms(x_ref[...], g_ref[...]).astype(BF16)
    rw_ref[...] = _dot(h, wrw_ref[...])
    da_ref[...] = _dot(h, wda_ref[...])


def _proj(x, norm_g, w_in):
    t = x.shape[0]
    tm = min(TOKEN_TILE, t)
    w_rw = w_in[:, :RW_COLS].astype(BF16)
    w_da = w_in[:, RW_COLS:].astype(BF16)
    const2 = lambda i: (0, 0)
    return pl.pallas_call(
        _proj_kernel,
        grid=(t // tm,),
        in_specs=[
            pl.BlockSpec((tm, D_MODEL), lambda i: (i, 0)),
            pl.BlockSpec((1, D_MODEL), const2),
            pl.BlockSpec((D_MODEL, RW_COLS), const2),
            pl.BlockSpec((D_MODEL, 3 * DA_C), const2),
        ],
        out_specs=[
            pl.BlockSpec((tm, RW_COLS), lambda i: (i, 0)),
            pl.BlockSpec((tm, 3 * DA_C), lambda i: (i, 0)),
        ],
        out_shape=[
            jax.ShapeDtypeStruct((t, RW_COLS), F32),
            jax.ShapeDtypeStruct((t, 3 * DA_C), F32),
        ],
        compiler_params=_params(("parallel",), 48),
        name="proj",
    )(x, norm_g.reshape(1, D_MODEL), w_rw, w_da)


def _rwprep_kernel(u_ref, up_ref, un_ref, conv_ref, w0_ref, wup_ref, a0_ref, aup_ref, gup_ref,
                   kk_ref, ka_ref, rk_ref, bd_ref,
                   r_o, v_o, kk_o, g_o, bonus_o, lw_o, kd_o, b_o, *, tm, s_len):
    i = pl.program_id(0)
    u = u_ref[...]
    start = i * tm
    first = (start % s_len) == 0
    last = ((start + tm) % s_len) == 0
    prev_row = jnp.where(first, 0.0, up_ref[SUBLANES - 1:SUBLANES, :])
    next_row = jnp.where(last, 0.0, un_ref[0:1, :])
    row = lax.broadcasted_iota(jnp.int32, (tm, 1), 0)
    prev = jnp.where(row == 0, prev_row, pltpu.roll(u, 1, 0))
    nxt = jnp.where(row == tm - 1, next_row, pltpu.roll(u, tm - 1, 0))
    c = prev * conv_ref[0:1, :] + u * conv_ref[1:2, :] + nxt * conv_ref[2:3, :]

    o_w = 3 * RW_C
    o_a = o_w + 2 * LORA_W
    o_g = o_a + 2 * LORA_A
    r = c[:, 0:RW_C]
    k = c[:, RW_C:2 * RW_C]
    v = c[:, 2 * RW_C:3 * RW_C]
    tw = jnp.tanh(c[:, o_w:o_a]).astype(BF16)
    ad = c[:, o_a:o_g].astype(BF16)
    gd = c[:, o_g:RW_COLS]
    bd = bd_ref[...]

    kkv = k * kk_ref[...]
    norm = jnp.sqrt(_group_sum(kkv * kkv, bd))
    kk = kkv / jnp.maximum(norm, 1e-12)
    g = _dot(jax.nn.sigmoid(gd).astype(BF16), gup_ref[...])

    ksum = jnp.zeros_like(k)
    for d in range(2):
        lw = -DECAY_SCALE * jax.nn.sigmoid(w0_ref[d:d + 1, :] + _dot(tw, wup_ref[d]))
        a = jax.nn.sigmoid(a0_ref[d:d + 1, :] + _dot(ad, aup_ref[d]))
        kd = k * (1.0 + (a - 1.0) * ka_ref[...])
        lw_o[d] = lw
        kd_o[d] = kd
        b_o[d] = kk * a
        ksum = ksum + kd
    coef = _group_sum(r * ksum * rk_ref[...], bd)
    r_o[...] = r
    v_o[...] = v
    kk_o[...] = kk
    g_o[...] = g
    bonus_o[...] = coef * v


def _pad_dir_rows(w_up):
    r = w_up.shape[1]
    z = jnp.zeros_like(w_up[0])
    return jnp.stack([jnp.concatenate([w_up[0], z], 0), jnp.concatenate([z, w_up[1]], 0)]).astype(BF16)


def _rwprep(proj_rw, s_len, conv_w, w0, w_up, a0, a_up, g_up, k_k, k_a, r_k, bd):
    t = proj_rw.shape[0]
    tm = min(PREP_TILE, s_len)
    nb8 = t // SUBLANES
    per = tm // SUBLANES
    const2 = lambda i: (0, 0)
    const3 = lambda i: (0, 0, 0)
    tok = pl.BlockSpec((tm, RW_C), lambda i: (i, 0))
    tok2 = pl.BlockSpec((2, tm, RW_C), lambda i: (0, i, 0))
    vec = pl.BlockSpec((1, RW_C), const2)
    one = jax.ShapeDtypeStruct((t, RW_C), F32)
    two = jax.ShapeDtypeStruct((2, t, RW_C), F32)
    return pl.pallas_call(
        functools.partial(_rwprep_kernel, tm=tm, s_len=s_len),
        grid=(t // tm,),
        in_specs=[
            pl.BlockSpec((tm, RW_COLS), lambda i: (i, 0)),
            pl.BlockSpec((SUBLANES, RW_COLS), lambda i: (jnp.maximum(i * per - 1, 0), 0)),
            pl.BlockSpec((SUBLANES, RW_COLS), lambda i: (jnp.minimum((i + 1) * per, nb8 - 1), 0)),
            pl.BlockSpec((3, RW_COLS), const2),
            pl.BlockSpec((2, RW_C), const2),
            pl.BlockSpec((2, 2 * LORA_W, RW_C), const3),
            pl.BlockSpec((2, RW_C), const2),
            pl.BlockSpec((2, 2 * LORA_A, RW_C), const3),
            pl.BlockSpec((LORA_G, RW_C), const2),
            vec, vec, vec,
            pl.BlockSpec((RW_C, RW_C), const2),
        ],
        out_specs=[tok, tok, tok, tok, tok, tok2, tok2, tok2],
        out_shape=[one, one, one, one, one, two, two, two],
        compiler_params=_params(("parallel",), 48),
        name="rwprep",
    )(proj_rw, proj_rw, proj_rw, conv_w, w0, _pad_dir_rows(w_up), a0, _pad_dir_rows(a_up),
      g_up.astype(BF16), k_k.reshape(1, RW_C), k_a.reshape(1, RW_C), r_k.reshape(1, RW_C), bd)


def _scan_kernel(rf_ref, vf_ref, kkf_ref, lwf_ref, kdf_ref, bf_ref, rr_ref, vr_ref, kkr_ref, lwr_ref, kdr_ref, br_ref,
                 yf_ref, yr_ref, h_ref, *, chunk):
    L = chunk
    L2 = 2 * L
    PW = 2 * RW_N
    n_pairs = RW_C // PW
    n_sub = lwf_ref.shape[0] // L
    assert L2 == PW, "interaction matrices and lane pairs share the 128-lane tile"

    @pl.when(pl.program_id(1) == 0)
    def _():
        h_ref[...] = jnp.zeros_like(h_ref)

    row = lax.broadcasted_iota(jnp.int32, (L, L), 0)
    col = lax.broadcasted_iota(jnp.int32, (L, L), 1)
    row2 = lax.broadcasted_iota(jnp.int32, (L, L2), 0)
    col2 = lax.broadcasted_iota(jnp.int32, (L, L2), 1) & (L - 1)
    eye2 = (row2 == col2).astype(F32)
    first_head = lax.broadcasted_iota(jnp.int32, (L, PW), 1) < RW_N

    def stack(z):
        zero = jnp.zeros_like(z)
        return jnp.concatenate([jnp.where(first_head, z, zero), jnp.where(first_head, zero, z)], axis=0)

    chains = []
    dirs = ((rf_ref, vf_ref, kkf_ref, lwf_ref, kdf_ref, bf_ref, yf_ref, 1),
            (rr_ref, vr_ref, kkr_ref, lwr_ref, kdr_ref, br_ref, yr_ref, -1))
    for d, (r_ref, v_ref, kk_ref, lw_ref, kd_ref, b_ref, y_ref, sgn) in enumerate(dirs):
        incl2 = (row2 - col2) * sgn >= 0
        strict2 = (row2 - col2) * sgn > 0
        tri = ((row - col) * sgn >= 0).astype(BF16)
        for step, j in enumerate(range(n_sub) if sgn > 0 else reversed(range(n_sub))):
            rows = slice(j * L, (j + 1) * L)
            lw = lw_ref[rows, :]
            hi, mid, lo = _split3(lw)
            c_in = _dot(tri, hi) + _dot(tri, mid) + _dot(tri, lo)
            c_tot = jnp.sum(lw, axis=0, keepdims=True)
            e_neg = jnp.exp(-c_in)
            e_end = jnp.exp(c_tot - c_in)
            kk = kk_ref[rows, :]
            b = b_ref[rows, :]
            kd = kd_ref[rows, :]
            a_t = -kk * jnp.exp(c_in - lw)
            r_t = r_ref[rows, :] * jnp.exp(c_in)
            b_t = b * e_neg
            k_t = kd * e_neg
            b_p = b * e_end
            k_p = kd * e_end
            p_end = jnp.exp(c_tot)
            v = v_ref[rows, :]
            for p in range(n_pairs):
                ls = slice(p * PW, (p + 1) * PW)
                a_n = a_t[:, ls].astype(BF16)
                chains.append(dict(
                    d=d, p=p, step=step, rows=rows, ls=ls, y_ref=y_ref, incl=incl2, strict=strict2,
                    ar=jnp.concatenate([a_n, r_t[:, ls].astype(BF16)], axis=0),
                    bk=jnp.concatenate([stack(b_t[:, ls].astype(BF16)), stack(k_t[:, ls].astype(BF16))], axis=0),
                    aa=stack(a_n), vv=stack(v[:, ls].astype(BF16)),
                    bpT=jnp.transpose(stack(b_p[:, ls])).astype(BF16),
                    kpT=jnp.transpose(stack(k_p[:, ls])).astype(BF16),
                    pe_col=jnp.transpose(jnp.broadcast_to(p_end[:, ls], (PW, PW)))))

    for c in chains:
        x = _dot_nt(c["ar"], c["bk"])
        c["a_ab"] = jnp.where(c["strict"], x[:L, :L2], 0.0)
        c["a_ak"] = jnp.where(c["strict"], x[:L, L2:], 0.0).astype(BF16)
        c["a_rb"] = jnp.where(c["incl"], x[L:, :L2], 0.0).astype(BF16)
        c["a_rk"] = jnp.where(c["incl"], x[L:, L2:], 0.0).astype(BF16)
        c["inv"] = eye2 + c["a_ab"]
        c["pw"] = c["a_ab"].astype(BF16)
    n_sq = int(math.log2(L)) - 1
    for k in range(n_sq + 1):
        for c in chains:
            pws = stack(c["pw"])
            if k == 0:
                c["pw"] = _dot(c["pw"], pws).astype(BF16)
            elif k < n_sq:
                both = _dot(jnp.concatenate([c["pw"], c["inv"].astype(BF16)], axis=0), pws)
                c["pw"] = both[:L, :].astype(BF16)
                c["inv"] = c["inv"] + both[L:, :]
            else:
                c["inv"] = c["inv"] + _dot(c["inv"].astype(BF16), pws)
    for c in chains:
        c["akv"] = stack(_dot(c["a_ak"], c["vv"]).astype(BF16))
    for c in chains:
        inv = c["inv"].astype(BF16)
        c["w_t"] = _dot(inv, c["aa"]).astype(BF16)
        c["u_t"] = _dot(inv, c["akv"])
    state = {(d, p): h_ref[d, p] for d in range(2) for p in range(n_pairs)}
    for step in range(n_sub):
        now = [c for c in chains if c["step"] == step]
        for c in now:
            c["h"] = state[c["d"], c["p"]]
            c["hb"] = c["h"].astype(BF16)
            c["uu"] = stack((_dot(c["w_t"], c["hb"]) + c["u_t"]).astype(BF16))
        for c in now:
            state[c["d"], c["p"]] = c["pe_col"] * c["h"] + _dot(c["bpT"], c["uu"]) + _dot(c["kpT"], c["vv"])
        for c in now:
            c["y_ref"][c["rows"], c["ls"]] = (_dot(c["ar"][L:, :], c["hb"]) + _dot(c["a_rb"], c["uu"])
                                              + _dot(c["a_rk"], c["vv"]))
    for (d, p), h in state.items():
        h_ref[d, p] = h


def _scan(r, v, kk, lw, kd, b, bsz, s_len):
    rows = CHUNK * SCAN_CHUNKS_PER_STEP
    nc = s_len // rows
    shp = lambda a: a.reshape(bsz, s_len, RW_C)
    shp2 = lambda a: a.reshape(2, bsz, s_len, RW_C)
    f_sh = pl.BlockSpec((None, rows, RW_C), lambda bi, c: (bi, c, 0))
    r_sh = pl.BlockSpec((None, rows, RW_C), lambda bi, c: (bi, nc - 1 - c, 0))
    f_pd = pl.BlockSpec((None, None, rows, RW_C), lambda bi, c: (0, bi, c, 0))
    r_pd = pl.BlockSpec((None, None, rows, RW_C), lambda bi, c: (1, bi, nc - 1 - c, 0))
    out = jax.ShapeDtypeStruct((bsz, s_len, RW_C), F32)
    r3, v3, kk3, lw4, kd4, b4 = shp(r), shp(v), shp(kk), shp2(lw), shp2(kd), shp2(b)
    yf, yr = pl.pallas_call(
        functools.partial(_scan_kernel, chunk=CHUNK),
        grid=(bsz, nc),
        in_specs=[f_sh, f_sh, f_sh, f_pd, f_pd, f_pd, r_sh, r_sh, r_sh, r_pd, r_pd, r_pd],
        out_specs=[f_sh, r_sh],
        out_shape=[out, out],
        scratch_shapes=[pltpu.VMEM((2, RW_C // (2 * RW_N), 2 * RW_N, 2 * RW_N), F32)],
        compiler_params=_params(("parallel", "arbitrary"), 32),
        name="scan",
    )(r3, v3, kk3, lw4, kd4, b4, r3, v3, kk3, lw4, kd4, b4)
    return yf.reshape(bsz * s_len, RW_C), yr.reshape(bsz * s_len, RW_C)


def _daprep_kernel(p_ref, cf_ref, sn_ref, sp_ref, qn_ref, kn_ref, bd_ref, q_o, k_o, v_o):
    bd = bd_ref[...]
    cf, sn, sp = cf_ref[...], sn_ref[...], sp_ref[...]

    def norm_rope(x, g):
        ms = _group_sum(x * x, bd) * (1.0 / DA_DH)
        xn = x * lax.rsqrt(ms + NORM_EPS) * g
        half = ROT_DIM // 2
        return xn * cf + pltpu.roll(xn, DA_C - half, 1) * sn + pltpu.roll(xn, half, 1) * sp

    q = norm_rope(p_ref[:, 0:DA_C], qn_ref[...]) * (DA_DH ** -0.5 * LOG2E)
    k = norm_rope(p_ref[:, DA_C:2 * DA_C], kn_ref[...])
    q_o[...] = q.astype(BF16)
    k_o[...] = k.astype(BF16)
    vt = jnp.transpose(p_ref[:, 2 * DA_C:3 * DA_C]).astype(BF16)
    hw = 2 * DA_DH
    for h in range(DA_HEADS):
        v_o[h, 0:hw, :] = vt[h * hw:(h + 1) * hw, :]
        v_o[h, hw:hw + ONES_ROWS, :] = jnp.ones((ONES_ROWS, vt.shape[1]), BF16)


def _rope_lane_tables(s_len):
    half = ROT_DIM // 2
    inv_freq = ROPE_THETA ** (-jnp.arange(0, ROT_DIM, 2, dtype=F32) / ROT_DIM)
    ang = jnp.arange(s_len, dtype=F32)[:, None] * inv_freq[None, :]
    cos, sin = jnp.cos(ang), jnp.sin(ang)
    j = jnp.arange(DA_C) % DA_DH
    f = j % half
    cf = jnp.where(j < ROT_DIM, cos[:, f], 1.0)
    sn = jnp.where(j < half, -sin[:, f], 0.0)
    sp = jnp.where((j >= half) & (j < ROT_DIM), sin[:, f], 0.0)
    return cf, sn, sp


def _daprep(proj_da, bsz, s_len, q_norm, k_norm, bd):
    t = proj_da.shape[0]
    tm = min(ATT_TK, s_len)
    per_seq = s_len // tm
    cf, sn, sp = _rope_lane_tables(s_len)
    const2 = lambda i: (0, 0)
    tok = pl.BlockSpec((tm, DA_C), lambda i: (i, 0))
    tab = pl.BlockSpec((tm, DA_C), lambda i: (i % per_seq, 0))
    vec = pl.BlockSpec((1, DA_C), const2)
    out = jax.ShapeDtypeStruct((t, DA_C), BF16)
    reps = DA_C // DA_DH
    return pl.pallas_call(
        _daprep_kernel,
        grid=(t // tm,),
        in_specs=[pl.BlockSpec((tm, 3 * DA_C), lambda i: (i, 0)), tab, tab, tab, vec, vec,
                  pl.BlockSpec((DA_C, DA_C), const2)],
        out_specs=[tok, tok, pl.BlockSpec((None, None, DA_HEADS, VT_ROWS, tm),
                                          lambda i: (i // per_seq, i % per_seq, 0, 0, 0))],
        out_shape=[out, out, jax.ShapeDtypeStruct((bsz, per_seq, DA_HEADS, VT_ROWS, tm), BF16)],
        compiler_params=_params(("parallel",), 32),
        name="daprep",
    )(proj_da, cf, sn, sp, jnp.tile(q_norm, reps).reshape(1, DA_C), jnp.tile(k_norm, reps).reshape(1, DA_C), bd)


def _attn_kernel(q_ref, k_ref, vt_ref, lq1_ref, lk1_ref, lq2_ref, lk2_ref, sub_ref, o_ref,
                 acc_ref, sa_ref, sb_ref, pa_ref, pb_ref, *, tq, tk):
    q = q_ref[...]
    lane = lax.broadcasted_iota(jnp.int32, q.shape, 1)
    zero = jnp.zeros_like(q)
    qs = jnp.concatenate([jnp.where(lane < DA_DH, q, zero), jnp.where(lane >= DA_DH, q, zero)], axis=0)
    nk = k_ref.shape[0] // tk

    def scores(i):
        return _dot_nt(k_ref[i * tk:(i + 1) * tk, :], qs)

    def softmax(s_ref, p_ref, m_prev):
        s = s_ref[...]
        m_new = jnp.maximum(m_prev, jnp.max(s, axis=0, keepdims=True))
        p_ref[...] = jnp.exp2((s - m_new).astype(BF16))
        return m_new, jnp.exp2(m_prev - m_new)

    def accumulate(p_ref, i, corr):
        pv = _dot(vt_ref[i], p_ref[...])
        acc_ref[...] = pv if i == 0 else corr * acc_ref[...] + pv

    s_refs = (sa_ref, sb_ref)
    p_refs = (pa_ref, pb_ref)
    s_refs[0][...] = scores(0)
    m = jnp.full((1, 2 * tq), -jnp.inf, F32)
    corr = None
    for i in range(nk):
        if i + 1 < nk:
            s_refs[(i + 1) % 2][...] = scores(i + 1)
        if i > 0:
            accumulate(p_refs[(i - 1) % 2], i - 1, corr)
        m, corr = softmax(s_refs[i % 2], p_refs[i % 2], m)
    accumulate(p_refs[(nk - 1) % 2], nk - 1, corr)

    hw = 2 * DA_DH
    o2 = acc_ref[0:hw, :] / acc_ref[hw:hw + 1, :]
    lam = (jnp.exp(jnp.sum(lq1_ref[...] * lk1_ref[...], axis=1, keepdims=True))
           - jnp.exp(jnp.sum(lq2_ref[...] * lk2_ref[...], axis=1, keepdims=True))) + LAM_INIT
    o = jnp.transpose(o2[:, 0:tq] - lam * o2[:, tq:2 * tq])
    o_ref[...] = _rms(o, sub_ref[...]) * (1.0 - LAM_INIT)


def _attn(q, k, vt, bsz, s_len, lq1, lk1, lq2, lk2, subln):
    tq = min(ATT_TQ, s_len)
    tk = min(ATT_TK, s_len)
    nk = s_len // tk
    hw = 2 * DA_DH
    shp = lambda a: a.reshape(bsz, s_len, DA_C)
    lvec = pl.BlockSpec((1, DA_DH), lambda b, h, qi: (0, 0))
    o = pl.pallas_call(
        functools.partial(_attn_kernel, tq=tq, tk=tk),
        grid=(bsz, DA_HEADS, s_len // tq),
        in_specs=[
            pl.BlockSpec((None, tq, hw), lambda b, h, qi: (b, qi, h)),
            pl.BlockSpec((None, s_len, hw), lambda b, h, qi: (b, 0, h)),
            pl.BlockSpec((None, nk, None, VT_ROWS, tk), lambda b, h, qi: (b, 0, h, 0, 0)),
            lvec, lvec, lvec, lvec,
            pl.BlockSpec((1, hw), lambda b, h, qi: (0, 0)),
        ],
        out_specs=pl.BlockSpec((None, tq, hw), lambda b, h, qi: (b, qi, h)),
        out_shape=jax.ShapeDtypeStruct((bsz, s_len, DA_C), F32),
        scratch_shapes=[pltpu.VMEM((VT_ROWS, 2 * tq), F32),
                        pltpu.VMEM((tk, 2 * tq), F32), pltpu.VMEM((tk, 2 * tq), F32),
                        pltpu.VMEM((tk, 2 * tq), BF16), pltpu.VMEM((tk, 2 * tq), BF16)],
        compiler_params=_params(("parallel", "parallel", "arbitrary"), 32),
        name="attn",
    )(shp(q), shp(k), vt, lq1.reshape(1, DA_DH), lk1.reshape(1, DA_DH), lq2.reshape(1, DA_DH),
      lk2.reshape(1, DA_DH), subln.reshape(1, hw))
    return o.reshape(bsz * s_len, DA_C)


def _mixout_kernel(x_ref, yf_ref, yr_ref, g_ref, bonus_ref, da_ref, lng_ref, lnb_ref, bd_ref, wrw_ref, wda_ref, o_ref):
    bd = bd_ref[...]
    y = yf_ref[...] + yr_ref[...]
    mu = _group_sum(y, bd) * (1.0 / RW_N)
    yc = y - mu
    var = _group_sum(yc * yc, bd) * (1.0 / RW_N)
    yn = yc * lax.rsqrt(var + GN_EPS) * lng_ref[...] + lnb_ref[...]
    y_rw = ((yn + bonus_ref[...]) * g_ref[...]).astype(BF16)
    mixed = _dot(y_rw, wrw_ref[...]) + _dot(da_ref[...].astype(BF16), wda_ref[...])
    o_ref[...] = x_ref[...] + mixed


def _mixout(x1, yf, yr, g, bonus, o_da, ln_g, ln_b, bd, w_out):
    t = x1.shape[0]
    tm = min(TOKEN_TILE, t)
    const2 = lambda i: (0, 0)
    tok = pl.BlockSpec((tm, RW_C), lambda i: (i, 0))
    vec = pl.BlockSpec((1, RW_C), const2)
    return pl.pallas_call(
        _mixout_kernel,
        grid=(t // tm,),
        in_specs=[
            pl.BlockSpec((tm, D_MODEL), lambda i: (i, 0)),
            tok, tok, tok, tok, tok, vec, vec,
            pl.BlockSpec((RW_C, RW_C), const2),
            pl.BlockSpec((RW_C, D_MODEL), const2),
            pl.BlockSpec((DA_C, D_MODEL), const2),
        ],
        out_specs=pl.BlockSpec((tm, D_MODEL), lambda i: (i, 0)),
        out_shape=jax.ShapeDtypeStruct((t, D_MODEL), F32),
        compiler_params=_params(("parallel",), 48),
        name="mixout",
    )(x1, yf, yr, g, bonus, o_da, ln_g.reshape(1, RW_C), ln_b.reshape(1, RW_C), bd,
      w_out[:RW_C].astype(BF16), w_out[RW_C:].astype(BF16))


def _encoder_layer(x, p):
    bsz, s_len, _ = x.shape
    t = bsz * s_len
    xf = x.reshape(t, D_MODEL)
    lane = jnp.arange(RW_C) // RW_N
    bd = (lane[:, None] == lane[None, :]).astype(BF16)

    x1 = _ffn(xf, p["ffn1_norm"], p["ffn1_w_gu"], p["ffn1_w_down"])
    proj_rw, proj_da = _proj(x1, p["mix_norm"], p["w_in"])
    r, v, kk, g, bonus, lw, kd, b = _rwprep(
        proj_rw, s_len, p["conv_w"], p["rw_w0"], p["rw_w_up"], p["rw_a0"], p["rw_a_up"], p["rw_g_up"],
        p["rw_k_k"], p["rw_k_a"], p["rw_r_k"], bd)
    yf, yr = _scan(r, v, kk, lw, kd, b, bsz, s_len)
    q, k, va = _daprep(proj_da, bsz, s_len, p["da_q_norm"], p["da_k_norm"], bd)
    o_da = _attn(q, k, va, bsz, s_len, p["da_lq1"], p["da_lk1"], p["da_lq2"], p["da_lk2"], p["da_subln"])
    x2 = _mixout(x1, yf, yr, g, bonus, o_da, p["rw_ln_g"], p["rw_ln_b"], bd, p["w_out"])
    out = _ffn(x2, p["ffn2_norm"], p["ffn2_w_gu"], p["ffn2_w_down"], p["final_norm"])
    return out.reshape(bsz, s_len, D_MODEL)


def kernel(x_prompt, x_sample, ffn1_norm, ffn1_w_gu, ffn1_w_down, mix_norm, w_in, conv_w, rw_w0, rw_w_up, rw_a0, rw_a_up, rw_g_up, rw_k_k, rw_k_a, rw_r_k, rw_ln_g, rw_ln_b, da_q_norm, da_k_norm, da_lq1, da_lk1, da_lq2, da_lk2, da_subln, w_out, ffn2_norm, ffn2_w_gu, ffn2_w_down, final_norm):
    names = ("ffn1_norm", "ffn1_w_gu", "ffn1_w_down", "mix_norm", "w_in", "conv_w", "rw_w0", "rw_w_up", "rw_a0",
             "rw_a_up", "rw_g_up", "rw_k_k", "rw_k_a", "rw_r_k", "rw_ln_g", "rw_ln_b", "da_q_norm", "da_k_norm",
             "da_lq1", "da_lk1", "da_lq2", "da_lk2", "da_subln", "w_out", "ffn2_norm", "ffn2_w_gu", "ffn2_w_down",
             "final_norm")
    vals = (ffn1_norm, ffn1_w_gu, ffn1_w_down, mix_norm, w_in, conv_w, rw_w0, rw_w_up, rw_a0, rw_a_up, rw_g_up,
            rw_k_k, rw_k_a, rw_r_k, rw_ln_g, rw_ln_b, da_q_norm, da_k_norm, da_lq1, da_lk1, da_lq2, da_lk2,
            da_subln, w_out, ffn2_norm, ffn2_w_gu, ffn2_w_down, final_norm)
    assert all(a.shape[0] == 1 for a in vals), "single-layer (depth 1) parameter stacks expected"
    p = {n: a[0] for n, a in zip(names, vals)}
    return (_encoder_layer(x_prompt, p), _encoder_layer(x_sample, p))
```

```python
import functools
import math

import jax
import jax.numpy as jnp
from jax import lax
from jax.experimental import pallas as pl
from jax.experimental.pallas import tpu as pltpu

F32 = jnp.float32
BF16 = jnp.bfloat16

D_MODEL = 1024
D_FF = 2816
RW_HEADS = 8
RW_N = 64
RW_C = RW_HEADS * RW_N
LORA_W = 64
LORA_A = 64
LORA_G = 128
DECAY_SCALE = 0.606531
GN_EPS = 64e-5
DA_HEADS = 4
DA_DH = 64
DA_C = DA_HEADS * 2 * DA_DH
ROT_DIM = DA_DH // 4
ROPE_THETA = 500000.0
NORM_EPS = 1e-6
RW_COLS = 3 * RW_C + 2 * LORA_W + 2 * LORA_A + LORA_G
LAM_INIT = 0.8 - 0.6 * math.exp(-0.3 * 0)
LOG2E = math.log2(math.e)

V7X_VMEM_BYTES = 64 * 1024 * 1024
SUBLANES = 8

FF_CHUNK = 256
N_FF_CHUNKS = D_FF // FF_CHUNK
TOKEN_TILE = 512
PREP_TILE = 256
CHUNK = 64
SCAN_CHUNKS_PER_STEP = 4
ATT_TQ = 512
ONES_ROWS = 16
VT_ROWS = 2 * DA_DH + ONES_ROWS
ATT_TK = 512


def _params(sem, vmem_mb):
    return pltpu.CompilerParams(dimension_semantics=sem, vmem_limit_bytes=vmem_mb * 1024 * 1024)


def _dot(a, b):
    return jnp.dot(a, b, preferred_element_type=F32)


def _dot_nt(a, b):
    return lax.dot_general(a, b, (((1,), (1,)), ((), ())), preferred_element_type=F32)


def _dot_tn(a, b):
    return lax.dot_general(a, b, (((0,), (0,)), ((), ())), preferred_element_type=F32)


def _split2(x):
    hi = x.astype(BF16)
    lo = (x - hi.astype(F32)).astype(BF16)
    return hi, lo


def _split3(x):
    hi = x.astype(BF16)
    r1 = x - hi.astype(F32)
    mid = r1.astype(BF16)
    lo = (r1 - mid.astype(F32)).astype(BF16)
    return hi, mid, lo


def _group_sum(x, bd):
    hi, lo = _split2(x)
    return _dot(hi, bd) + _dot(lo, bd)


def _rms(x, g, eps=NORM_EPS):
    return x * lax.rsqrt(jnp.mean(x * x, axis=-1, keepdims=True) + eps) * g


def _ffn_kernel(x_ref, g_ref, wg_ref, wu_ref, wd_ref, *rest, final):
    if final:
        fg_ref, o_ref = rest
    else:
        (o_ref,) = rest
    x = x_ref[...]
    h = _rms(x, g_ref[...]).astype(BF16)
    acc = jnp.zeros_like(x)
    for j in range(N_FF_CHUNKS):
        gate = _dot(h, wg_ref[j])
        up = _dot(h, wu_ref[j])
        act = (gate * jax.nn.sigmoid(gate) * up).astype(BF16)
        acc = acc + _dot(act, wd_ref[j])
    y = x + 0.5 * acc
    if final:
        y = _rms(y, fg_ref[...])
    o_ref[...] = y


def _ffn(x, norm_g, w_gu, w_down, final_g=None):
    t = x.shape[0]
    tm = min(TOKEN_TILE, t)
    wg = w_gu[:, :D_FF].reshape(D_MODEL, N_FF_CHUNKS, FF_CHUNK).transpose(1, 0, 2).astype(BF16)
    wu = w_gu[:, D_FF:].reshape(D_MODEL, N_FF_CHUNKS, FF_CHUNK).transpose(1, 0, 2).astype(BF16)
    wd = w_down.reshape(N_FF_CHUNKS, FF_CHUNK, D_MODEL).astype(BF16)
    const3 = lambda i: (0, 0, 0)
    const2 = lambda i: (0, 0)
    in_specs = [
        pl.BlockSpec((tm, D_MODEL), lambda i: (i, 0)),
        pl.BlockSpec((1, D_MODEL), const2),
        pl.BlockSpec((N_FF_CHUNKS, D_MODEL, FF_CHUNK), const3),
        pl.BlockSpec((N_FF_CHUNKS, D_MODEL, FF_CHUNK), const3),
        pl.BlockSpec((N_FF_CHUNKS, FF_CHUNK, D_MODEL), const3),
    ]
    args = [x, norm_g.reshape(1, D_MODEL), wg, wu, wd]
    if final_g is not None:
        in_specs.append(pl.BlockSpec((1, D_MODEL), const2))
        args.append(final_g.reshape(1, D_MODEL))
    return pl.pallas_call(
        functools.partial(_ffn_kernel, final=final_g is not None),
        grid=(t // tm,),
        in_specs=in_specs,
        out_specs=pl.BlockSpec((tm, D_MODEL), lambda i: (i, 0)),
        out_shape=jax.ShapeDtypeStruct((t, D_MODEL), F32),
        compiler_params=_params(("parallel",), 56),
        name="ffn_final" if final_g is not None else "ffn",
    )(*args)


def _rwprep_kernel(x_ref, xp_ref, xn_ref, g_ref, w_ref, conv_ref, w0_ref, wup_ref, a0_ref, aup_ref, gup_ref,
                   kk_ref, ka_ref, rk_ref, bd_ref,
                   r_o, v_o, kk_o, g_o, bonus_o, lw_o, kd_o, b_o, *, tm, s_len):
    i = pl.program_id(0)
    xa = jnp.concatenate([x_ref[...], xp_ref[...], xn_ref[...]], axis=0)
    ua = _dot(_rms(xa, g_ref[...]).astype(BF16), w_ref[...])
    u = ua[0:tm, :]
    start = i * tm
    first = (start % s_len) == 0
    last = ((start + tm) % s_len) == 0
    prev_row = jnp.where(first, 0.0, ua[tm + SUBLANES - 1:tm + SUBLANES, :])
    next_row = jnp.where(last, 0.0, ua[tm + SUBLANES:tm + SUBLANES + 1, :])
    row = lax.broadcasted_iota(jnp.int32, (tm, 1), 0)
    prev = jnp.where(row == 0, prev_row, pltpu.roll(u, 1, 0))
    nxt = jnp.where(row == tm - 1, next_row, pltpu.roll(u, tm - 1, 0))
    c = prev * conv_ref[0:1, :] + u * conv_ref[1:2, :] + nxt * conv_ref[2:3, :]

    o_w = 3 * RW_C
    o_a = o_w + 2 * LORA_W
    o_g = o_a + 2 * LORA_A
    r = c[:, 0:RW_C]
    k = c[:, RW_C:2 * RW_C]
    v = c[:, 2 * RW_C:3 * RW_C]
    tw = jnp.tanh(c[:, o_w:o_a]).astype(BF16)
    ad = c[:, o_a:o_g].astype(BF16)
    gd = c[:, o_g:RW_COLS]
    bd = bd_ref[...]

    kkv = k * kk_ref[...]
    norm = jnp.sqrt(_group_sum(kkv * kkv, bd))
    kk = kkv / jnp.maximum(norm, 1e-12)
    g = _dot(jax.nn.sigmoid(gd).astype(BF16), gup_ref[...])

    ksum = jnp.zeros_like(k)
    for d in range(2):
        lw = -DECAY_SCALE * jax.nn.sigmoid(w0_ref[d:d + 1, :] + _dot(tw, wup_ref[d]))
        a = jax.nn.sigmoid(a0_ref[d:d + 1, :] + _dot(ad, aup_ref[d]))
        kd = k * (1.0 + (a - 1.0) * ka_ref[...])
        lw_o[d] = lw
        kd_o[d] = kd
        b_o[d] = kk * a
        ksum = ksum + kd
    coef = _group_sum(r * ksum * rk_ref[...], bd)
    r_o[...] = r
    v_o[...] = v
    kk_o[...] = kk
    g_o[...] = g
    bonus_o[...] = coef * v


def _pad_dir_rows(w_up):
    r = w_up.shape[1]
    z = jnp.zeros_like(w_up[0])
    return jnp.stack([jnp.concatenate([w_up[0], z], 0), jnp.concatenate([z, w_up[1]], 0)]).astype(BF16)


def _rwprep(x1, s_len, norm_g, w_rw, conv_w, w0, w_up, a0, a_up, g_up, k_k, k_a, r_k, bd):
    t = x1.shape[0]
    tm = min(PREP_TILE, s_len)
    nb8 = t // SUBLANES
    per = tm // SUBLANES
    const2 = lambda i: (0, 0)
    const3 = lambda i: (0, 0, 0)
    tok = pl.BlockSpec((tm, RW_C), lambda i: (i, 0))
    tok2 = pl.BlockSpec((2, tm, RW_C), lambda i: (0, i, 0))
    vec = pl.BlockSpec((1, RW_C), const2)
    one = jax.ShapeDtypeStruct((t, RW_C), F32)
    two = jax.ShapeDtypeStruct((2, t, RW_C), F32)
    return pl.pallas_call(
        functools.partial(_rwprep_kernel, tm=tm, s_len=s_len),
        grid=(t // tm,),
        in_specs=[
            pl.BlockSpec((tm, D_MODEL), lambda i: (i, 0)),
            pl.BlockSpec((SUBLANES, D_MODEL), lambda i: (jnp.maximum(i * per - 1, 0), 0)),
            pl.BlockSpec((SUBLANES, D_MODEL), lambda i: (jnp.minimum((i + 1) * per, nb8 - 1), 0)),
            pl.BlockSpec((1, D_MODEL), const2),
            pl.BlockSpec((D_MODEL, RW_COLS), const2),
            pl.BlockSpec((3, RW_COLS), const2),
            pl.BlockSpec((2, RW_C), const2),
            pl.BlockSpec((2, 2 * LORA_W, RW_C), const3),
            pl.BlockSpec((2, RW_C), const2),
            pl.BlockSpec((2, 2 * LORA_A, RW_C), const3),
            pl.BlockSpec((LORA_G, RW_C), const2),
            vec, vec, vec,
            pl.BlockSpec((RW_C, RW_C), const2),
        ],
        out_specs=[tok, tok, tok, tok, tok, tok2, tok2, tok2],
        out_shape=[one, one, one, one, one, two, two, two],
        compiler_params=_params(("parallel",), 48),
        name="rwprep",
    )(x1, x1, x1, norm_g.reshape(1, D_MODEL), w_rw.astype(BF16), conv_w, w0, _pad_dir_rows(w_up), a0, _pad_dir_rows(a_up),
      g_up.astype(BF16), k_k.reshape(1, RW_C), k_a.reshape(1, RW_C), r_k.reshape(1, RW_C), bd)


def _scan_kernel(rf_ref, vf_ref, kkf_ref, lwf_ref, kdf_ref, bf_ref, rr_ref, vr_ref, kkr_ref, lwr_ref, kdr_ref, br_ref,
                 yf_ref, yr_ref, h_ref, *, chunk):
    L = chunk
    L2 = 2 * L
    PW = 2 * RW_N
    n_pairs = RW_C // PW
    n_sub = lwf_ref.shape[0] // L
    assert L2 == PW, "interaction matrices and lane pairs share the 128-lane tile"

    @pl.when(pl.program_id(1) == 0)
    def _():
        h_ref[...] = jnp.zeros_like(h_ref)

    row = lax.broadcasted_iota(jnp.int32, (L, L), 0)
    col = lax.broadcasted_iota(jnp.int32, (L, L), 1)
    row2 = lax.broadcasted_iota(jnp.int32, (L, L2), 0)
    col2 = lax.broadcasted_iota(jnp.int32, (L, L2), 1) & (L - 1)
    eye2 = (row2 == col2).astype(F32)
    first_head = lax.broadcasted_iota(jnp.int32, (L, PW), 1) < RW_N

    def stack(z):
        zero = jnp.zeros_like(z)
        return jnp.concatenate([jnp.where(first_head, z, zero), jnp.where(first_head, zero, z)], axis=0)

    chains = []
    dirs = ((rf_ref, vf_ref, kkf_ref, lwf_ref, kdf_ref, bf_ref, yf_ref, 1),
            (rr_ref, vr_ref, kkr_ref, lwr_ref, kdr_ref, br_ref, yr_ref, -1))
    for d, (r_ref, v_ref, kk_ref, lw_ref, kd_ref, b_ref, y_ref, sgn) in enumerate(dirs):
        incl2 = (row2 - col2) * sgn >= 0
        strict2 = (row2 - col2) * sgn > 0
        tri = ((row - col) * sgn >= 0).astype(BF16)
        for step, j in enumerate(range(n_sub) if sgn > 0 else reversed(range(n_sub))):
            rows = slice(j * L, (j + 1) * L)
            lw = lw_ref[rows, :]
            hi, mid, lo = _split3(lw)
            c_in = _dot(tri, hi) + _dot(tri, mid) + _dot(tri, lo)
            c_tot = jnp.sum(lw, axis=0, keepdims=True)
            e_neg = jnp.exp(-c_in)
            e_end = jnp.exp(c_tot - c_in)
            kk = kk_ref[rows, :]
            b = b_ref[rows, :]
            kd = kd_ref[rows, :]
            a_t = -kk * jnp.exp(c_in - lw)
            r_t = r_ref[rows, :] * jnp.exp(c_in)
            b_t = b * e_neg
            k_t = kd * e_neg
            b_p = b * e_end
            k_p = kd * e_end
            p_end = jnp.exp(c_tot)
            v = v_ref[rows, :]
            for p in range(n_pairs):
                ls = slice(p * PW, (p + 1) * PW)
                a_n = a_t[:, ls].astype(BF16)
                chains.append(dict(
                    d=d, p=p, step=step, rows=rows, ls=ls, y_ref=y_ref, incl=incl2, strict=strict2,
                    ar=jnp.concatenate([a_n, r_t[:, ls].astype(BF16)], axis=0),
                    bk=jnp.concatenate([stack(b_t[:, ls].astype(BF16)), stack(k_t[:, ls].astype(BF16))], axis=0),
                    aa=stack(a_n), vv=stack(v[:, ls].astype(BF16)),
                    bpT=jnp.transpose(stack(b_p[:, ls])).astype(BF16),
                    kpT=jnp.transpose(stack(k_p[:, ls])).astype(BF16),
                    pe_col=jnp.transpose(jnp.broadcast_to(p_end[:, ls], (PW, PW)))))

    for c in chains:
        x = _dot_nt(c["ar"], c["bk"])
        c["a_ab"] = jnp.where(c["strict"], x[:L, :L2], 0.0)
        c["a_ak"] = jnp.where(c["strict"], x[:L, L2:], 0.0).astype(BF16)
        c["a_rb"] = jnp.where(c["incl"], x[L:, :L2], 0.0).astype(BF16)
        c["a_rk"] = jnp.where(c["incl"], x[L:, L2:], 0.0).astype(BF16)
        c["inv"] = eye2 + c["a_ab"]
        c["pw"] = c["a_ab"].astype(BF16)
    n_sq = int(math.log2(L)) - 1
    for k in range(n_sq + 1):
        for c in chains:
            pws = stack(c["pw"])
            if k == 0:
                c["pw"] = _dot(c["pw"], pws).astype(BF16)
            elif k < n_sq:
                both = _dot(jnp.concatenate([c["pw"], c["inv"].astype(BF16)], axis=0), pws)
                c["pw"] = both[:L, :].astype(BF16)
                c["inv"] = c["inv"] + both[L:, :]
            else:
                c["inv"] = c["inv"] + _dot(c["inv"].astype(BF16), pws)
    for c in chains:
        c["akv"] = stack(_dot(c["a_ak"], c["vv"]).astype(BF16))
    for c in chains:
        inv = c["inv"].astype(BF16)
        c["w_t"] = _dot(inv, c["aa"]).astype(BF16)
        c["u_t"] = _dot(inv, c["akv"])
    state = {(d, p): h_ref[d, p] for d in range(2) for p in range(n_pairs)}
    for step in range(n_sub):
        now = [c for c in chains if c["step"] == step]
        for c in now:
            c["h"] = state[c["d"], c["p"]]
            c["hb"] = c["h"].astype(BF16)
            c["uu"] = stack((_dot(c["w_t"], c["hb"]) + c["u_t"]).astype(BF16))
        for c in now:
            state[c["d"], c["p"]] = c["pe_col"] * c["h"] + _dot(c["bpT"], c["uu"]) + _dot(c["kpT"], c["vv"])
        for c in now:
            c["y_ref"][c["rows"], c["ls"]] = (_dot(c["ar"][L:, :], c["hb"]) + _dot(c["a_rb"], c["uu"])
                                              + _dot(c["a_rk"], c["vv"]))
    for (d, p), h in state.items():
        h_ref[d, p] = h


def _scan(r, v, kk, lw, kd, b, bsz, s_len):
    rows = CHUNK * SCAN_CHUNKS_PER_STEP
    nc = s_len // rows
    shp = lambda a: a.reshape(bsz, s_len, RW_C)
    shp2 = lambda a: a.reshape(2, bsz, s_len, RW_C)
    f_sh = pl.BlockSpec((None, rows, RW_C), lambda bi, c: (bi, c, 0))
    r_sh = pl.BlockSpec((None, rows, RW_C), lambda bi, c: (bi, nc - 1 - c, 0))
    f_pd = pl.BlockSpec((None, None, rows, RW_C), lambda bi, c: (0, bi, c, 0))
    r_pd = pl.BlockSpec((None, None, rows, RW_C), lambda bi, c: (1, bi, nc - 1 - c, 0))
    out = jax.ShapeDtypeStruct((bsz, s_len, RW_C), F32)
    r3, v3, kk3, lw4, kd4, b4 = shp(r), shp(v), shp(kk), shp2(lw), shp2(kd), shp2(b)
    yf, yr = pl.pallas_call(
        functools.partial(_scan_kernel, chunk=CHUNK),
        grid=(bsz, nc),
        in_specs=[f_sh, f_sh, f_sh, f_pd, f_pd, f_pd, r_sh, r_sh, r_sh, r_pd, r_pd, r_pd],
        out_specs=[f_sh, r_sh],
        out_shape=[out, out],
        scratch_shapes=[pltpu.VMEM((2, RW_C // (2 * RW_N), 2 * RW_N, 2 * RW_N), F32)],
        compiler_params=_params(("parallel", "arbitrary"), 32),
        name="scan",
    )(r3, v3, kk3, lw4, kd4, b4, r3, v3, kk3, lw4, kd4, b4)
    return yf.reshape(bsz * s_len, RW_C), yr.reshape(bsz * s_len, RW_C)


def _daprep_kernel(x_ref, g_ref, w_ref, cf_ref, sn_ref, sp_ref, qn_ref, kn_ref, bd_ref, q_o, k_o, v_o):
    bd = bd_ref[...]
    cf, sn, sp = cf_ref[...], sn_ref[...], sp_ref[...]
    p = _dot(_rms(x_ref[...], g_ref[...]).astype(BF16), w_ref[...])

    def norm_rope(x, g):
        ms = _group_sum(x * x, bd) * (1.0 / DA_DH)
        xn = x * lax.rsqrt(ms + NORM_EPS) * g
        half = ROT_DIM // 2
        return xn * cf + pltpu.roll(xn, DA_C - half, 1) * sn + pltpu.roll(xn, half, 1) * sp

    q = norm_rope(p[:, 0:DA_C], qn_ref[...]) * (DA_DH ** -0.5 * LOG2E)
    k = norm_rope(p[:, DA_C:2 * DA_C], kn_ref[...])
    q_o[...] = q.astype(BF16)
    k_o[...] = k.astype(BF16)
    vt = jnp.transpose(p[:, 2 * DA_C:3 * DA_C]).astype(BF16)
    hw = 2 * DA_DH
    for h in range(DA_HEADS):
        v_o[h, 0:hw, :] = vt[h * hw:(h + 1) * hw, :]
        v_o[h, hw:hw + ONES_ROWS, :] = jnp.ones((ONES_ROWS, vt.shape[1]), BF16)


def _rope_lane_tables(s_len):
    half = ROT_DIM // 2
    inv_freq = ROPE_THETA ** (-jnp.arange(0, ROT_DIM, 2, dtype=F32) / ROT_DIM)
    ang = jnp.arange(s_len, dtype=F32)[:, None] * inv_freq[None, :]
    cos, sin = jnp.cos(ang), jnp.sin(ang)
    j = jnp.arange(DA_C) % DA_DH
    f = j % half
    cf = jnp.where(j < ROT_DIM, cos[:, f], 1.0)
    sn = jnp.where(j < half, -sin[:, f], 0.0)
    sp = jnp.where((j >= half) & (j < ROT_DIM), sin[:, f], 0.0)
    return cf, sn, sp


def _daprep(x1, bsz, s_len, norm_g, w_da, q_norm, k_norm, bd):
    t = x1.shape[0]
    tm = min(ATT_TK, s_len)
    per_seq = s_len // tm
    cf, sn, sp = _rope_lane_tables(s_len)
    const2 = lambda i: (0, 0)
    tok = pl.BlockSpec((tm, DA_C), lambda i: (i, 0))
    tab = pl.BlockSpec((tm, DA_C), lambda i: (i % per_seq, 0))
    vec = pl.BlockSpec((1, DA_C), const2)
    out = jax.ShapeDtypeStruct((t, DA_C), BF16)
    reps = DA_C // DA_DH
    return pl.pallas_call(
        _daprep_kernel,
        grid=(t // tm,),
        in_specs=[pl.BlockSpec((tm, D_MODEL), lambda i: (i, 0)), pl.BlockSpec((1, D_MODEL), const2),
                  pl.BlockSpec((D_MODEL, 3 * DA_C), const2), tab, tab, tab, vec, vec,
                  pl.BlockSpec((DA_C, DA_C), const2)],
        out_specs=[tok, tok, pl.BlockSpec((None, None, DA_HEADS, VT_ROWS, tm),
                                          lambda i: (i // per_seq, i % per_seq, 0, 0, 0))],
        out_shape=[out, out, jax.ShapeDtypeStruct((bsz, per_seq, DA_HEADS, VT_ROWS, tm), BF16)],
        compiler_params=_params(("parallel",), 32),
        name="daprep",
    )(x1, norm_g.reshape(1, D_MODEL), w_da.astype(BF16), cf, sn, sp,
      jnp.tile(q_norm, reps).reshape(1, DA_C), jnp.tile(k_norm, reps).reshape(1, DA_C), bd)


def _attn_kernel(q_ref, k_ref, vt_ref, lq1_ref, lk1_ref, lq2_ref, lk2_ref, sub_ref, o_ref,
                 acc_ref, sa_ref, sb_ref, pa_ref, pb_ref, *, tq, tk):
    q = q_ref[...]
    lane = lax.broadcasted_iota(jnp.int32, q.shape, 1)
    zero = jnp.zeros_like(q)
    qs = jnp.concatenate([jnp.where(lane < DA_DH, q, zero), jnp.where(lane >= DA_DH, q, zero)], axis=0)
    nk = k_ref.shape[0] // tk

    def scores(i):
        return _dot_nt(k_ref[i * tk:(i + 1) * tk, :], qs)

    def scores_into(s_ref, i):
        s = scores(i)
        s_ref[...] = s
        return jnp.max(s, axis=0, keepdims=True)

    def softmax(s_ref, p_ref, m_prev, m_blk):
        m_new = jnp.maximum(m_prev, m_blk)
        p_ref[...] = jnp.exp2((s_ref[...] - m_new).astype(BF16))
        return m_new, jnp.exp2(m_prev - m_new)

    def accumulate(p_ref, i, corr):
        pv = _dot(vt_ref[i], p_ref[...])
        acc_ref[...] = pv if i == 0 else corr * acc_ref[...] + pv

    s_refs = (sa_ref, sb_ref)
    p_refs = (pa_ref, pb_ref)
    m_blk = scores_into(s_refs[0], 0)
    m = jnp.full((1, 2 * tq), -jnp.inf, F32)
    corr = None
    for i in range(nk):
        m_next = scores_into(s_refs[(i + 1) % 2], i + 1) if i + 1 < nk else None
        if i > 0:
            accumulate(p_refs[(i - 1) % 2], i - 1, corr)
        m, corr = softmax(s_refs[i % 2], p_refs[i % 2], m, m_blk)
        m_blk = m_next
    accumulate(p_refs[(nk - 1) % 2], nk - 1, corr)

    hw = 2 * DA_DH
    o2 = acc_ref[0:hw, :] / acc_ref[hw:hw + 1, :]
    lam = (jnp.exp(jnp.sum(lq1_ref[...] * lk1_ref[...], axis=1, keepdims=True))
           - jnp.exp(jnp.sum(lq2_ref[...] * lk2_ref[...], axis=1, keepdims=True))) + LAM_INIT
    o = jnp.transpose(o2[:, 0:tq] - lam * o2[:, tq:2 * tq])
    o_ref[...] = _rms(o, sub_ref[...]) * (1.0 - LAM_INIT)


def _attn(q, k, vt, bsz, s_len, lq1, lk1, lq2, lk2, subln):
    tq = min(ATT_TQ, s_len)
    tk = min(ATT_TK, s_len)
    nk = s_len // tk
    hw = 2 * DA_DH
    shp = lambda a: a.reshape(bsz, s_len, DA_C)
    lvec = pl.BlockSpec((1, DA_DH), lambda b, h, qi: (0, 0))
    o = pl.pallas_call(
        functools.partial(_attn_kernel, tq=tq, tk=tk),
        grid=(bsz, DA_HEADS, s_len // tq),
        in_specs=[
            pl.BlockSpec((None, tq, hw), lambda b, h, qi: (b, qi, h)),
            pl.BlockSpec((None, s_len, hw), lambda b, h, qi: (b, 0, h)),
            pl.BlockSpec((None, nk, None, VT_ROWS, tk), lambda b, h, qi: (b, 0, h, 0, 0)),
            lvec, lvec, lvec, lvec,
            pl.BlockSpec((1, hw), lambda b, h, qi: (0, 0)),
        ],
        out_specs=pl.BlockSpec((None, tq, hw), lambda b, h, qi: (b, qi, h)),
        out_shape=jax.ShapeDtypeStruct((bsz, s_len, DA_C), F32),
        scratch_shapes=[pltpu.VMEM((VT_ROWS, 2 * tq), F32),
                        pltpu.VMEM((tk, 2 * tq), F32), pltpu.VMEM((tk, 2 * tq), F32),
                        pltpu.VMEM((tk, 2 * tq), BF16), pltpu.VMEM((tk, 2 * tq), BF16)],
        compiler_params=_params(("parallel", "parallel", "arbitrary"), 32),
        name="attn",
    )(shp(q), shp(k), vt, lq1.reshape(1, DA_DH), lk1.reshape(1, DA_DH), lq2.reshape(1, DA_DH),
      lk2.reshape(1, DA_DH), subln.reshape(1, hw))
    return o.reshape(bsz * s_len, DA_C)


def _mixout_kernel(x_ref, yf_ref, yr_ref, g_ref, bonus_ref, da_ref, lng_ref, lnb_ref, bd_ref, wrw_ref, wda_ref, o_ref):
    bd = bd_ref[...]
    y = yf_ref[...] + yr_ref[...]
    mu = _group_sum(y, bd) * (1.0 / RW_N)
    yc = y - mu
    var = _group_sum(yc * yc, bd) * (1.0 / RW_N)
    yn = yc * lax.rsqrt(var + GN_EPS) * lng_ref[...] + lnb_ref[...]
    y_rw = ((yn + bonus_ref[...]) * g_ref[...]).astype(BF16)
    mixed = _dot(y_rw, wrw_ref[...]) + _dot(da_ref[...].astype(BF16), wda_ref[...])
    o_ref[...] = x_ref[...] + mixed


def _mixout(x1, yf, yr, g, bonus, o_da, ln_g, ln_b, bd, w_out):
    t = x1.shape[0]
    tm = min(TOKEN_TILE, t)
    const2 = lambda i: (0, 0)
    tok = pl.BlockSpec((tm, RW_C), lambda i: (i, 0))
    vec = pl.BlockSpec((1, RW_C), const2)
    return pl.pallas_call(
        _mixout_kernel,
        grid=(t // tm,),
        in_specs=[
            pl.BlockSpec((tm, D_MODEL), lambda i: (i, 0)),
            tok, tok, tok, tok, tok, vec, vec,
            pl.BlockSpec((RW_C, RW_C), const2),
            pl.BlockSpec((RW_C, D_MODEL), const2),
            pl.BlockSpec((DA_C, D_MODEL), const2),
        ],
        out_specs=pl.BlockSpec((tm, D_MODEL), lambda i: (i, 0)),
        out_shape=jax.ShapeDtypeStruct((t, D_MODEL), F32),
        compiler_params=_params(("parallel",), 48),
        name="mixout",
    )(x1, yf, yr, g, bonus, o_da, ln_g.reshape(1, RW_C), ln_b.reshape(1, RW_C), bd,
      w_out[:RW_C].astype(BF16), w_out[RW_C:].astype(BF16))


def _encoder_layer(x, p):
    bsz, s_len, _ = x.shape
    t = bsz * s_len
    xf = x.reshape(t, D_MODEL)
    lane = jnp.arange(RW_C) // RW_N
    bd = (lane[:, None] == lane[None, :]).astype(BF16)

    x1 = _ffn(xf, p["ffn1_norm"], p["ffn1_w_gu"], p["ffn1_w_down"])
    r, v, kk, g, bonus, lw, kd, b = _rwprep(
        x1, s_len, p["mix_norm"], p["w_in"][:, :RW_COLS], p["conv_w"], p["rw_w0"], p["rw_w_up"], p["rw_a0"],
        p["rw_a_up"], p["rw_g_up"], p["rw_k_k"], p["rw_k_a"], p["rw_r_k"], bd)
    yf, yr = _scan(r, v, kk, lw, kd, b, bsz, s_len)
    q, k, va = _daprep(x1, bsz, s_len, p["mix_norm"], p["w_in"][:, RW_COLS:], p["da_q_norm"], p["da_k_norm"], bd)
    o_da = _attn(q, k, va, bsz, s_len, p["da_lq1"], p["da_lk1"], p["da_lq2"], p["da_lk2"], p["da_subln"])
    x2 = _mixout(x1, yf, yr, g, bonus, o_da, p["rw_ln_g"], p["rw_ln_b"], bd, p["w_out"])
    out = _ffn(x2, p["ffn2_norm"], p["ffn2_w_gu"], p["ffn2_w_down"], p["final_norm"])
    return out.reshape(bsz, s_len, D_MODEL)


def kernel(x_prompt, x_sample, ffn1_norm, ffn1_w_gu, ffn1_w_down, mix_norm, w_in, conv_w, rw_w0, rw_w_up, rw_a0, rw_a_up, rw_g_up, rw_k_k, rw_k_a, rw_r_k, rw_ln_g, rw_ln_b, da_q_norm, da_k_norm, da_lq1, da_lk1, da_lq2, da_lk2, da_subln, w_out, ffn2_norm, ffn2_w_gu, ffn2_w_down, final_norm):
    names = ("ffn1_norm", "ffn1_w_gu", "ffn1_w_down", "mix_norm", "w_in", "conv_w", "rw_w0", "rw_w_up", "rw_a0",
             "rw_a_up", "rw_g_up", "rw_k_k", "rw_k_a", "rw_r_k", "rw_ln_g", "rw_ln_b", "da_q_norm", "da_k_norm",
             "da_lq1", "da_lk1", "da_lq2", "da_lk2", "da_subln", "w_out", "ffn2_norm", "ffn2_w_gu", "ffn2_w_down",
             "final_norm")
    vals = (ffn1_norm, ffn1_w_gu, ffn1_w_down, mix_norm, w_in, conv_w, rw_w0, rw_w_up, rw_a0, rw_a_up, rw_g_up,
            rw_k_k, rw_k_a, rw_r_k, rw_ln_g, rw_ln_b, da_q_norm, da_k_norm, da_lq1, da_lk1, da_lq2, da_lk2,
            da_subln, w_out, ffn2_norm, ffn2_w_gu, ffn2_w_down, final_norm)
    assert all(a.shape[0] == 1 for a in vals), "single-layer (depth 1) parameter stacks expected"
    p = {n: a[0] for n, a in zip(names, vals)}
    return (_encoder_layer(x_prompt, p), _encoder_layer(x_sample, p))
```

```python
import functools
import math

import jax
import jax.numpy as jnp
from jax import lax
from jax.experimental import pallas as pl
from jax.experimental.pallas import tpu as pltpu

F32 = jnp.float32
BF16 = jnp.bfloat16

D_MODEL = 1024
D_FF = 2816
RW_HEADS = 8
RW_N = 64
RW_C = RW_HEADS * RW_N
LORA_W = 64
LORA_A = 64
LORA_G = 128
DECAY_SCALE = 0.606531
GN_EPS = 64e-5
DA_HEADS = 4
DA_DH = 64
DA_C = DA_HEADS * 2 * DA_DH
ROT_DIM = DA_DH // 4
ROPE_THETA = 500000.0
NORM_EPS = 1e-6
RW_COLS = 3 * RW_C + 2 * LORA_W + 2 * LORA_A + LORA_G
LAM_INIT = 0.8 - 0.6 * math.exp(-0.3 * 0)
LOG2E = math.log2(math.e)

V7X_VMEM_BYTES = 64 * 1024 * 1024
SUBLANES = 8

FF_CHUNK = 256
N_FF_CHUNKS = D_FF // FF_CHUNK
TOKEN_TILE = 512
PREP_TILE = 256
CHUNK = 64
SCAN_CHUNKS_PER_STEP = 4
ATT_TQ = 1024
ONES_ROWS = 16
VT_ROWS = 2 * DA_DH + ONES_ROWS
ATT_TK = 512


def _params(sem, vmem_mb):
    return pltpu.CompilerParams(dimension_semantics=sem, vmem_limit_bytes=vmem_mb * 1024 * 1024)


def _dot(a, b):
    return jnp.dot(a, b, preferred_element_type=F32)


def _dot_nt(a, b):
    return lax.dot_general(a, b, (((1,), (1,)), ((), ())), preferred_element_type=F32)


def _dot_tn(a, b):
    return lax.dot_general(a, b, (((0,), (0,)), ((), ())), preferred_element_type=F32)


def _split2(x):
    hi = x.astype(BF16)
    lo = (x - hi.astype(F32)).astype(BF16)
    return hi, lo


def _split3(x):
    hi = x.astype(BF16)
    r1 = x - hi.astype(F32)
    mid = r1.astype(BF16)
    lo = (r1 - mid.astype(F32)).astype(BF16)
    return hi, mid, lo


def _group_sum(x, bd):
    hi, lo = _split2(x)
    return _dot(hi, bd) + _dot(lo, bd)


def _rms(x, g, eps=NORM_EPS):
    return x * lax.rsqrt(jnp.mean(x * x, axis=-1, keepdims=True) + eps) * g


def _half_swiglu(x, g_ref, wgu_ref, wd_ref):
    h = _rms(x, g_ref[...]).astype(BF16)
    acc = jnp.zeros_like(x)
    for j in range(N_FF_CHUNKS):
        lo = j * FF_CHUNK
        gate = _dot(h, wgu_ref[:, lo:lo + FF_CHUNK])
        up = _dot(h, wgu_ref[:, D_FF + lo:D_FF + lo + FF_CHUNK])
        act = (gate * jax.nn.sigmoid(gate) * up).astype(BF16)
        acc = acc + _dot(act, wd_ref[lo:lo + FF_CHUNK, :])
    return x + 0.5 * acc


def _ffn_kernel(x_ref, g_ref, wgu_ref, wd_ref, o_ref):
    o_ref[...] = _half_swiglu(x_ref[...], g_ref, wgu_ref, wd_ref)


def _resident(shape):
    return pl.BlockSpec(shape, lambda i: (0,) * len(shape), pipeline_mode=pl.Buffered(1))


def _ffn(x, norm_g, w_gu, w_down):
    t = x.shape[0]
    tm = min(TOKEN_TILE, t)
    return pl.pallas_call(
        _ffn_kernel,
        grid=(t // tm,),
        in_specs=[
            pl.BlockSpec((tm, D_MODEL), lambda i: (i, 0)),
            _resident((1, D_MODEL)),
            _resident((D_MODEL, 2 * D_FF)),
            _resident((D_FF, D_MODEL)),
        ],
        out_specs=pl.BlockSpec((tm, D_MODEL), lambda i: (i, 0)),
        out_shape=jax.ShapeDtypeStruct((t, D_MODEL), F32),
        compiler_params=_params(("parallel",), 48),
        name="ffn",
    )(x, norm_g.reshape(1, D_MODEL), w_gu.astype(BF16), w_down.astype(BF16))


def _rwprep_kernel(x_ref, xp_ref, xn_ref, g_ref, w_ref, conv_ref, w0_ref, wup_ref, a0_ref, aup_ref, gup_ref,
                   kk_ref, ka_ref, rk_ref, bd_ref,
                   r_o, v_o, kk_o, g_o, bonus_o, lw_o, kd_o, b_o, *, tm, s_len):
    i = pl.program_id(0)
    xa = jnp.concatenate([x_ref[...], xp_ref[...], xn_ref[...]], axis=0)
    ua = _dot(_rms(xa, g_ref[...]).astype(BF16), w_ref[...])
    u = ua[0:tm, :]
    start = i * tm
    first = (start % s_len) == 0
    last = ((start + tm) % s_len) == 0
    prev_row = jnp.where(first, 0.0, ua[tm + SUBLANES - 1:tm + SUBLANES, :])
    next_row = jnp.where(last, 0.0, ua[tm + SUBLANES:tm + SUBLANES + 1, :])
    row = lax.broadcasted_iota(jnp.int32, (tm, 1), 0)
    prev = jnp.where(row == 0, prev_row, pltpu.roll(u, 1, 0))
    nxt = jnp.where(row == tm - 1, next_row, pltpu.roll(u, tm - 1, 0))
    c = prev * conv_ref[0:1, :] + u * conv_ref[1:2, :] + nxt * conv_ref[2:3, :]

    o_w = 3 * RW_C
    o_a = o_w + 2 * LORA_W
    o_g = o_a + 2 * LORA_A
    r = c[:, 0:RW_C]
    k = c[:, RW_C:2 * RW_C]
    v = c[:, 2 * RW_C:3 * RW_C]
    tw = jnp.tanh(c[:, o_w:o_a]).astype(BF16)
    ad = c[:, o_a:o_g].astype(BF16)
    gd = c[:, o_g:RW_COLS]
    bd = bd_ref[...]

    kkv = k * kk_ref[...]
    norm = jnp.sqrt(_group_sum(kkv * kkv, bd))
    kk = kkv / jnp.maximum(norm, 1e-12)
    g = _dot(jax.nn.sigmoid(gd).astype(BF16), gup_ref[...])

    ksum = jnp.zeros_like(k)
    for d in range(2):
        lw = -DECAY_SCALE * jax.nn.sigmoid(w0_ref[d:d + 1, :] + _dot(tw, wup_ref[d]))
        a = jax.nn.sigmoid(a0_ref[d:d + 1, :] + _dot(ad, aup_ref[d]))
        kd = k * (1.0 + (a - 1.0) * ka_ref[...])
        lw_o[d] = lw
        kd_o[d] = kd
        b_o[d] = kk * a
        ksum = ksum + kd
    coef = _group_sum(r * ksum * rk_ref[...], bd)
    r_o[...] = r
    v_o[...] = v
    kk_o[...] = kk
    g_o[...] = g
    bonus_o[...] = coef * v


def _pad_dir_rows(w_up):
    r = w_up.shape[1]
    z = jnp.zeros_like(w_up[0])
    return jnp.stack([jnp.concatenate([w_up[0], z], 0), jnp.concatenate([z, w_up[1]], 0)]).astype(BF16)


def _rwprep(x1, s_len, norm_g, w_rw, conv_w, w0, w_up, a0, a_up, g_up, k_k, k_a, r_k, bd):
    t = x1.shape[0]
    tm = min(PREP_TILE, s_len)
    nb8 = t // SUBLANES
    per = tm // SUBLANES
    const2 = lambda i: (0, 0)
    const3 = lambda i: (0, 0, 0)
    tok = pl.BlockSpec((tm, RW_C), lambda i: (i, 0))
    tok2 = pl.BlockSpec((2, tm, RW_C), lambda i: (0, i, 0))
    vec = pl.BlockSpec((1, RW_C), const2)
    one = jax.ShapeDtypeStruct((t, RW_C), F32)
    two = jax.ShapeDtypeStruct((2, t, RW_C), F32)
    return pl.pallas_call(
        functools.partial(_rwprep_kernel, tm=tm, s_len=s_len),
        grid=(t // tm,),
        in_specs=[
            pl.BlockSpec((tm, D_MODEL), lambda i: (i, 0)),
            pl.BlockSpec((SUBLANES, D_MODEL), lambda i: (jnp.maximum(i * per - 1, 0), 0)),
            pl.BlockSpec((SUBLANES, D_MODEL), lambda i: (jnp.minimum((i + 1) * per, nb8 - 1), 0)),
            pl.BlockSpec((1, D_MODEL), const2),
            pl.BlockSpec((D_MODEL, RW_COLS), const2),
            pl.BlockSpec((3, RW_COLS), const2),
            pl.BlockSpec((2, RW_C), const2),
            pl.BlockSpec((2, 2 * LORA_W, RW_C), const3),
            pl.BlockSpec((2, RW_C), const2),
            pl.BlockSpec((2, 2 * LORA_A, RW_C), const3),
            pl.BlockSpec((LORA_G, RW_C), const2),
            vec, vec, vec,
            pl.BlockSpec((RW_C, RW_C), const2),
        ],
        out_specs=[tok, tok, tok, tok, tok, tok2, tok2, tok2],
        out_shape=[one, one, one, one, one, two, two, two],
        compiler_params=_params(("parallel",), 48),
        name="rwprep",
    )(x1, x1, x1, norm_g.reshape(1, D_MODEL), w_rw.astype(BF16), conv_w, w0, _pad_dir_rows(w_up), a0, _pad_dir_rows(a_up),
      g_up.astype(BF16), k_k.reshape(1, RW_C), k_a.reshape(1, RW_C), r_k.reshape(1, RW_C), bd)


def _scan_kernel(rf_ref, vf_ref, kkf_ref, lwf_ref, kdf_ref, bf_ref, rr_ref, vr_ref, kkr_ref, lwr_ref, kdr_ref, br_ref,
                 yf_ref, yr_ref, h_ref, *, chunk):
    L = chunk
    L2 = 2 * L
    PW = 2 * RW_N
    n_pairs = RW_C // PW
    n_sub = lwf_ref.shape[0] // L
    assert L2 == PW, "interaction matrices and lane pairs share the 128-lane tile"

    @pl.when(pl.program_id(1) == 0)
    def _():
        h_ref[...] = jnp.zeros_like(h_ref)

    row = lax.broadcasted_iota(jnp.int32, (L, L), 0)
    col = lax.broadcasted_iota(jnp.int32, (L, L), 1)
    row2 = lax.broadcasted_iota(jnp.int32, (L, L2), 0)
    col2 = lax.broadcasted_iota(jnp.int32, (L, L2), 1) & (L - 1)
    eye2 = (row2 == col2).astype(F32)
    first_head = lax.broadcasted_iota(jnp.int32, (L, PW), 1) < RW_N

    def stack(z):
        zero = jnp.zeros_like(z)
        return jnp.concatenate([jnp.where(first_head, z, zero), jnp.where(first_head, zero, z)], axis=0)

    chains = []
    dirs = ((rf_ref, vf_ref, kkf_ref, lwf_ref, kdf_ref, bf_ref, yf_ref, 1),
            (rr_ref, vr_ref, kkr_ref, lwr_ref, kdr_ref, br_ref, yr_ref, -1))
    for d, (r_ref, v_ref, kk_ref, lw_ref, kd_ref, b_ref, y_ref, sgn) in enumerate(dirs):
        incl2 = (row2 - col2) * sgn >= 0
        strict2 = (row2 - col2) * sgn > 0
        tri = ((row - col) * sgn >= 0).astype(BF16)
        for step, j in enumerate(range(n_sub) if sgn > 0 else reversed(range(n_sub))):
            rows = slice(j * L, (j + 1) * L)
            lw = lw_ref[rows, :]
            hi, mid, lo = _split3(lw)
            c_in = _dot(tri, hi) + _dot(tri, mid) + _dot(tri, lo)
            c_tot = jnp.sum(lw, axis=0, keepdims=True)
            e_neg = jnp.exp(-c_in)
            e_end = jnp.exp(c_tot - c_in)
            kk = kk_ref[rows, :]
            b = b_ref[rows, :]
            kd = kd_ref[rows, :]
            a_t = -kk * jnp.exp(c_in - lw)
            r_t = r_ref[rows, :] * jnp.exp(c_in)
            b_t = b * e_neg
            k_t = kd * e_neg
            b_p = b * e_end
            k_p = kd * e_end
            p_end = jnp.exp(c_tot)
            v = v_ref[rows, :]
            for p in range(n_pairs):
                ls = slice(p * PW, (p + 1) * PW)
                a_n = a_t[:, ls].astype(BF16)
                chains.append(dict(
                    d=d, p=p, step=step, rows=rows, ls=ls, y_ref=y_ref, incl=incl2, strict=strict2,
                    ar=jnp.concatenate([a_n, r_t[:, ls].astype(BF16)], axis=0),
                    bk=jnp.concatenate([stack(b_t[:, ls].astype(BF16)), stack(k_t[:, ls].astype(BF16))], axis=0),
                    aa=stack(a_n), vv=stack(v[:, ls].astype(BF16)),
                    bpT=jnp.transpose(stack(b_p[:, ls])).astype(BF16),
                    kpT=jnp.transpose(stack(k_p[:, ls])).astype(BF16),
                    pe_col=jnp.transpose(jnp.broadcast_to(p_end[:, ls], (PW, PW)))))

    for c in chains:
        x = _dot_nt(c["ar"], c["bk"])
        c["a_ab"] = jnp.where(c["strict"], x[:L, :L2], 0.0)
        c["a_ak"] = jnp.where(c["strict"], x[:L, L2:], 0.0).astype(BF16)
        c["a_rb"] = jnp.where(c["incl"], x[L:, :L2], 0.0).astype(BF16)
        c["a_rk"] = jnp.where(c["incl"], x[L:, L2:], 0.0).astype(BF16)
        c["inv"] = eye2 + c["a_ab"]
        c["pw"] = c["a_ab"].astype(BF16)
    n_sq = int(math.log2(L)) - 1
    for k in range(n_sq + 1):
        for c in chains:
            pws = stack(c["pw"])
            if k == 0:
                c["pw"] = _dot(c["pw"], pws).astype(BF16)
            elif k < n_sq:
                both = _dot(jnp.concatenate([c["pw"], c["inv"].astype(BF16)], axis=0), pws)
                c["pw"] = both[:L, :].astype(BF16)
                c["inv"] = c["inv"] + both[L:, :]
            else:
                c["inv"] = c["inv"] + _dot(c["inv"].astype(BF16), pws)
    for c in chains:
        c["akv"] = stack(_dot(c["a_ak"], c["vv"]).astype(BF16))
    for c in chains:
        inv = c["inv"].astype(BF16)
        c["w_t"] = _dot(inv, c["aa"]).astype(BF16)
        c["u_t"] = _dot(inv, c["akv"])
    state = {(d, p): h_ref[d, p] for d in range(2) for p in range(n_pairs)}
    for step in range(n_sub):
        now = [c for c in chains if c["step"] == step]
        for c in now:
            c["h"] = state[c["d"], c["p"]]
            c["hb"] = c["h"].astype(BF16)
            c["uu"] = stack((_dot(c["w_t"], c["hb"]) + c["u_t"]).astype(BF16))
        for c in now:
            state[c["d"], c["p"]] = c["pe_col"] * c["h"] + _dot(c["bpT"], c["uu"]) + _dot(c["kpT"], c["vv"])
        for c in now:
            c["y_ref"][c["rows"], c["ls"]] = (_dot(c["ar"][L:, :], c["hb"]) + _dot(c["a_rb"], c["uu"])
                                              + _dot(c["a_rk"], c["vv"]))
    for (d, p), h in state.items():
        h_ref[d, p] = h


def _scan(r, v, kk, lw, kd, b, bsz, s_len):
    rows = CHUNK * SCAN_CHUNKS_PER_STEP
    nc = s_len // rows
    shp = lambda a: a.reshape(bsz, s_len, RW_C)
    shp2 = lambda a: a.reshape(2, bsz, s_len, RW_C)
    f_sh = pl.BlockSpec((None, rows, RW_C), lambda bi, c: (bi, c, 0))
    r_sh = pl.BlockSpec((None, rows, RW_C), lambda bi, c: (bi, nc - 1 - c, 0))
    f_pd = pl.BlockSpec((None, None, rows, RW_C), lambda bi, c: (0, bi, c, 0))
    r_pd = pl.BlockSpec((None, None, rows, RW_C), lambda bi, c: (1, bi, nc - 1 - c, 0))
    out = jax.ShapeDtypeStruct((bsz, s_len, RW_C), F32)
    r3, v3, kk3, lw4, kd4, b4 = shp(r), shp(v), shp(kk), shp2(lw), shp2(kd), shp2(b)
    yf, yr = pl.pallas_call(
        functools.partial(_scan_kernel, chunk=CHUNK),
        grid=(bsz, nc),
        in_specs=[f_sh, f_sh, f_sh, f_pd, f_pd, f_pd, r_sh, r_sh, r_sh, r_pd, r_pd, r_pd],
        out_specs=[f_sh, r_sh],
        out_shape=[out, out],
        scratch_shapes=[pltpu.VMEM((2, RW_C // (2 * RW_N), 2 * RW_N, 2 * RW_N), F32)],
        compiler_params=_params(("parallel", "arbitrary"), 32),
        name="scan",
    )(r3, v3, kk3, lw4, kd4, b4, r3, v3, kk3, lw4, kd4, b4)
    return yf.reshape(bsz * s_len, RW_C), yr.reshape(bsz * s_len, RW_C)


def _daprep_kernel(x_ref, g_ref, w_ref, cf_ref, sn_ref, sp_ref, qn_ref, kn_ref, bd_ref, q_o, k_o, v_o):
    bd = bd_ref[...]
    cf, sn, sp = cf_ref[...], sn_ref[...], sp_ref[...]
    p = _dot(_rms(x_ref[...], g_ref[...]).astype(BF16), w_ref[...])

    def norm_rope(x, g):
        ms = _group_sum(x * x, bd) * (1.0 / DA_DH)
        xn = x * lax.rsqrt(ms + NORM_EPS) * g
        half = ROT_DIM // 2
        return xn * cf + pltpu.roll(xn, DA_C - half, 1) * sn + pltpu.roll(xn, half, 1) * sp

    q = norm_rope(p[:, 0:DA_C], qn_ref[...]) * (DA_DH ** -0.5 * LOG2E)
    k = norm_rope(p[:, DA_C:2 * DA_C], kn_ref[...])
    q_o[...] = q.astype(BF16)
    k_o[...] = k.astype(BF16)
    vt = jnp.transpose(p[:, 2 * DA_C:3 * DA_C]).astype(BF16)
    hw = 2 * DA_DH
    for h in range(DA_HEADS):
        v_o[h, 0:hw, :] = vt[h * hw:(h + 1) * hw, :]
        v_o[h, hw:hw + ONES_ROWS, :] = jnp.ones((ONES_ROWS, vt.shape[1]), BF16)


def _rope_lane_tables(s_len):
    half = ROT_DIM // 2
    inv_freq = ROPE_THETA ** (-jnp.arange(0, ROT_DIM, 2, dtype=F32) / ROT_DIM)
    ang = jnp.arange(s_len, dtype=F32)[:, None] * inv_freq[None, :]
    cos, sin = jnp.cos(ang), jnp.sin(ang)
    j = jnp.arange(DA_C) % DA_DH
    f = j % half
    cf = jnp.where(j < ROT_DIM, cos[:, f], 1.0)
    sn = jnp.where(j < half, -sin[:, f], 0.0)
    sp = jnp.where((j >= half) & (j < ROT_DIM), sin[:, f], 0.0)
    return cf, sn, sp


def _daprep(x1, bsz, s_len, norm_g, w_da, q_norm, k_norm, bd):
    t = x1.shape[0]
    tm = min(ATT_TK, s_len)
    per_seq = s_len // tm
    cf, sn, sp = _rope_lane_tables(s_len)
    const2 = lambda i: (0, 0)
    tok = pl.BlockSpec((tm, DA_C), lambda i: (i, 0))
    tab = pl.BlockSpec((tm, DA_C), lambda i: (i % per_seq, 0))
    vec = pl.BlockSpec((1, DA_C), const2)
    out = jax.ShapeDtypeStruct((t, DA_C), BF16)
    reps = DA_C // DA_DH
    return pl.pallas_call(
        _daprep_kernel,
        grid=(t // tm,),
        in_specs=[pl.BlockSpec((tm, D_MODEL), lambda i: (i, 0)), pl.BlockSpec((1, D_MODEL), const2),
                  pl.BlockSpec((D_MODEL, 3 * DA_C), const2), tab, tab, tab, vec, vec,
                  pl.BlockSpec((DA_C, DA_C), const2)],
        out_specs=[tok, tok, pl.BlockSpec((None, None, DA_HEADS, VT_ROWS, tm),
                                          lambda i: (i // per_seq, i % per_seq, 0, 0, 0))],
        out_shape=[out, out, jax.ShapeDtypeStruct((bsz, per_seq, DA_HEADS, VT_ROWS, tm), BF16)],
        compiler_params=_params(("parallel",), 32),
        name="daprep",
    )(x1, norm_g.reshape(1, D_MODEL), w_da.astype(BF16), cf, sn, sp,
      jnp.tile(q_norm, reps).reshape(1, DA_C), jnp.tile(k_norm, reps).reshape(1, DA_C), bd)


def _attn_kernel(q_ref, k_ref, vt_ref, lq1_ref, lk1_ref, lq2_ref, lk2_ref, sub_ref, o_ref,
                 acc_ref, sa_ref, sb_ref, pa_ref, pb_ref, *, tq, tk):
    q = q_ref[...]
    lane = lax.broadcasted_iota(jnp.int32, q.shape, 1)
    zero = jnp.zeros_like(q)
    qs = jnp.concatenate([jnp.where(lane < DA_DH, q, zero), jnp.where(lane >= DA_DH, q, zero)], axis=0)
    nk = k_ref.shape[0] // tk

    def scores(i):
        return _dot_nt(k_ref[i * tk:(i + 1) * tk, :], qs)

    def scores_into(s_ref, i):
        s = scores(i)
        s_ref[...] = s
        return jnp.max(s, axis=0, keepdims=True)

    def softmax(s_ref, p_ref, m_prev, m_blk):
        m_new = jnp.maximum(m_prev, m_blk)
        p_ref[...] = jnp.exp2((s_ref[...] - m_new).astype(BF16))
        return m_new, jnp.exp2(m_prev - m_new)

    def accumulate(p_ref, i, corr):
        pv = _dot(vt_ref[i], p_ref[...])
        acc_ref[...] = pv if i == 0 else corr * acc_ref[...] + pv

    s_refs = (sa_ref, sb_ref)
    p_refs = (pa_ref, pb_ref)
    m_blk = scores_into(s_refs[0], 0)
    m = jnp.full((1, 2 * tq), -jnp.inf, F32)
    corr = None
    for i in range(nk):
        m_next = scores_into(s_refs[(i + 1) % 2], i + 1) if i + 1 < nk else None
        if i > 0:
            accumulate(p_refs[(i - 1) % 2], i - 1, corr)
        m, corr = softmax(s_refs[i % 2], p_refs[i % 2], m, m_blk)
        m_blk = m_next
    accumulate(p_refs[(nk - 1) % 2], nk - 1, corr)

    hw = 2 * DA_DH
    o2 = acc_ref[0:hw, :] / acc_ref[hw:hw + 1, :]
    lam = (jnp.exp(jnp.sum(lq1_ref[...] * lk1_ref[...], axis=1, keepdims=True))
           - jnp.exp(jnp.sum(lq2_ref[...] * lk2_ref[...], axis=1, keepdims=True))) + LAM_INIT
    o = jnp.transpose(o2[:, 0:tq] - lam * o2[:, tq:2 * tq])
    o_ref[...] = _rms(o, sub_ref[...]) * (1.0 - LAM_INIT)


def _attn(q, k, vt, bsz, s_len, lq1, lk1, lq2, lk2, subln):
    tq = min(ATT_TQ, s_len)
    tk = min(ATT_TK, s_len)
    nk = s_len // tk
    hw = 2 * DA_DH
    shp = lambda a: a.reshape(bsz, s_len, DA_C)
    lvec = pl.BlockSpec((1, DA_DH), lambda b, h, qi: (0, 0))
    o = pl.pallas_call(
        functools.partial(_attn_kernel, tq=tq, tk=tk),
        grid=(bsz, DA_HEADS, s_len // tq),
        in_specs=[
            pl.BlockSpec((None, tq, hw), lambda b, h, qi: (b, qi, h)),
            pl.BlockSpec((None, s_len, hw), lambda b, h, qi: (b, 0, h)),
            pl.BlockSpec((None, nk, None, VT_ROWS, tk), lambda b, h, qi: (b, 0, h, 0, 0)),
            lvec, lvec, lvec, lvec,
            pl.BlockSpec((1, hw), lambda b, h, qi: (0, 0)),
        ],
        out_specs=pl.BlockSpec((None, tq, hw), lambda b, h, qi: (b, qi, h)),
        out_shape=jax.ShapeDtypeStruct((bsz, s_len, DA_C), F32),
        scratch_shapes=[pltpu.VMEM((VT_ROWS, 2 * tq), F32),
                        pltpu.VMEM((tk, 2 * tq), F32), pltpu.VMEM((tk, 2 * tq), F32),
                        pltpu.VMEM((tk, 2 * tq), BF16), pltpu.VMEM((tk, 2 * tq), BF16)],
        compiler_params=_params(("parallel", "parallel", "arbitrary"), 32),
        name="attn",
    )(shp(q), shp(k), vt, lq1.reshape(1, DA_DH), lk1.reshape(1, DA_DH), lq2.reshape(1, DA_DH),
      lk2.reshape(1, DA_DH), subln.reshape(1, hw))
    return o.reshape(bsz * s_len, DA_C)


def _mixout_ffn_kernel(x_ref, yf_ref, yr_ref, g_ref, bonus_ref, da_ref, lng_ref, lnb_ref, bd_ref, wout_ref,
                       fg_ref, wgu_ref, wd_ref, final_ref, o_ref):
    bd = bd_ref[...]
    y = yf_ref[...] + yr_ref[...]
    mu = _group_sum(y, bd) * (1.0 / RW_N)
    yc = y - mu
    var = _group_sum(yc * yc, bd) * (1.0 / RW_N)
    yn = yc * lax.rsqrt(var + GN_EPS) * lng_ref[...] + lnb_ref[...]
    y_rw = ((yn + bonus_ref[...]) * g_ref[...]).astype(BF16)
    mixed = _dot(y_rw, wout_ref[0:RW_C, :]) + _dot(da_ref[...].astype(BF16), wout_ref[RW_C:RW_C + DA_C, :])
    x2 = x_ref[...] + mixed
    o_ref[...] = _rms(_half_swiglu(x2, fg_ref, wgu_ref, wd_ref), final_ref[...])


def _mixout_ffn(x1, yf, yr, g, bonus, o_da, ln_g, ln_b, bd, w_out, norm_g, w_gu, w_down, final_g):
    t = x1.shape[0]
    tm = min(TOKEN_TILE, t)
    tok = pl.BlockSpec((tm, RW_C), lambda i: (i, 0))
    return pl.pallas_call(
        _mixout_ffn_kernel,
        grid=(t // tm,),
        in_specs=[
            pl.BlockSpec((tm, D_MODEL), lambda i: (i, 0)),
            tok, tok, tok, tok, tok,
            _resident((1, RW_C)), _resident((1, RW_C)), _resident((RW_C, RW_C)), _resident((RW_C + DA_C, D_MODEL)),
            _resident((1, D_MODEL)), _resident((D_MODEL, 2 * D_FF)), _resident((D_FF, D_MODEL)),
            _resident((1, D_MODEL)),
        ],
        out_specs=pl.BlockSpec((tm, D_MODEL), lambda i: (i, 0)),
        out_shape=jax.ShapeDtypeStruct((t, D_MODEL), F32),
        compiler_params=_params(("parallel",), 56),
        name="mixout_ffn",
    )(x1, yf, yr, g, bonus, o_da, ln_g.reshape(1, RW_C), ln_b.reshape(1, RW_C), bd, w_out.astype(BF16),
      norm_g.reshape(1, D_MODEL), w_gu.astype(BF16), w_down.astype(BF16), final_g.reshape(1, D_MODEL))


def _encoder_layer(x, p):
    bsz, s_len, _ = x.shape
    t = bsz * s_len
    xf = x.reshape(t, D_MODEL)
    lane = jnp.arange(RW_C) // RW_N
    bd = (lane[:, None] == lane[None, :]).astype(BF16)

    x1 = _ffn(xf, p["ffn1_norm"], p["ffn1_w_gu"], p["ffn1_w_down"])
    r, v, kk, g, bonus, lw, kd, b = _rwprep(
        x1, s_len, p["mix_norm"], p["w_in"][:, :RW_COLS], p["conv_w"], p["rw_w0"], p["rw_w_up"], p["rw_a0"],
        p["rw_a_up"], p["rw_g_up"], p["rw_k_k"], p["rw_k_a"], p["rw_r_k"], bd)
    yf, yr = _scan(r, v, kk, lw, kd, b, bsz, s_len)
    q, k, va = _daprep(x1, bsz, s_len, p["mix_norm"], p["w_in"][:, RW_COLS:], p["da_q_norm"], p["da_k_norm"], bd)
    o_da = _attn(q, k, va, bsz, s_len, p["da_lq1"], p["da_lk1"], p["da_lq2"], p["da_lk2"], p["da_subln"])
    out = _mixout_ffn(x1, yf, yr, g, bonus, o_da, p["rw_ln_g"], p["rw_ln_b"], bd, p["w_out"],
                      p["ffn2_norm"], p["ffn2_w_gu"], p["ffn2_w_down"], p["final_norm"])
    return out.reshape(bsz, s_len, D_MODEL)


def kernel(x_prompt, x_sample, ffn1_norm, ffn1_w_gu, ffn1_w_down, mix_norm, w_in, conv_w, rw_w0, rw_w_up, rw_a0, rw_a_up, rw_g_up, rw_k_k, rw_k_a, rw_r_k, rw_ln_g, rw_ln_b, da_q_norm, da_k_norm, da_lq1, da_lk1, da_lq2, da_lk2, da_subln, w_out, ffn2_norm, ffn2_w_gu, ffn2_w_down, final_norm):
    names = ("ffn1_norm", "ffn1_w_gu", "ffn1_w_down", "mix_norm", "w_in", "conv_w", "rw_w0", "rw_w_up", "rw_a0",
             "rw_a_up", "rw_g_up", "rw_k_k", "rw_k_a", "rw_r_k", "rw_ln_g", "rw_ln_b", "da_q_norm", "da_k_norm",
             "da_lq1", "da_lk1", "da_lq2", "da_lk2", "da_subln", "w_out", "ffn2_norm", "ffn2_w_gu", "ffn2_w_down",
             "final_norm")
    vals = (ffn1_norm, ffn1_w_gu, ffn1_w_down, mix_norm, w_in, conv_w, rw_w0, rw_w_up, rw_a0, rw_a_up, rw_g_up,
            rw_k_k, rw_k_a, rw_r_k, rw_ln_g, rw_ln_b, da_q_norm, da_k_norm, da_lq1, da_lk1, da_lq2, da_lk2,
            da_subln, w_out, ffn2_norm, ffn2_w_gu, ffn2_w_down, final_norm)
    assert all(a.shape[0] == 1 for a in vals), "single-layer (depth 1) parameter stacks expected"
    p = {n: a[0] for n, a in zip(names, vals)}
    return (_encoder_layer(x_prompt, p), _encoder_layer(x_sample, p))
```

```python
import functools
import math

import jax
import jax.numpy as jnp
from jax import lax
from jax.experimental import pallas as pl
from jax.experimental.pallas import tpu as pltpu

F32 = jnp.float32
BF16 = jnp.bfloat16

D_MODEL = 1024
D_FF = 2816
RW_HEADS = 8
RW_N = 64
RW_C = RW_HEADS * RW_N
LORA_W = 64
LORA_A = 64
LORA_G = 128
DECAY_SCALE = 0.606531
GN_EPS = 64e-5
DA_HEADS = 4
DA_DH = 64
DA_C = DA_HEADS * 2 * DA_DH
ROT_DIM = DA_DH // 4
ROPE_THETA = 500000.0
NORM_EPS = 1e-6
RW_COLS = 3 * RW_C + 2 * LORA_W + 2 * LORA_A + LORA_G
LAM_INIT = 0.8 - 0.6 * math.exp(-0.3 * 0)
LOG2E = math.log2(math.e)

V7X_VMEM_BYTES = 64 * 1024 * 1024
SUBLANES = 8

FF_CHUNK = 256
N_FF_CHUNKS = D_FF // FF_CHUNK
TOKEN_TILE = 512
PREP_TILE = 256
CHUNK = 64
SCAN_CHUNKS_PER_STEP = 4
ATT_TQ = 512
ATT_STEP_WORK = 4 * 1024 * 1024
ONES_ROWS = 16
VT_ROWS = 2 * DA_DH + ONES_ROWS
ATT_TK = 512


def _params(sem, vmem_mb):
    return pltpu.CompilerParams(dimension_semantics=sem, vmem_limit_bytes=vmem_mb * 1024 * 1024)


def _dot(a, b):
    return jnp.dot(a, b, preferred_element_type=F32)


def _dot_nt(a, b):
    return lax.dot_general(a, b, (((1,), (1,)), ((), ())), preferred_element_type=F32)


def _dot_tn(a, b):
    return lax.dot_general(a, b, (((0,), (0,)), ((), ())), preferred_element_type=F32)


def _split2(x):
    hi = x.astype(BF16)
    lo = (x - hi.astype(F32)).astype(BF16)
    return hi, lo


def _split3(x):
    hi = x.astype(BF16)
    r1 = x - hi.astype(F32)
    mid = r1.astype(BF16)
    lo = (r1 - mid.astype(F32)).astype(BF16)
    return hi, mid, lo


def _group_sum(x, bd):
    hi, lo = _split2(x)
    return _dot(hi, bd) + _dot(lo, bd)


def _rms(x, g, eps=NORM_EPS):
    return x * lax.rsqrt(jnp.mean(x * x, axis=-1, keepdims=True) + eps) * g


def _half_swiglu(x, g_ref, wgu_ref, wd_ref):
    h = _rms(x, g_ref[...]).astype(BF16)
    acc = jnp.zeros_like(x)
    for j in range(N_FF_CHUNKS):
        lo = j * FF_CHUNK
        gate = _dot(h, wgu_ref[:, lo:lo + FF_CHUNK])
        up = _dot(h, wgu_ref[:, D_FF + lo:D_FF + lo + FF_CHUNK])
        act = (gate * jax.nn.sigmoid(gate) * up).astype(BF16)
        acc = acc + _dot(act, wd_ref[lo:lo + FF_CHUNK, :])
    return x + 0.5 * acc


def _ffn_kernel(x_ref, g_ref, wgu_ref, wd_ref, o_ref):
    o_ref[...] = _half_swiglu(x_ref[...], g_ref, wgu_ref, wd_ref)


def _resident(shape):
    return pl.BlockSpec(shape, lambda i: (0,) * len(shape), pipeline_mode=pl.Buffered(1))


def _ffn(x, norm_g, w_gu, w_down):
    t = x.shape[0]
    tm = min(TOKEN_TILE, t)
    return pl.pallas_call(
        _ffn_kernel,
        grid=(t // tm,),
        in_specs=[
            pl.BlockSpec((tm, D_MODEL), lambda i: (i, 0)),
            _resident((1, D_MODEL)),
            _resident((D_MODEL, 2 * D_FF)),
            _resident((D_FF, D_MODEL)),
        ],
        out_specs=pl.BlockSpec((tm, D_MODEL), lambda i: (i, 0)),
        out_shape=jax.ShapeDtypeStruct((t, D_MODEL), F32),
        compiler_params=_params(("parallel",), 48),
        name="ffn",
    )(x, norm_g.reshape(1, D_MODEL), w_gu.astype(BF16), w_down.astype(BF16))


def _rwprep_kernel(x_ref, xp_ref, xn_ref, g_ref, w_ref, conv_ref, w0_ref, wup_ref, a0_ref, aup_ref, gup_ref,
                   kk_ref, ka_ref, rk_ref, bd_ref,
                   r_o, v_o, kk_o, g_o, bonus_o, lw_o, kd_o, b_o, *, tm, s_len):
    i = pl.program_id(0)
    xa = jnp.concatenate([x_ref[...], xp_ref[...], xn_ref[...]], axis=0)
    ua = _dot(_rms(xa, g_ref[...]).astype(BF16), w_ref[...])
    u = ua[0:tm, :]
    start = i * tm
    first = (start % s_len) == 0
    last = ((start + tm) % s_len) == 0
    prev_row = jnp.where(first, 0.0, ua[tm + SUBLANES - 1:tm + SUBLANES, :])
    next_row = jnp.where(last, 0.0, ua[tm + SUBLANES:tm + SUBLANES + 1, :])
    row = lax.broadcasted_iota(jnp.int32, (tm, 1), 0)
    prev = jnp.where(row == 0, prev_row, pltpu.roll(u, 1, 0))
    nxt = jnp.where(row == tm - 1, next_row, pltpu.roll(u, tm - 1, 0))
    c = prev * conv_ref[0:1, :] + u * conv_ref[1:2, :] + nxt * conv_ref[2:3, :]

    o_w = 3 * RW_C
    o_a = o_w + 2 * LORA_W
    o_g = o_a + 2 * LORA_A
    r = c[:, 0:RW_C]
    k = c[:, RW_C:2 * RW_C]
    v = c[:, 2 * RW_C:3 * RW_C]
    tw = jnp.tanh(c[:, o_w:o_a]).astype(BF16)
    ad = c[:, o_a:o_g].astype(BF16)
    gd = c[:, o_g:RW_COLS]
    bd = bd_ref[...]

    kkv = k * kk_ref[...]
    norm = jnp.sqrt(_group_sum(kkv * kkv, bd))
    kk = kkv / jnp.maximum(norm, 1e-12)
    g = _dot(jax.nn.sigmoid(gd).astype(BF16), gup_ref[...])

    ksum = jnp.zeros_like(k)
    for d in range(2):
        lw = -DECAY_SCALE * jax.nn.sigmoid(w0_ref[d:d + 1, :] + _dot(tw, wup_ref[d]))
        a = jax.nn.sigmoid(a0_ref[d:d + 1, :] + _dot(ad, aup_ref[d]))
        kd = k * (1.0 + (a - 1.0) * ka_ref[...])
        lw_o[d] = lw
        kd_o[d] = kd
        b_o[d] = kk * a
        ksum = ksum + kd
    coef = _group_sum(r * ksum * rk_ref[...], bd)
    r_o[...] = r
    v_o[...] = v
    kk_o[...] = kk
    g_o[...] = g
    bonus_o[...] = coef * v


def _pad_dir_rows(w_up):
    r = w_up.shape[1]
    z = jnp.zeros_like(w_up[0])
    return jnp.stack([jnp.concatenate([w_up[0], z], 0), jnp.concatenate([z, w_up[1]], 0)]).astype(BF16)


def _rwprep(x1, s_len, norm_g, w_rw, conv_w, w0, w_up, a0, a_up, g_up, k_k, k_a, r_k, bd):
    t = x1.shape[0]
    tm = min(PREP_TILE, s_len)
    nb8 = t // SUBLANES
    per = tm // SUBLANES
    const2 = lambda i: (0, 0)
    const3 = lambda i: (0, 0, 0)
    tok = pl.BlockSpec((tm, RW_C), lambda i: (i, 0))
    tok2 = pl.BlockSpec((2, tm, RW_C), lambda i: (0, i, 0))
    vec = pl.BlockSpec((1, RW_C), const2)
    one = jax.ShapeDtypeStruct((t, RW_C), F32)
    two = jax.ShapeDtypeStruct((2, t, RW_C), F32)
    return pl.pallas_call(
        functools.partial(_rwprep_kernel, tm=tm, s_len=s_len),
        grid=(t // tm,),
        in_specs=[
            pl.BlockSpec((tm, D_MODEL), lambda i: (i, 0)),
            pl.BlockSpec((SUBLANES, D_MODEL), lambda i: (jnp.maximum(i * per - 1, 0), 0)),
            pl.BlockSpec((SUBLANES, D_MODEL), lambda i: (jnp.minimum((i + 1) * per, nb8 - 1), 0)),
            pl.BlockSpec((1, D_MODEL), const2),
            pl.BlockSpec((D_MODEL, RW_COLS), const2),
            pl.BlockSpec((3, RW_COLS), const2),
            pl.BlockSpec((2, RW_C), const2),
            pl.BlockSpec((2, 2 * LORA_W, RW_C), const3),
            pl.BlockSpec((2, RW_C), const2),
            pl.BlockSpec((2, 2 * LORA_A, RW_C), const3),
            pl.BlockSpec((LORA_G, RW_C), const2),
            vec, vec, vec,
            pl.BlockSpec((RW_C, RW_C), const2),
        ],
        out_specs=[tok, tok, tok, tok, tok, tok2, tok2, tok2],
        out_shape=[one, one, one, one, one, two, two, two],
        compiler_params=_params(("parallel",), 48),
        name="rwprep",
    )(x1, x1, x1, norm_g.reshape(1, D_MODEL), w_rw.astype(BF16), conv_w, w0, _pad_dir_rows(w_up), a0, _pad_dir_rows(a_up),
      g_up.astype(BF16), k_k.reshape(1, RW_C), k_a.reshape(1, RW_C), r_k.reshape(1, RW_C), bd)


def _scan_kernel(rf_ref, vf_ref, kkf_ref, lwf_ref, kdf_ref, bf_ref, rr_ref, vr_ref, kkr_ref, lwr_ref, kdr_ref, br_ref,
                 yf_ref, yr_ref, h_ref, *, chunk):
    L = chunk
    L2 = 2 * L
    PW = 2 * RW_N
    n_pairs = RW_C // PW
    n_sub = lwf_ref.shape[0] // L
    assert L2 == PW, "interaction matrices and lane pairs share the 128-lane tile"

    @pl.when(pl.program_id(1) == 0)
    def _():
        h_ref[...] = jnp.zeros_like(h_ref)

    row = lax.broadcasted_iota(jnp.int32, (L, L), 0)
    col = lax.broadcasted_iota(jnp.int32, (L, L), 1)
    row2 = lax.broadcasted_iota(jnp.int32, (L, L2), 0)
    col2 = lax.broadcasted_iota(jnp.int32, (L, L2), 1) & (L - 1)
    eye2 = (row2 == col2).astype(F32)
    first_head = lax.broadcasted_iota(jnp.int32, (L, PW), 1) < RW_N

    def stack(z):
        zero = jnp.zeros_like(z)
        return jnp.concatenate([jnp.where(first_head, z, zero), jnp.where(first_head, zero, z)], axis=0)

    chains = []
    dirs = ((rf_ref, vf_ref, kkf_ref, lwf_ref, kdf_ref, bf_ref, yf_ref, 1),
            (rr_ref, vr_ref, kkr_ref, lwr_ref, kdr_ref, br_ref, yr_ref, -1))
    for d, (r_ref, v_ref, kk_ref, lw_ref, kd_ref, b_ref, y_ref, sgn) in enumerate(dirs):
        incl2 = (row2 - col2) * sgn >= 0
        strict2 = (row2 - col2) * sgn > 0
        tri = ((row - col) * sgn >= 0).astype(BF16)
        for step, j in enumerate(range(n_sub) if sgn > 0 else reversed(range(n_sub))):
            rows = slice(j * L, (j + 1) * L)
            lw = lw_ref[rows, :]
            hi, mid, lo = _split3(lw)
            c_in = _dot(tri, hi) + _dot(tri, mid) + _dot(tri, lo)
            c_tot = jnp.sum(lw, axis=0, keepdims=True)
            e_neg = jnp.exp(-c_in)
            e_end = jnp.exp(c_tot - c_in)
            kk = kk_ref[rows, :]
            b = b_ref[rows, :]
            kd = kd_ref[rows, :]
            a_t = -kk * jnp.exp(c_in - lw)
            r_t = r_ref[rows, :] * jnp.exp(c_in)
            b_t = b * e_neg
            k_t = kd * e_neg
            b_p = b * e_end
            k_p = kd * e_end
            p_end = jnp.exp(c_tot)
            v = v_ref[rows, :]
            for p in range(n_pairs):
                ls = slice(p * PW, (p + 1) * PW)
                a_n = a_t[:, ls].astype(BF16)
                chains.append(dict(
                    d=d, p=p, step=step, rows=rows, ls=ls, y_ref=y_ref, incl=incl2, strict=strict2,
                    ar=jnp.concatenate([a_n, r_t[:, ls].astype(BF16)], axis=0),
                    bk=jnp.concatenate([stack(b_t[:, ls].astype(BF16)), stack(k_t[:, ls].astype(BF16))], axis=0),
                    aa=stack(a_n), vv=stack(v[:, ls].astype(BF16)),
                    bpT=jnp.transpose(stack(b_p[:, ls])).astype(BF16),
                    kpT=jnp.transpose(stack(k_p[:, ls])).astype(BF16),
                    pe_col=jnp.transpose(jnp.broadcast_to(p_end[:, ls], (PW, PW)))))

    for c in chains:
        x = _dot_nt(c["ar"], c["bk"])
        c["a_ab"] = jnp.where(c["strict"], x[:L, :L2], 0.0)
        c["a_ak"] = jnp.where(c["strict"], x[:L, L2:], 0.0).astype(BF16)
        c["a_rb"] = jnp.where(c["incl"], x[L:, :L2], 0.0).astype(BF16)
        c["a_rk"] = jnp.where(c["incl"], x[L:, L2:], 0.0).astype(BF16)
        c["inv"] = eye2 + c["a_ab"]
        c["pw"] = c["a_ab"].astype(BF16)
    n_sq = int(math.log2(L)) - 1
    for k in range(n_sq + 1):
        for c in chains:
            pws = stack(c["pw"])
            if k == 0:
                c["pw"] = _dot(c["pw"], pws).astype(BF16)
            elif k < n_sq:
                both = _dot(jnp.concatenate([c["pw"], c["inv"].astype(BF16)], axis=0), pws)
                c["pw"] = both[:L, :].astype(BF16)
                c["inv"] = c["inv"] + both[L:, :]
            else:
                c["inv"] = c["inv"] + _dot(c["inv"].astype(BF16), pws)
    for c in chains:
        c["akv"] = stack(_dot(c["a_ak"], c["vv"]).astype(BF16))
    for c in chains:
        inv = c["inv"].astype(BF16)
        c["wi"] = jnp.concatenate([_dot(inv, c["aa"]).astype(BF16), inv], axis=1)
        c["bkT"] = jnp.concatenate([c["bpT"], c["kpT"]], axis=1)
        c["rab"] = jnp.concatenate([c["ar"][L:, :], c["a_rb"], c["a_rk"]], axis=1)
    state = {(d, p): h_ref[d, p] for d in range(2) for p in range(n_pairs)}
    for step in range(n_sub):
        now = [c for c in chains if c["step"] == step]
        for c in now:
            c["h"] = state[c["d"], c["p"]]
            c["hb"] = c["h"].astype(BF16)
            u = _dot(c["wi"], jnp.concatenate([c["hb"], c["akv"]], axis=0))
            c["uv"] = jnp.concatenate([stack(u.astype(BF16)), c["vv"]], axis=0)
        for c in now:
            state[c["d"], c["p"]] = c["pe_col"] * c["h"] + _dot(c["bkT"], c["uv"])
        for c in now:
            c["y_ref"][c["rows"], c["ls"]] = _dot(c["rab"], jnp.concatenate([c["hb"], c["uv"]], axis=0))
    for (d, p), h in state.items():
        h_ref[d, p] = h


def _scan(r, v, kk, lw, kd, b, bsz, s_len):
    rows = CHUNK * SCAN_CHUNKS_PER_STEP
    nc = s_len // rows
    shp = lambda a: a.reshape(bsz, s_len, RW_C)
    shp2 = lambda a: a.reshape(2, bsz, s_len, RW_C)
    f_sh = pl.BlockSpec((None, rows, RW_C), lambda bi, c: (bi, c, 0))
    r_sh = pl.BlockSpec((None, rows, RW_C), lambda bi, c: (bi, nc - 1 - c, 0))
    f_pd = pl.BlockSpec((None, None, rows, RW_C), lambda bi, c: (0, bi, c, 0))
    r_pd = pl.BlockSpec((None, None, rows, RW_C), lambda bi, c: (1, bi, nc - 1 - c, 0))
    out = jax.ShapeDtypeStruct((bsz, s_len, RW_C), F32)
    r3, v3, kk3, lw4, kd4, b4 = shp(r), shp(v), shp(kk), shp2(lw), shp2(kd), shp2(b)
    yf, yr = pl.pallas_call(
        functools.partial(_scan_kernel, chunk=CHUNK),
        grid=(bsz, nc),
        in_specs=[f_sh, f_sh, f_sh, f_pd, f_pd, f_pd, r_sh, r_sh, r_sh, r_pd, r_pd, r_pd],
        out_specs=[f_sh, r_sh],
        out_shape=[out, out],
        scratch_shapes=[pltpu.VMEM((2, RW_C // (2 * RW_N), 2 * RW_N, 2 * RW_N), F32)],
        compiler_params=_params(("parallel", "arbitrary"), 32),
        name="scan",
    )(r3, v3, kk3, lw4, kd4, b4, r3, v3, kk3, lw4, kd4, b4)
    return yf.reshape(bsz * s_len, RW_C), yr.reshape(bsz * s_len, RW_C)


def _daprep_kernel(x_ref, g_ref, w_ref, cf_ref, sn_ref, sp_ref, qn_ref, kn_ref, bd_ref, q_o, k_o, v_o):
    bd = bd_ref[...]
    cf, sn, sp = cf_ref[...], sn_ref[...], sp_ref[...]
    p = _dot(_rms(x_ref[...], g_ref[...]).astype(BF16), w_ref[...])

    def norm_rope(x, g):
        ms = _group_sum(x * x, bd) * (1.0 / DA_DH)
        xn = x * lax.rsqrt(ms + NORM_EPS) * g
        half = ROT_DIM // 2
        return xn * cf + pltpu.roll(xn, DA_C - half, 1) * sn + pltpu.roll(xn, half, 1) * sp

    q = norm_rope(p[:, 0:DA_C], qn_ref[...]) * (DA_DH ** -0.5 * LOG2E)
    k = norm_rope(p[:, DA_C:2 * DA_C], kn_ref[...])
    q_o[...] = q.astype(BF16)
    k_o[...] = k.astype(BF16)
    vt = jnp.transpose(p[:, 2 * DA_C:3 * DA_C]).astype(BF16)
    hw = 2 * DA_DH
    for h in range(DA_HEADS):
        v_o[h, 0:hw, :] = vt[h * hw:(h + 1) * hw, :]
        v_o[h, hw:hw + ONES_ROWS, :] = jnp.ones((ONES_ROWS, vt.shape[1]), BF16)


def _rope_lane_tables(s_len):
    half = ROT_DIM // 2
    inv_freq = ROPE_THETA ** (-jnp.arange(0, ROT_DIM, 2, dtype=F32) / ROT_DIM)
    ang = jnp.arange(s_len, dtype=F32)[:, None] * inv_freq[None, :]
    cos, sin = jnp.cos(ang), jnp.sin(ang)
    j = jnp.arange(DA_C) % DA_DH
    f = j % half
    cf = jnp.where(j < ROT_DIM, cos[:, f], 1.0)
    sn = jnp.where(j < half, -sin[:, f], 0.0)
    sp = jnp.where((j >= half) & (j < ROT_DIM), sin[:, f], 0.0)
    return cf, sn, sp


def _daprep(x1, bsz, s_len, norm_g, w_da, q_norm, k_norm, bd):
    t = x1.shape[0]
    tm = min(ATT_TK, s_len)
    per_seq = s_len // tm
    cf, sn, sp = _rope_lane_tables(s_len)
    const2 = lambda i: (0, 0)
    tok = pl.BlockSpec((tm, DA_C), lambda i: (i, 0))
    tab = pl.BlockSpec((tm, DA_C), lambda i: (i % per_seq, 0))
    vec = pl.BlockSpec((1, DA_C), const2)
    out = jax.ShapeDtypeStruct((t, DA_C), BF16)
    reps = DA_C // DA_DH
    return pl.pallas_call(
        _daprep_kernel,
        grid=(t // tm,),
        in_specs=[pl.BlockSpec((tm, D_MODEL), lambda i: (i, 0)), pl.BlockSpec((1, D_MODEL), const2),
                  pl.BlockSpec((D_MODEL, 3 * DA_C), const2), tab, tab, tab, vec, vec,
                  pl.BlockSpec((DA_C, DA_C), const2)],
        out_specs=[tok, tok, pl.BlockSpec((None, None, DA_HEADS, VT_ROWS, tm),
                                          lambda i: (i // per_seq, i % per_seq, 0, 0, 0))],
        out_shape=[out, out, jax.ShapeDtypeStruct((bsz, per_seq, DA_HEADS, VT_ROWS, tm), BF16)],
        compiler_params=_params(("parallel",), 32),
        name="daprep",
    )(x1, norm_g.reshape(1, D_MODEL), w_da.astype(BF16), cf, sn, sp,
      jnp.tile(q_norm, reps).reshape(1, DA_C), jnp.tile(k_norm, reps).reshape(1, DA_C), bd)


def _attn_kernel(q_ref, k_ref, vt_ref, lq1_ref, lk1_ref, lq2_ref, lk2_ref, sub_ref, o_ref,
                 acc_ref, sa_ref, sb_ref, pa_ref, pb_ref, *, tq, tk):
    q = q_ref[...]
    lane = lax.broadcasted_iota(jnp.int32, q.shape, 1)
    zero = jnp.zeros_like(q)
    qs = jnp.concatenate([jnp.where(lane < DA_DH, q, zero), jnp.where(lane >= DA_DH, q, zero)], axis=0)
    nk = k_ref.shape[0] // tk

    def scores(i):
        return _dot_nt(k_ref[i * tk:(i + 1) * tk, :], qs)

    def scores_into(s_ref, i):
        s = scores(i)
        s_ref[...] = s.astype(BF16)
        return jnp.max(s, axis=0, keepdims=True)

    def softmax(s_ref, p_ref, m_prev, m_blk):
        m_new = jnp.maximum(m_prev, m_blk.astype(BF16).astype(F32))
        p_ref[...] = jnp.exp2(s_ref[...] - m_new.astype(BF16))
        return m_new, jnp.exp2(m_prev - m_new)

    def accumulate(p_ref, i, corr):
        pv = _dot(vt_ref[i], p_ref[...])
        acc_ref[...] = pv if i == 0 else corr * acc_ref[...] + pv

    s_refs = (sa_ref, sb_ref)
    p_refs = (pa_ref, pb_ref)
    m_blk = scores_into(s_refs[0], 0)
    m = jnp.full((1, 2 * tq), -jnp.inf, F32)
    corr = None
    for i in range(nk):
        m_next = scores_into(s_refs[(i + 1) % 2], i + 1) if i + 1 < nk else None
        if i > 0:
            accumulate(p_refs[(i - 1) % 2], i - 1, corr)
        m, corr = softmax(s_refs[i % 2], p_refs[i % 2], m, m_blk)
        m_blk = m_next
    accumulate(p_refs[(nk - 1) % 2], nk - 1, corr)

    hw = 2 * DA_DH
    o2 = acc_ref[0:hw, :] / acc_ref[hw:hw + 1, :]
    lam = (jnp.exp(jnp.sum(lq1_ref[...] * lk1_ref[...], axis=1, keepdims=True))
           - jnp.exp(jnp.sum(lq2_ref[...] * lk2_ref[...], axis=1, keepdims=True))) + LAM_INIT
    o = jnp.transpose(o2[:, 0:tq] - lam * o2[:, tq:2 * tq])
    o_ref[...] = _rms(o, sub_ref[...]) * (1.0 - LAM_INIT)


def _attn(q, k, vt, bsz, s_len, lq1, lk1, lq2, lk2, subln):
    tq = min(max(ATT_TQ, ATT_STEP_WORK // s_len), 2 * ATT_TQ, s_len)
    tk = min(ATT_TK, s_len)
    nk = s_len // tk
    hw = 2 * DA_DH
    shp = lambda a: a.reshape(bsz, s_len, DA_C)
    lvec = pl.BlockSpec((1, DA_DH), lambda b, h, qi: (0, 0))
    o = pl.pallas_call(
        functools.partial(_attn_kernel, tq=tq, tk=tk),
        grid=(bsz, DA_HEADS, s_len // tq),
        in_specs=[
            pl.BlockSpec((None, tq, hw), lambda b, h, qi: (b, qi, h)),
            pl.BlockSpec((None, s_len, hw), lambda b, h, qi: (b, 0, h)),
            pl.BlockSpec((None, nk, None, VT_ROWS, tk), lambda b, h, qi: (b, 0, h, 0, 0)),
            lvec, lvec, lvec, lvec,
            pl.BlockSpec((1, hw), lambda b, h, qi: (0, 0)),
        ],
        out_specs=pl.BlockSpec((None, tq, hw), lambda b, h, qi: (b, qi, h)),
        out_shape=jax.ShapeDtypeStruct((bsz, s_len, DA_C), F32),
        scratch_shapes=[pltpu.VMEM((VT_ROWS, 2 * tq), F32),
                        pltpu.VMEM((tk, 2 * tq), BF16), pltpu.VMEM((tk, 2 * tq), BF16),
                        pltpu.VMEM((tk, 2 * tq), BF16), pltpu.VMEM((tk, 2 * tq), BF16)],
        compiler_params=_params(("parallel", "parallel", "arbitrary"), 32),
        name="attn",
    )(shp(q), shp(k), vt, lq1.reshape(1, DA_DH), lk1.reshape(1, DA_DH), lq2.reshape(1, DA_DH),
      lk2.reshape(1, DA_DH), subln.reshape(1, hw))
    return o.reshape(bsz * s_len, DA_C)


def _mixout_ffn_kernel(x_ref, yf_ref, yr_ref, g_ref, bonus_ref, da_ref, lng_ref, lnb_ref, bd_ref, wout_ref,
                       fg_ref, wgu_ref, wd_ref, final_ref, o_ref):
    bd = bd_ref[...]
    y = yf_ref[...] + yr_ref[...]
    mu = _group_sum(y, bd) * (1.0 / RW_N)
    yc = y - mu
    var = _group_sum(yc * yc, bd) * (1.0 / RW_N)
    yn = yc * lax.rsqrt(var + GN_EPS) * lng_ref[...] + lnb_ref[...]
    y_rw = ((yn + bonus_ref[...]) * g_ref[...]).astype(BF16)
    mixed = _dot(y_rw, wout_ref[0:RW_C, :]) + _dot(da_ref[...].astype(BF16), wout_ref[RW_C:RW_C + DA_C, :])
    x2 = x_ref[...] + mixed
    o_ref[...] = _rms(_half_swiglu(x2, fg_ref, wgu_ref, wd_ref), final_ref[...])


def _mixout_ffn(x1, yf, yr, g, bonus, o_da, ln_g, ln_b, bd, w_out, norm_g, w_gu, w_down, final_g):
    t = x1.shape[0]
    tm = min(TOKEN_TILE, t)
    tok = pl.BlockSpec((tm, RW_C), lambda i: (i, 0))
    return pl.pallas_call(
        _mixout_ffn_kernel,
        grid=(t // tm,),
        in_specs=[
            pl.BlockSpec((tm, D_MODEL), lambda i: (i, 0)),
            tok, tok, tok, tok, tok,
            _resident((1, RW_C)), _resident((1, RW_C)), _resident((RW_C, RW_C)), _resident((RW_C + DA_C, D_MODEL)),
            _resident((1, D_MODEL)), _resident((D_MODEL, 2 * D_FF)), _resident((D_FF, D_MODEL)),
            _resident((1, D_MODEL)),
        ],
        out_specs=pl.BlockSpec((tm, D_MODEL), lambda i: (i, 0)),
        out_shape=jax.ShapeDtypeStruct((t, D_MODEL), F32),
        compiler_params=_params(("parallel",), 56),
        name="mixout_ffn",
    )(x1, yf, yr, g, bonus, o_da, ln_g.reshape(1, RW_C), ln_b.reshape(1, RW_C), bd, w_out.astype(BF16),
      norm_g.reshape(1, D_MODEL), w_gu.astype(BF16), w_down.astype(BF16), final_g.reshape(1, D_MODEL))


def _encoder_layer(x, p):
    bsz, s_len, _ = x.shape
    t = bsz * s_len
    xf = x.reshape(t, D_MODEL)
    lane = jnp.arange(RW_C) // RW_N
    bd = (lane[:, None] == lane[None, :]).astype(BF16)

    x1 = _ffn(xf, p["ffn1_norm"], p["ffn1_w_gu"], p["ffn1_w_down"])
    r, v, kk, g, bonus, lw, kd, b = _rwprep(
        x1, s_len, p["mix_norm"], p["w_in"][:, :RW_COLS], p["conv_w"], p["rw_w0"], p["rw_w_up"], p["rw_a0"],
        p["rw_a_up"], p["rw_g_up"], p["rw_k_k"], p["rw_k_a"], p["rw_r_k"], bd)
    yf, yr = _scan(r, v, kk, lw, kd, b, bsz, s_len)
    q, k, va = _daprep(x1, bsz, s_len, p["mix_norm"], p["w_in"][:, RW_COLS:], p["da_q_norm"], p["da_k_norm"], bd)
    o_da = _attn(q, k, va, bsz, s_len, p["da_lq1"], p["da_lk1"], p["da_lq2"], p["da_lk2"], p["da_subln"])
    out = _mixout_ffn(x1, yf, yr, g, bonus, o_da, p["rw_ln_g"], p["rw_ln_b"], bd, p["w_out"],
                      p["ffn2_norm"], p["ffn2_w_gu"], p["ffn2_w_down"], p["final_norm"])
    return out.reshape(bsz, s_len, D_MODEL)


def kernel(x_prompt, x_sample, ffn1_norm, ffn1_w_gu, ffn1_w_down, mix_norm, w_in, conv_w, rw_w0, rw_w_up, rw_a0, rw_a_up, rw_g_up, rw_k_k, rw_k_a, rw_r_k, rw_ln_g, rw_ln_b, da_q_norm, da_k_norm, da_lq1, da_lk1, da_lq2, da_lk2, da_subln, w_out, ffn2_norm, ffn2_w_gu, ffn2_w_down, final_norm):
    names = ("ffn1_norm", "ffn1_w_gu", "ffn1_w_down", "mix_norm", "w_in", "conv_w", "rw_w0", "rw_w_up", "rw_a0",
             "rw_a_up", "rw_g_up", "rw_k_k", "rw_k_a", "rw_r_k", "rw_ln_g", "rw_ln_b", "da_q_norm", "da_k_norm",
             "da_lq1", "da_lk1", "da_lq2", "da_lk2", "da_subln", "w_out", "ffn2_norm", "ffn2_w_gu", "ffn2_w_down",
             "final_norm")
    vals = (ffn1_norm, ffn1_w_gu, ffn1_w_down, mix_norm, w_in, conv_w, rw_w0, rw_w_up, rw_a0, rw_a_up, rw_g_up,
            rw_k_k, rw_k_a, rw_r_k, rw_ln_g, rw_ln_b, da_q_norm, da_k_norm, da_lq1, da_lk1, da_lq2, da_lk2,
            da_subln, w_out, ffn2_norm, ffn2_w_gu, ffn2_w_down, final_norm)
    assert all(a.shape[0] == 1 for a in vals), "single-layer (depth 1) parameter stacks expected"
    p = {n: a[0] for n, a in zip(names, vals)}
    return (_encoder_layer(x_prompt, p), _encoder_layer(x_sample, p))
```

```python
import functools
import math

import jax
import jax.numpy as jnp
from jax import lax
from jax.experimental import pallas as pl
from jax.experimental.pallas import tpu as pltpu

F32 = jnp.float32
BF16 = jnp.bfloat16

D_MODEL = 1024
D_FF = 2816
RW_HEADS = 8
RW_N = 64
RW_C = RW_HEADS * RW_N
LORA_W = 64
LORA_A = 64
LORA_G = 128
DECAY_SCALE = 0.606531
GN_EPS = 64e-5
DA_HEADS = 4
DA_DH = 64
DA_C = DA_HEADS * 2 * DA_DH
ROT_DIM = DA_DH // 4
ROPE_THETA = 500000.0
NORM_EPS = 1e-6
RW_COLS = 3 * RW_C + 2 * LORA_W + 2 * LORA_A + LORA_G
LAM_INIT = 0.8 - 0.6 * math.exp(-0.3 * 0)
LOG2E = math.log2(math.e)

V7X_VMEM_BYTES = 64 * 1024 * 1024
SUBLANES = 8
VMEM_WEIGHTS_RESIDENT = V7X_VMEM_BYTES * 7 // 8
VMEM_TILES_ONLY = V7X_VMEM_BYTES // 2

FF_CHUNK = 256
N_FF_CHUNKS = D_FF // FF_CHUNK
TOKEN_TILE = 512
PREP_TILE = 512
CHUNK = 64
SCAN_CHUNKS_PER_STEP = 4
ATT_TQ = 512
ATT_STEP_WORK = 4 * 1024 * 1024
ONES_ROWS = 16
VT_ROWS = 2 * DA_DH + ONES_ROWS
ATT_TK = 512


def _params(sem, vmem_bytes):
    return pltpu.CompilerParams(dimension_semantics=sem, vmem_limit_bytes=vmem_bytes)


def _dot(a, b):
    return jnp.dot(a, b, preferred_element_type=F32)


def _dot_nt(a, b):
    return lax.dot_general(a, b, (((1,), (1,)), ((), ())), preferred_element_type=F32)


def _split2(x):
    hi = x.astype(BF16)
    lo = (x - hi.astype(F32)).astype(BF16)
    return hi, lo


def _split3(x):
    hi = x.astype(BF16)
    r1 = x - hi.astype(F32)
    mid = r1.astype(BF16)
    lo = (r1 - mid.astype(F32)).astype(BF16)
    return hi, mid, lo


def _group_sum(x, bd):
    hi, lo = _split2(x)
    return _dot(hi, bd) + _dot(lo, bd)


def _rms(x, g, eps=NORM_EPS):
    return x * lax.rsqrt(jnp.mean(x * x, axis=-1, keepdims=True) + eps) * g


def _half_swiglu(x, g_ref, wgu_ref, wd_ref):
    h = _rms(x, g_ref[...]).astype(BF16)
    acc = jnp.zeros_like(x)
    for j in range(N_FF_CHUNKS):
        lo = j * FF_CHUNK
        gate = _dot(h, wgu_ref[:, lo:lo + FF_CHUNK])
        up = _dot(h, wgu_ref[:, D_FF + lo:D_FF + lo + FF_CHUNK])
        act = (gate * jax.nn.sigmoid(gate) * up).astype(BF16)
        acc = acc + _dot(act, wd_ref[lo:lo + FF_CHUNK, :])
    return x + 0.5 * acc


def _ffn_kernel(x_ref, g_ref, wgu_ref, wd_ref, o_ref):
    o_ref[...] = _half_swiglu(x_ref[...], g_ref, wgu_ref, wd_ref)


def _resident(shape):
    return pl.BlockSpec(shape, lambda i: (0,) * len(shape), pipeline_mode=pl.Buffered(1))


def _ffn(x, norm_g, w_gu, w_down):
    t = x.shape[0]
    tm = min(TOKEN_TILE, t)
    return pl.pallas_call(
        _ffn_kernel,
        grid=(t // tm,),
        in_specs=[
            pl.BlockSpec((tm, D_MODEL), lambda i: (i, 0)),
            _resident((1, D_MODEL)),
            _resident((D_MODEL, 2 * D_FF)),
            _resident((D_FF, D_MODEL)),
        ],
        out_specs=pl.BlockSpec((tm, D_MODEL), lambda i: (i, 0)),
        out_shape=jax.ShapeDtypeStruct((t, D_MODEL), F32),
        compiler_params=_params(("parallel",), VMEM_WEIGHTS_RESIDENT),
        name="ffn",
    )(x, norm_g.reshape(1, D_MODEL), w_gu.astype(BF16), w_down.astype(BF16))


def _rwprep_kernel(x_ref, xp_ref, xn_ref, g_ref, w_ref, conv_ref, w0_ref, wup_ref, a0_ref, aup_ref, gup_ref,
                   kk_ref, ka_ref, rk_ref, bd_ref,
                   r_o, v_o, kk_o, g_o, bonus_o, lw_o, kd_o, b_o, *, tm, s_len):
    i = pl.program_id(0)
    xa = jnp.concatenate([x_ref[...], xp_ref[...], xn_ref[...]], axis=0)
    ua = _dot(_rms(xa, g_ref[...]).astype(BF16), w_ref[...])
    u = ua[0:tm, :]
    start = i * tm
    first = (start % s_len) == 0
    last = ((start + tm) % s_len) == 0
    prev_row = jnp.where(first, 0.0, ua[tm + SUBLANES - 1:tm + SUBLANES, :])
    next_row = jnp.where(last, 0.0, ua[tm + SUBLANES:tm + SUBLANES + 1, :])
    row = lax.broadcasted_iota(jnp.int32, (tm, 1), 0)
    prev = jnp.where(row == 0, prev_row, pltpu.roll(u, 1, 0))
    nxt = jnp.where(row == tm - 1, next_row, pltpu.roll(u, tm - 1, 0))
    c = prev * conv_ref[0:1, :] + u * conv_ref[1:2, :] + nxt * conv_ref[2:3, :]

    o_w = 3 * RW_C
    o_a = o_w + 2 * LORA_W
    o_g = o_a + 2 * LORA_A
    r = c[:, 0:RW_C]
    k = c[:, RW_C:2 * RW_C]
    v = c[:, 2 * RW_C:3 * RW_C]
    tw = jnp.tanh(c[:, o_w:o_a]).astype(BF16)
    ad = c[:, o_a:o_g].astype(BF16)
    gd = c[:, o_g:RW_COLS]
    bd = bd_ref[...]

    kkv = k * kk_ref[...]
    norm = jnp.sqrt(_group_sum(kkv * kkv, bd))
    kk = kkv / jnp.maximum(norm, 1e-12)
    g = _dot(jax.nn.sigmoid(gd).astype(BF16), gup_ref[...])

    ksum = jnp.zeros_like(k)
    for d in range(2):
        lw = -DECAY_SCALE * jax.nn.sigmoid(w0_ref[d:d + 1, :] + _dot(tw, wup_ref[d]))
        a = jax.nn.sigmoid(a0_ref[d:d + 1, :] + _dot(ad, aup_ref[d]))
        kd = k * (1.0 + (a - 1.0) * ka_ref[...])
        lw_o[d] = lw
        kd_o[d] = kd
        b_o[d] = kk * a
        ksum = ksum + kd
    coef = _group_sum(r * ksum * rk_ref[...], bd)
    r_o[...] = r
    v_o[...] = v
    kk_o[...] = kk
    g_o[...] = g
    bonus_o[...] = coef * v


def _pad_dir_rows(w_up):
    z = jnp.zeros_like(w_up[0])
    return jnp.stack([jnp.concatenate([w_up[0], z], 0), jnp.concatenate([z, w_up[1]], 0)]).astype(BF16)


def _rwprep(x1, s_len, norm_g, w_rw, conv_w, w0, w_up, a0, a_up, g_up, k_k, k_a, r_k, bd):
    t = x1.shape[0]
    tm = min(PREP_TILE, s_len)
    nb8 = t // SUBLANES
    per = tm // SUBLANES
    const2 = lambda i: (0, 0)
    const3 = lambda i: (0, 0, 0)
    tok = pl.BlockSpec((tm, RW_C), lambda i: (i, 0))
    tok2 = pl.BlockSpec((2, tm, RW_C), lambda i: (0, i, 0))
    vec = pl.BlockSpec((1, RW_C), const2)
    one = jax.ShapeDtypeStruct((t, RW_C), F32)
    two = jax.ShapeDtypeStruct((2, t, RW_C), F32)
    return pl.pallas_call(
        functools.partial(_rwprep_kernel, tm=tm, s_len=s_len),
        grid=(t // tm,),
        in_specs=[
            pl.BlockSpec((tm, D_MODEL), lambda i: (i, 0)),
            pl.BlockSpec((SUBLANES, D_MODEL), lambda i: (jnp.maximum(i * per - 1, 0), 0)),
            pl.BlockSpec((SUBLANES, D_MODEL), lambda i: (jnp.minimum((i + 1) * per, nb8 - 1), 0)),
            pl.BlockSpec((1, D_MODEL), const2),
            pl.BlockSpec((D_MODEL, RW_COLS), const2),
            pl.BlockSpec((3, RW_COLS), const2),
            pl.BlockSpec((2, RW_C), const2),
            pl.BlockSpec((2, 2 * LORA_W, RW_C), const3),
            pl.BlockSpec((2, RW_C), const2),
            pl.BlockSpec((2, 2 * LORA_A, RW_C), const3),
            pl.BlockSpec((LORA_G, RW_C), const2),
            vec, vec, vec,
            pl.BlockSpec((RW_C, RW_C), const2),
        ],
        out_specs=[tok, tok, tok, tok, tok, tok2, tok2, tok2],
        out_shape=[one, one, one, one, one, two, two, two],
        compiler_params=_params(("parallel",), VMEM_WEIGHTS_RESIDENT),
        name="rwprep",
    )(x1, x1, x1, norm_g.reshape(1, D_MODEL), w_rw.astype(BF16), conv_w, w0, _pad_dir_rows(w_up), a0, _pad_dir_rows(a_up),
      g_up.astype(BF16), k_k.reshape(1, RW_C), k_a.reshape(1, RW_C), r_k.reshape(1, RW_C), bd)


def _scan_kernel(rf_ref, vf_ref, kkf_ref, lwf_ref, kdf_ref, bf_ref, rr_ref, vr_ref, kkr_ref, lwr_ref, kdr_ref, br_ref,
                 yf_ref, yr_ref, h_ref, *, chunk):
    L = chunk
    L2 = 2 * L
    PW = 2 * RW_N
    n_pairs = RW_C // PW
    n_sub = lwf_ref.shape[0] // L
    assert L2 == PW, "interaction matrices and lane pairs share the 128-lane tile"

    @pl.when(pl.program_id(1) == 0)
    def _():
        h_ref[...] = jnp.zeros_like(h_ref)

    row = lax.broadcasted_iota(jnp.int32, (L, L), 0)
    col = lax.broadcasted_iota(jnp.int32, (L, L), 1)
    row2 = lax.broadcasted_iota(jnp.int32, (L, L2), 0)
    col2 = lax.broadcasted_iota(jnp.int32, (L, L2), 1) & (L - 1)
    eye2 = (row2 == col2).astype(F32)
    first_head = lax.broadcasted_iota(jnp.int32, (L, PW), 1) < RW_N

    def stack(z):
        zero = jnp.zeros_like(z)
        return jnp.concatenate([jnp.where(first_head, z, zero), jnp.where(first_head, zero, z)], axis=0)

    chains = []
    dirs = ((rf_ref, vf_ref, kkf_ref, lwf_ref, kdf_ref, bf_ref, yf_ref, 1),
            (rr_ref, vr_ref, kkr_ref, lwr_ref, kdr_ref, br_ref, yr_ref, -1))
    for d, (r_ref, v_ref, kk_ref, lw_ref, kd_ref, b_ref, y_ref, sgn) in enumerate(dirs):
        incl2 = (row2 - col2) * sgn >= 0
        strict2 = (row2 - col2) * sgn > 0
        tri = ((row - col) * sgn >= 0).astype(BF16)
        for step, j in enumerate(range(n_sub) if sgn > 0 else reversed(range(n_sub))):
            rows = slice(j * L, (j + 1) * L)
            lw = lw_ref[rows, :]
            hi, mid, lo = _split3(lw)
            c_in = _dot(tri, hi) + _dot(tri, mid) + _dot(tri, lo)
            c_tot = jnp.sum(lw, axis=0, keepdims=True)
            e_neg = jnp.exp(-c_in)
            e_end = jnp.exp(c_tot - c_in)
            kk = kk_ref[rows, :]
            b = b_ref[rows, :]
            kd = kd_ref[rows, :]
            a_t = -kk * jnp.exp(c_in - lw)
            r_t = r_ref[rows, :] * jnp.exp(c_in)
            b_t = b * e_neg
            k_t = kd * e_neg
            b_p = b * e_end
            k_p = kd * e_end
            p_end = jnp.exp(c_tot)
            v = v_ref[rows, :]
            for p in range(n_pairs):
                ls = slice(p * PW, (p + 1) * PW)
                a_n = a_t[:, ls].astype(BF16)
                chains.append(dict(
                    d=d, p=p, step=step, rows=rows, ls=ls, y_ref=y_ref, incl=incl2, strict=strict2,
                    ar=jnp.concatenate([a_n, r_t[:, ls].astype(BF16)], axis=0),
                    bk=jnp.concatenate([stack(b_t[:, ls].astype(BF16)), stack(k_t[:, ls].astype(BF16))], axis=0),
                    aa=stack(a_n), vv=stack(v[:, ls].astype(BF16)),
                    bpT=jnp.transpose(stack(b_p[:, ls])).astype(BF16),
                    kpT=jnp.transpose(stack(k_p[:, ls])).astype(BF16),
                    pe_col=jnp.transpose(jnp.broadcast_to(p_end[:, ls], (PW, PW)))))

    for c in chains:
        x = _dot_nt(c["ar"], c["bk"])
        c["a_ab"] = jnp.where(c["strict"], x[:L, :L2], 0.0)
        c["a_ak"] = jnp.where(c["strict"], x[:L, L2:], 0.0).astype(BF16)
        c["a_rb"] = jnp.where(c["incl"], x[L:, :L2], 0.0).astype(BF16)
        c["a_rk"] = jnp.where(c["incl"], x[L:, L2:], 0.0).astype(BF16)
        c["inv"] = eye2 + c["a_ab"]
        c["pw"] = c["a_ab"].astype(BF16)
    n_sq = int(math.log2(L)) - 1
    for k in range(n_sq + 1):
        for c in chains:
            pws = stack(c["pw"])
            if k == 0:
                c["pw"] = _dot(c["pw"], pws).astype(BF16)
            elif k < n_sq:
                both = _dot(jnp.concatenate([c["pw"], c["inv"].astype(BF16)], axis=0), pws)
                c["pw"] = both[:L, :].astype(BF16)
                c["inv"] = c["inv"] + both[L:, :]
            else:
                c["inv"] = c["inv"] + _dot(c["inv"].astype(BF16), pws)
    for c in chains:
        c["akv"] = stack(_dot(c["a_ak"], c["vv"]).astype(BF16))
    for c in chains:
        inv = c["inv"].astype(BF16)
        c["wi"] = jnp.concatenate([_dot(inv, c["aa"]).astype(BF16), inv], axis=1)
        c["bkT"] = jnp.concatenate([c["bpT"], c["kpT"]], axis=1)
        c["rab"] = jnp.concatenate([c["ar"][L:, :], c["a_rb"], c["a_rk"]], axis=1)
    state = {(d, p): h_ref[d, p] for d in range(2) for p in range(n_pairs)}
    for step in range(n_sub):
        now = [c for c in chains if c["step"] == step]
        for c in now:
            c["h"] = state[c["d"], c["p"]]
            c["hb"] = c["h"].astype(BF16)
            u = _dot(c["wi"], jnp.concatenate([c["hb"], c["akv"]], axis=0))
            c["uv"] = jnp.concatenate([stack(u.astype(BF16)), c["vv"]], axis=0)
        for c in now:
            state[c["d"], c["p"]] = c["pe_col"] * c["h"] + _dot(c["bkT"], c["uv"])
        for c in now:
            c["y_ref"][c["rows"], c["ls"]] = _dot(c["rab"], jnp.concatenate([c["hb"], c["uv"]], axis=0))
    for (d, p), h in state.items():
        h_ref[d, p] = h


def _scan(r, v, kk, lw, kd, b, bsz, s_len):
    rows = CHUNK * SCAN_CHUNKS_PER_STEP
    nc = s_len // rows
    shp = lambda a: a.reshape(bsz, s_len, RW_C)
    shp2 = lambda a: a.reshape(2, bsz, s_len, RW_C)
    f_sh = pl.BlockSpec((None, rows, RW_C), lambda bi, c: (bi, c, 0))
    r_sh = pl.BlockSpec((None, rows, RW_C), lambda bi, c: (bi, nc - 1 - c, 0))
    f_pd = pl.BlockSpec((None, None, rows, RW_C), lambda bi, c: (0, bi, c, 0))
    r_pd = pl.BlockSpec((None, None, rows, RW_C), lambda bi, c: (1, bi, nc - 1 - c, 0))
    out = jax.ShapeDtypeStruct((bsz, s_len, RW_C), F32)
    r3, v3, kk3, lw4, kd4, b4 = shp(r), shp(v), shp(kk), shp2(lw), shp2(kd), shp2(b)
    yf, yr = pl.pallas_call(
        functools.partial(_scan_kernel, chunk=CHUNK),
        grid=(bsz, nc),
        in_specs=[f_sh, f_sh, f_sh, f_pd, f_pd, f_pd, r_sh, r_sh, r_sh, r_pd, r_pd, r_pd],
        out_specs=[f_sh, r_sh],
        out_shape=[out, out],
        scratch_shapes=[pltpu.VMEM((2, RW_C // (2 * RW_N), 2 * RW_N, 2 * RW_N), F32)],
        compiler_params=_params(("parallel", "arbitrary"), VMEM_TILES_ONLY),
        name="scan",
    )(r3, v3, kk3, lw4, kd4, b4, r3, v3, kk3, lw4, kd4, b4)
    return yf.reshape(bsz * s_len, RW_C), yr.reshape(bsz * s_len, RW_C)


def _daprep_kernel(x_ref, g_ref, w_ref, cf_ref, sn_ref, sp_ref, qn_ref, kn_ref, bd_ref, q_o, k_o, v_o):
    bd = bd_ref[...]
    cf, sn, sp = cf_ref[...], sn_ref[...], sp_ref[...]
    p = _dot(_rms(x_ref[...], g_ref[...]).astype(BF16), w_ref[...])

    def norm_rope(x, g):
        ms = _group_sum(x * x, bd) * (1.0 / DA_DH)
        xn = x * lax.rsqrt(ms + NORM_EPS) * g
        half = ROT_DIM // 2
        return xn * cf + pltpu.roll(xn, DA_C - half, 1) * sn + pltpu.roll(xn, half, 1) * sp

    q = norm_rope(p[:, 0:DA_C], qn_ref[...]) * (DA_DH ** -0.5 * LOG2E)
    k = norm_rope(p[:, DA_C:2 * DA_C], kn_ref[...])
    q_o[...] = q.astype(BF16)
    k_o[...] = k.astype(BF16)
    vt = jnp.transpose(p[:, 2 * DA_C:3 * DA_C]).astype(BF16)
    hw = 2 * DA_DH
    for h in range(DA_HEADS):
        v_o[h, 0:hw, :] = vt[h * hw:(h + 1) * hw, :]
        v_o[h, hw:hw + ONES_ROWS, :] = jnp.ones((ONES_ROWS, vt.shape[1]), BF16)


def _rope_lane_tables(s_len):
    half = ROT_DIM // 2
    inv_freq = ROPE_THETA ** (-jnp.arange(0, ROT_DIM, 2, dtype=F32) / ROT_DIM)
    ang = jnp.arange(s_len, dtype=F32)[:, None] * inv_freq[None, :]
    cos, sin = jnp.cos(ang), jnp.sin(ang)
    j = jnp.arange(DA_C) % DA_DH
    f = j % half
    cf = jnp.where(j < ROT_DIM, cos[:, f], 1.0)
    sn = jnp.where(j < half, -sin[:, f], 0.0)
    sp = jnp.where((j >= half) & (j < ROT_DIM), sin[:, f], 0.0)
    return cf, sn, sp


def _daprep(x1, bsz, s_len, norm_g, w_da, q_norm, k_norm, bd):
    t = x1.shape[0]
    tm = min(ATT_TK, s_len)
    per_seq = s_len // tm
    cf, sn, sp = _rope_lane_tables(s_len)
    const2 = lambda i: (0, 0)
    tok = pl.BlockSpec((tm, DA_C), lambda i: (i, 0))
    tab = pl.BlockSpec((tm, DA_C), lambda i: (i % per_seq, 0))
    vec = pl.BlockSpec((1, DA_C), const2)
    out = jax.ShapeDtypeStruct((t, DA_C), BF16)
    reps = DA_C // DA_DH
    return pl.pallas_call(
        _daprep_kernel,
        grid=(t // tm,),
        in_specs=[pl.BlockSpec((tm, D_MODEL), lambda i: (i, 0)), pl.BlockSpec((1, D_MODEL), const2),
                  pl.BlockSpec((D_MODEL, 3 * DA_C), const2), tab, tab, tab, vec, vec,
                  pl.BlockSpec((DA_C, DA_C), const2)],
        out_specs=[tok, tok, pl.BlockSpec((None, None, DA_HEADS, VT_ROWS, tm),
                                          lambda i: (i // per_seq, i % per_seq, 0, 0, 0))],
        out_shape=[out, out, jax.ShapeDtypeStruct((bsz, per_seq, DA_HEADS, VT_ROWS, tm), BF16)],
        compiler_params=_params(("parallel",), VMEM_WEIGHTS_RESIDENT),
        name="daprep",
    )(x1, norm_g.reshape(1, D_MODEL), w_da.astype(BF16), cf, sn, sp,
      jnp.tile(q_norm, reps).reshape(1, DA_C), jnp.tile(k_norm, reps).reshape(1, DA_C), bd)


def _attn_kernel(q_ref, k_ref, vt_ref, lq1_ref, lk1_ref, lq2_ref, lk2_ref, sub_ref, o_ref,
                 acc_ref, sa_ref, sb_ref, pa_ref, pb_ref, *, tq, tk):
    q = q_ref[...]
    lane = lax.broadcasted_iota(jnp.int32, q.shape, 1)
    zero = jnp.zeros_like(q)
    qs = jnp.concatenate([jnp.where(lane < DA_DH, q, zero), jnp.where(lane >= DA_DH, q, zero)], axis=0)
    nk = k_ref.shape[0] // tk

    def scores(i):
        return _dot_nt(k_ref[i * tk:(i + 1) * tk, :], qs)

    def scores_into(s_ref, i):
        s = scores(i)
        s_ref[...] = s.astype(BF16)
        return jnp.max(s, axis=0, keepdims=True)

    def softmax(s_ref, p_ref, m_prev, m_blk):
        m_new = jnp.maximum(m_prev, m_blk.astype(BF16).astype(F32))
        p_ref[...] = jnp.exp2(s_ref[...] - m_new.astype(BF16))
        return m_new, jnp.exp2(m_prev - m_new)

    def accumulate(p_ref, i, corr):
        pv = _dot(vt_ref[i], p_ref[...])
        acc_ref[...] = pv if i == 0 else corr * acc_ref[...] + pv

    s_refs = (sa_ref, sb_ref)
    p_refs = (pa_ref, pb_ref)
    m_blk = scores_into(s_refs[0], 0)
    m = jnp.full((1, 2 * tq), -jnp.inf, F32)
    corr = None
    for i in range(nk):
        m_next = scores_into(s_refs[(i + 1) % 2], i + 1) if i + 1 < nk else None
        if i > 0:
            accumulate(p_refs[(i - 1) % 2], i - 1, corr)
        m, corr = softmax(s_refs[i % 2], p_refs[i % 2], m, m_blk)
        m_blk = m_next
    accumulate(p_refs[(nk - 1) % 2], nk - 1, corr)

    hw = 2 * DA_DH
    o2 = acc_ref[0:hw, :] / acc_ref[hw:hw + 1, :]
    lam = (jnp.exp(jnp.sum(lq1_ref[...] * lk1_ref[...], axis=1, keepdims=True))
           - jnp.exp(jnp.sum(lq2_ref[...] * lk2_ref[...], axis=1, keepdims=True))) + LAM_INIT
    o = jnp.transpose(o2[:, 0:tq] - lam * o2[:, tq:2 * tq])
    o_ref[...] = _rms(o, sub_ref[...]) * (1.0 - LAM_INIT)


def _attn(q, k, vt, bsz, s_len, lq1, lk1, lq2, lk2, subln):
    tq = min(max(ATT_TQ, ATT_STEP_WORK // s_len), 2 * ATT_TQ, s_len)
    tk = min(ATT_TK, s_len)
    nk = s_len // tk
    hw = 2 * DA_DH
    shp = lambda a: a.reshape(bsz, s_len, DA_C)
    lvec = pl.BlockSpec((1, DA_DH), lambda b, h, qi: (0, 0))
    o = pl.pallas_call(
        functools.partial(_attn_kernel, tq=tq, tk=tk),
        grid=(bsz, DA_HEADS, s_len // tq),
        in_specs=[
            pl.BlockSpec((None, tq, hw), lambda b, h, qi: (b, qi, h)),
            pl.BlockSpec((None, s_len, hw), lambda b, h, qi: (b, 0, h)),
            pl.BlockSpec((None, nk, None, VT_ROWS, tk), lambda b, h, qi: (b, 0, h, 0, 0)),
            lvec, lvec, lvec, lvec,
            pl.BlockSpec((1, hw), lambda b, h, qi: (0, 0)),
        ],
        out_specs=pl.BlockSpec((None, tq, hw), lambda b, h, qi: (b, qi, h)),
        out_shape=jax.ShapeDtypeStruct((bsz, s_len, DA_C), F32),
        scratch_shapes=[pltpu.VMEM((VT_ROWS, 2 * tq), F32),
                        pltpu.VMEM((tk, 2 * tq), BF16), pltpu.VMEM((tk, 2 * tq), BF16),
                        pltpu.VMEM((tk, 2 * tq), BF16), pltpu.VMEM((tk, 2 * tq), BF16)],
        compiler_params=_params(("parallel", "parallel", "arbitrary"), VMEM_TILES_ONLY),
        name="attn",
    )(shp(q), shp(k), vt, lq1.reshape(1, DA_DH), lk1.reshape(1, DA_DH), lq2.reshape(1, DA_DH),
      lk2.reshape(1, DA_DH), subln.reshape(1, hw))
    return o.reshape(bsz * s_len, DA_C)


def _mixout_ffn_kernel(x_ref, yf_ref, yr_ref, g_ref, bonus_ref, da_ref, lng_ref, lnb_ref, bd_ref, wout_ref,
                       fg_ref, wgu_ref, wd_ref, final_ref, o_ref):
    bd = bd_ref[...]
    y = yf_ref[...] + yr_ref[...]
    mu = _group_sum(y, bd) * (1.0 / RW_N)
    yc = y - mu
    var = _group_sum(yc * yc, bd) * (1.0 / RW_N)
    yn = yc * lax.rsqrt(var + GN_EPS) * lng_ref[...] + lnb_ref[...]
    y_rw = ((yn + bonus_ref[...]) * g_ref[...]).astype(BF16)
    mixed = _dot(y_rw, wout_ref[0:RW_C, :]) + _dot(da_ref[...].astype(BF16), wout_ref[RW_C:RW_C + DA_C, :])
    x2 = x_ref[...] + mixed
    o_ref[...] = _rms(_half_swiglu(x2, fg_ref, wgu_ref, wd_ref), final_ref[...])


def _mixout_ffn(x1, yf, yr, g, bonus, o_da, ln_g, ln_b, bd, w_out, norm_g, w_gu, w_down, final_g):
    t = x1.shape[0]
    tm = min(TOKEN_TILE, t)
    tok = pl.BlockSpec((tm, RW_C), lambda i: (i, 0))
    return pl.pallas_call(
        _mixout_ffn_kernel,
        grid=(t // tm,),
        in_specs=[
            pl.BlockSpec((tm, D_MODEL), lambda i: (i, 0)),
            tok, tok, tok, tok, tok,
            _resident((1, RW_C)), _resident((1, RW_C)), _resident((RW_C, RW_C)), _resident((RW_C + DA_C, D_MODEL)),
            _resident((1, D_MODEL)), _resident((D_MODEL, 2 * D_FF)), _resident((D_FF, D_MODEL)),
            _resident((1, D_MODEL)),
        ],
        out_specs=pl.BlockSpec((tm, D_MODEL), lambda i: (i, 0)),
        out_shape=jax.ShapeDtypeStruct((t, D_MODEL), F32),
        compiler_params=_params(("parallel",), VMEM_WEIGHTS_RESIDENT),
        name="mixout_ffn",
    )(x1, yf, yr, g, bonus, o_da, ln_g.reshape(1, RW_C), ln_b.reshape(1, RW_C), bd, w_out.astype(BF16),
      norm_g.reshape(1, D_MODEL), w_gu.astype(BF16), w_down.astype(BF16), final_g.reshape(1, D_MODEL))


def _encoder_layer(x, p):
    bsz, s_len, _ = x.shape
    t = bsz * s_len
    xf = x.reshape(t, D_MODEL)
    lane = jnp.arange(RW_C) // RW_N
    bd = (lane[:, None] == lane[None, :]).astype(BF16)

    x1 = _ffn(xf, p["ffn1_norm"], p["ffn1_w_gu"], p["ffn1_w_down"])
    r, v, kk, g, bonus, lw, kd, b = _rwprep(
        x1, s_len, p["mix_norm"], p["w_in"][:, :RW_COLS], p["conv_w"], p["rw_w0"], p["rw_w_up"], p["rw_a0"],
        p["rw_a_up"], p["rw_g_up"], p["rw_k_k"], p["rw_k_a"], p["rw_r_k"], bd)
    yf, yr = _scan(r, v, kk, lw, kd, b, bsz, s_len)
    q, k, va = _daprep(x1, bsz, s_len, p["mix_norm"], p["w_in"][:, RW_COLS:], p["da_q_norm"], p["da_k_norm"], bd)
    o_da = _attn(q, k, va, bsz, s_len, p["da_lq1"], p["da_lk1"], p["da_lq2"], p["da_lk2"], p["da_subln"])
    out = _mixout_ffn(x1, yf, yr, g, bonus, o_da, p["rw_ln_g"], p["rw_ln_b"], bd, p["w_out"],
                      p["ffn2_norm"], p["ffn2_w_gu"], p["ffn2_w_down"], p["final_norm"])
    return out.reshape(bsz, s_len, D_MODEL)


def kernel(x_prompt, x_sample, ffn1_norm, ffn1_w_gu, ffn1_w_down, mix_norm, w_in, conv_w, rw_w0, rw_w_up, rw_a0, rw_a_up, rw_g_up, rw_k_k, rw_k_a, rw_r_k, rw_ln_g, rw_ln_b, da_q_norm, da_k_norm, da_lq1, da_lk1, da_lq2, da_lk2, da_subln, w_out, ffn2_norm, ffn2_w_gu, ffn2_w_down, final_norm):
    names = ("ffn1_norm", "ffn1_w_gu", "ffn1_w_down", "mix_norm", "w_in", "conv_w", "rw_w0", "rw_w_up", "rw_a0",
             "rw_a_up", "rw_g_up", "rw_k_k", "rw_k_a", "rw_r_k", "rw_ln_g", "rw_ln_b", "da_q_norm", "da_k_norm",
             "da_lq1", "da_lk1", "da_lq2", "da_lk2", "da_subln", "w_out", "ffn2_norm", "ffn2_w_gu", "ffn2_w_down",
             "final_norm")
    vals = (ffn1_norm, ffn1_w_gu, ffn1_w_down, mix_norm, w_in, conv_w, rw_w0, rw_w_up, rw_a0, rw_a_up, rw_g_up,
            rw_k_k, rw_k_a, rw_r_k, rw_ln_g, rw_ln_b, da_q_norm, da_k_norm, da_lq1, da_lk1, da_lq2, da_lk2,
            da_subln, w_out, ffn2_norm, ffn2_w_gu, ffn2_w_down, final_norm)
    assert all(a.shape[0] == 1 for a in vals), "single-layer (depth 1) parameter stacks expected"
    p = {n: a[0] for n, a in zip(names, vals)}
    return (_encoder_layer(x_prompt, p), _encoder_layer(x_sample, p))
```

```python
import functools
import math

import jax
import jax.numpy as jnp
from jax import lax
from jax.experimental import pallas as pl
from jax.experimental.pallas import tpu as pltpu

F32 = jnp.float32
BF16 = jnp.bfloat16

D_MODEL = 1024
D_FF = 2816
RW_HEADS = 8
RW_N = 64
RW_C = RW_HEADS * RW_N
LORA_W = 64
LORA_A = 64
LORA_G = 128
DECAY_SCALE = 0.606531
GN_EPS = 64e-5
DA_HEADS = 4
DA_DH = 64
DA_C = DA_HEADS * 2 * DA_DH
ROT_DIM = DA_DH // 4
ROPE_THETA = 500000.0
NORM_EPS = 1e-6
RW_COLS = 3 * RW_C + 2 * LORA_W + 2 * LORA_A + LORA_G
LAM_INIT = 0.8 - 0.6 * math.exp(-0.3 * 0)
LOG2E = math.log2(math.e)

V7X_VMEM_BYTES = 64 * 1024 * 1024
SUBLANES = 8
VMEM_WEIGHTS_RESIDENT = V7X_VMEM_BYTES * 7 // 8
VMEM_TILES_ONLY = V7X_VMEM_BYTES // 2

FF_CHUNK = 256
N_FF_CHUNKS = D_FF // FF_CHUNK
TOKEN_TILE = 512
PREP_TILE = 512
CHUNK = 64
SCAN_CHUNKS_PER_STEP = 4
ATT_TQ = 512
ATT_STEP_WORK = 4 * 1024 * 1024
ONES_ROWS = 16
VT_ROWS = 2 * DA_DH + ONES_ROWS
ATT_TK = 512


def _params(sem, vmem_bytes):
    return pltpu.CompilerParams(dimension_semantics=sem, vmem_limit_bytes=vmem_bytes)


def _dot(a, b):
    return jnp.dot(a, b, preferred_element_type=F32)


def _dot_nt(a, b):
    return lax.dot_general(a, b, (((1,), (1,)), ((), ())), preferred_element_type=F32)


def _split2(x):
    hi = x.astype(BF16)
    lo = (x - hi.astype(F32)).astype(BF16)
    return hi, lo


def _split3(x):
    hi = x.astype(BF16)
    r1 = x - hi.astype(F32)
    mid = r1.astype(BF16)
    lo = (r1 - mid.astype(F32)).astype(BF16)
    return hi, mid, lo


def _group_sum(x, bd):
    hi, lo = _split2(x)
    return _dot(hi, bd) + _dot(lo, bd)


def _rms(x, g, eps=NORM_EPS):
    return x * lax.rsqrt(jnp.mean(x * x, axis=-1, keepdims=True) + eps) * g


def _half_swiglu(x, g_ref, wgu_ref, wd_ref):
    h = _rms(x, g_ref[...]).astype(BF16)
    acc = jnp.zeros_like(x)
    for j in range(N_FF_CHUNKS):
        lo = j * FF_CHUNK
        gate = _dot(h, wgu_ref[:, lo:lo + FF_CHUNK])
        up = _dot(h, wgu_ref[:, D_FF + lo:D_FF + lo + FF_CHUNK])
        act = (gate * jax.nn.sigmoid(gate) * up).astype(BF16)
        acc = acc + _dot(act, wd_ref[lo:lo + FF_CHUNK, :])
    return x + 0.5 * acc


def _ffn_kernel(x_ref, g_ref, wgu_ref, wd_ref, o_ref):
    o_ref[...] = _half_swiglu(x_ref[...], g_ref, wgu_ref, wd_ref)


def _resident(shape):
    return pl.BlockSpec(shape, lambda i: (0,) * len(shape), pipeline_mode=pl.Buffered(1))


def _ffn(x, norm_g, w_gu, w_down):
    t = x.shape[0]
    tm = min(TOKEN_TILE, t)
    return pl.pallas_call(
        _ffn_kernel,
        grid=(t // tm,),
        in_specs=[
            pl.BlockSpec((tm, D_MODEL), lambda i: (i, 0)),
            _resident((1, D_MODEL)),
            _resident((D_MODEL, 2 * D_FF)),
            _resident((D_FF, D_MODEL)),
        ],
        out_specs=pl.BlockSpec((tm, D_MODEL), lambda i: (i, 0)),
        out_shape=jax.ShapeDtypeStruct((t, D_MODEL), F32),
        compiler_params=_params(("parallel",), VMEM_WEIGHTS_RESIDENT),
        name="ffn",
    )(x, norm_g.reshape(1, D_MODEL), w_gu.astype(BF16), w_down.astype(BF16))


def _rwprep_kernel(x_ref, xp_ref, xn_ref, g_ref, w_ref, conv_ref, w0_ref, wup_ref, a0_ref, aup_ref, gup_ref,
                   kk_ref, ka_ref, rk_ref, bd_ref,
                   r_o, v_o, kk_o, g_o, bonus_o, lw_o, kd_o, b_o, *, tm, s_len):
    i = pl.program_id(0)
    xa = jnp.concatenate([x_ref[...], xp_ref[...], xn_ref[...]], axis=0)
    ua = _dot(_rms(xa, g_ref[...]).astype(BF16), w_ref[...])
    u = ua[0:tm, :]
    start = i * tm
    first = (start % s_len) == 0
    last = ((start + tm) % s_len) == 0
    prev_row = jnp.where(first, 0.0, ua[tm + SUBLANES - 1:tm + SUBLANES, :])
    next_row = jnp.where(last, 0.0, ua[tm + SUBLANES:tm + SUBLANES + 1, :])
    row = lax.broadcasted_iota(jnp.int32, (tm, 1), 0)
    prev = jnp.where(row == 0, prev_row, pltpu.roll(u, 1, 0))
    nxt = jnp.where(row == tm - 1, next_row, pltpu.roll(u, tm - 1, 0))
    c = prev * conv_ref[0:1, :] + u * conv_ref[1:2, :] + nxt * conv_ref[2:3, :]

    o_w = 3 * RW_C
    o_a = o_w + 2 * LORA_W
    o_g = o_a + 2 * LORA_A
    r = c[:, 0:RW_C]
    k = c[:, RW_C:2 * RW_C]
    v = c[:, 2 * RW_C:3 * RW_C]
    tw = jnp.tanh(c[:, o_w:o_a]).astype(BF16)
    ad = c[:, o_a:o_g].astype(BF16)
    gd = c[:, o_g:RW_COLS]
    bd = bd_ref[...]

    kkv = k * kk_ref[...]
    norm = jnp.sqrt(_group_sum(kkv * kkv, bd))
    kk = kkv / jnp.maximum(norm, 1e-12)
    g = _dot(jax.nn.sigmoid(gd).astype(BF16), gup_ref[...])

    ksum = jnp.zeros_like(k)
    for d in range(2):
        lw = -DECAY_SCALE * jax.nn.sigmoid(w0_ref[d:d + 1, :] + _dot(tw, wup_ref[d]))
        a = jax.nn.sigmoid(a0_ref[d:d + 1, :] + _dot(ad, aup_ref[d]))
        kd = k * (1.0 + (a - 1.0) * ka_ref[...])
        lw_o[d] = lw
        kd_o[d] = kd
        b_o[d] = kk * a
        ksum = ksum + kd
    coef = _group_sum(r * ksum * rk_ref[...], bd)
    r_o[...] = r
    v_o[...] = v
    kk_o[...] = kk
    g_o[...] = g
    bonus_o[...] = coef * v


def _pad_dir_rows(w_up):
    z = jnp.zeros_like(w_up[0])
    return jnp.stack([jnp.concatenate([w_up[0], z], 0), jnp.concatenate([z, w_up[1]], 0)]).astype(BF16)


def _rwprep(x1, s_len, norm_g, w_rw, conv_w, w0, w_up, a0, a_up, g_up, k_k, k_a, r_k, bd):
    t = x1.shape[0]
    tm = min(PREP_TILE, s_len)
    nb8 = t // SUBLANES
    per = tm // SUBLANES
    const2 = lambda i: (0, 0)
    const3 = lambda i: (0, 0, 0)
    tok = pl.BlockSpec((tm, RW_C), lambda i: (i, 0))
    tok2 = pl.BlockSpec((2, tm, RW_C), lambda i: (0, i, 0))
    vec = pl.BlockSpec((1, RW_C), const2)
    one = jax.ShapeDtypeStruct((t, RW_C), F32)
    two = jax.ShapeDtypeStruct((2, t, RW_C), F32)
    return pl.pallas_call(
        functools.partial(_rwprep_kernel, tm=tm, s_len=s_len),
        grid=(t // tm,),
        in_specs=[
            pl.BlockSpec((tm, D_MODEL), lambda i: (i, 0)),
            pl.BlockSpec((SUBLANES, D_MODEL), lambda i: (jnp.maximum(i * per - 1, 0), 0)),
            pl.BlockSpec((SUBLANES, D_MODEL), lambda i: (jnp.minimum((i + 1) * per, nb8 - 1), 0)),
            pl.BlockSpec((1, D_MODEL), const2),
            pl.BlockSpec((D_MODEL, RW_COLS), const2),
            pl.BlockSpec((3, RW_COLS), const2),
            pl.BlockSpec((2, RW_C), const2),
            pl.BlockSpec((2, 2 * LORA_W, RW_C), const3),
            pl.BlockSpec((2, RW_C), const2),
            pl.BlockSpec((2, 2 * LORA_A, RW_C), const3),
            pl.BlockSpec((LORA_G, RW_C), const2),
            vec, vec, vec,
            pl.BlockSpec((RW_C, RW_C), const2),
        ],
        out_specs=[tok, tok, tok, tok, tok, tok2, tok2, tok2],
        out_shape=[one, one, one, one, one, two, two, two],
        compiler_params=_params(("parallel",), VMEM_WEIGHTS_RESIDENT),
        name="rwprep",
    )(x1, x1, x1, norm_g.reshape(1, D_MODEL), w_rw.astype(BF16), conv_w, w0, _pad_dir_rows(w_up), a0, _pad_dir_rows(a_up),
      g_up.astype(BF16), k_k.reshape(1, RW_C), k_a.reshape(1, RW_C), r_k.reshape(1, RW_C), bd)


def _scan_kernel(rf_ref, vf_ref, kkf_ref, lwf_ref, kdf_ref, bf_ref, rr_ref, vr_ref, kkr_ref, lwr_ref, kdr_ref, br_ref,
                 yf_ref, yr_ref, h_ref, *, chunk):
    L = chunk
    L2 = 2 * L
    PW = 2 * RW_N
    n_pairs = RW_C // PW
    n_sub = lwf_ref.shape[0] // L
    assert L2 == PW, "interaction matrices and lane pairs share the 128-lane tile"

    @pl.when(pl.program_id(1) == 0)
    def _():
        h_ref[...] = jnp.zeros_like(h_ref)

    row = lax.broadcasted_iota(jnp.int32, (L, L), 0)
    col = lax.broadcasted_iota(jnp.int32, (L, L), 1)
    row2 = lax.broadcasted_iota(jnp.int32, (L, L2), 0)
    col2 = lax.broadcasted_iota(jnp.int32, (L, L2), 1) & (L - 1)
    eye2 = (row2 == col2).astype(F32)
    first_head = lax.broadcasted_iota(jnp.int32, (L, PW), 1) < RW_N

    def stack(z):
        zero = jnp.zeros_like(z)
        return jnp.concatenate([jnp.where(first_head, z, zero), jnp.where(first_head, zero, z)], axis=0)

    chains = []
    dirs = ((rf_ref, vf_ref, kkf_ref, lwf_ref, kdf_ref, bf_ref, yf_ref, 1),
            (rr_ref, vr_ref, kkr_ref, lwr_ref, kdr_ref, br_ref, yr_ref, -1))
    for d, (r_ref, v_ref, kk_ref, lw_ref, kd_ref, b_ref, y_ref, sgn) in enumerate(dirs):
        incl2 = (row2 - col2) * sgn >= 0
        strict2 = (row2 - col2) * sgn > 0
        tri = ((row - col) * sgn >= 0).astype(BF16)
        for step, j in enumerate(range(n_sub) if sgn > 0 else reversed(range(n_sub))):
            rows = slice(j * L, (j + 1) * L)
            lw = lw_ref[rows, :]
            hi, mid, lo = _split3(lw)
            c_in = _dot(tri, hi) + _dot(tri, mid) + _dot(tri, lo)
            c_tot = jnp.sum(lw, axis=0, keepdims=True)
            e_neg = jnp.exp(-c_in)
            e_end = jnp.exp(c_tot - c_in)
            kk = kk_ref[rows, :]
            b = b_ref[rows, :]
            kd = kd_ref[rows, :]
            a_t = -kk * jnp.exp(c_in - lw)
            r_t = r_ref[rows, :] * jnp.exp(c_in)
            b_t = b * e_neg
            k_t = kd * e_neg
            b_p = b * e_end
            k_p = kd * e_end
            p_end = jnp.exp(c_tot)
            v = v_ref[rows, :]
            for p in range(n_pairs):
                ls = slice(p * PW, (p + 1) * PW)
                a_n = a_t[:, ls].astype(BF16)
                chains.append(dict(
                    d=d, p=p, step=step, rows=rows, ls=ls, y_ref=y_ref, incl=incl2, strict=strict2,
                    ar=jnp.concatenate([a_n, r_t[:, ls].astype(BF16)], axis=0),
                    bk=jnp.concatenate([stack(b_t[:, ls].astype(BF16)), stack(k_t[:, ls].astype(BF16))], axis=0),
                    aa=stack(a_n), vv=stack(v[:, ls].astype(BF16)),
                    bpT=jnp.transpose(stack(b_p[:, ls])).astype(BF16),
                    kpT=jnp.transpose(stack(k_p[:, ls])).astype(BF16),
                    pe_col=jnp.transpose(jnp.broadcast_to(p_end[:, ls], (PW, PW)))))

    for c in chains:
        x = _dot_nt(c["ar"], c["bk"])
        c["a_ab"] = jnp.where(c["strict"], x[:L, :L2], 0.0)
        c["a_ak"] = jnp.where(c["strict"], x[:L, L2:], 0.0).astype(BF16)
        c["a_rb"] = jnp.where(c["incl"], x[L:, :L2], 0.0).astype(BF16)
        c["a_rk"] = jnp.where(c["incl"], x[L:, L2:], 0.0).astype(BF16)
        c["inv"] = eye2 + c["a_ab"]
        c["pw"] = c["a_ab"].astype(BF16)
    n_sq = int(math.log2(L)) - 1
    for k in range(n_sq + 1):
        for c in chains:
            pws = stack(c["pw"])
            if k == 0:
                c["pw"] = _dot(c["pw"], pws).astype(BF16)
            elif k < n_sq:
                both = _dot(jnp.concatenate([c["pw"], c["inv"].astype(BF16)], axis=0), pws)
                c["pw"] = both[:L, :].astype(BF16)
                c["inv"] = c["inv"] + both[L:, :]
            else:
                c["inv"] = c["inv"] + _dot(c["inv"].astype(BF16), pws)
    for c in chains:
        c["akv"] = stack(_dot(c["a_ak"], c["vv"]).astype(BF16))
    for c in chains:
        inv = c["inv"].astype(BF16)
        c["wi"] = jnp.concatenate([_dot(inv, c["aa"]).astype(BF16), inv], axis=1)
        c["bkT"] = jnp.concatenate([c["bpT"], c["kpT"]], axis=1)
        c["rab"] = jnp.concatenate([c["ar"][L:, :], c["a_rb"], c["a_rk"]], axis=1)
    state = {(d, p): h_ref[d, p] for d in range(2) for p in range(n_pairs)}
    for step in range(n_sub):
        now = [c for c in chains if c["step"] == step]
        for c in now:
            c["h"] = state[c["d"], c["p"]]
            c["hb"] = c["h"].astype(BF16)
            u = _dot(c["wi"], jnp.concatenate([c["hb"], c["akv"]], axis=0))
            c["uv"] = jnp.concatenate([stack(u.astype(BF16)), c["vv"]], axis=0)
        for c in now:
            state[c["d"], c["p"]] = c["pe_col"] * c["h"] + _dot(c["bkT"], c["uv"])
        for c in now:
            c["y_ref"][c["rows"], c["ls"]] = _dot(c["rab"], jnp.concatenate([c["hb"], c["uv"]], axis=0))
    for (d, p), h in state.items():
        h_ref[d, p] = h


def _scan(r, v, kk, lw, kd, b, bsz, s_len):
    rows = CHUNK * SCAN_CHUNKS_PER_STEP
    nc = s_len // rows
    shp = lambda a: a.reshape(bsz, s_len, RW_C)
    shp2 = lambda a: a.reshape(2, bsz, s_len, RW_C)
    f_sh = pl.BlockSpec((None, rows, RW_C), lambda bi, c: (bi, c, 0))
    r_sh = pl.BlockSpec((None, rows, RW_C), lambda bi, c: (bi, nc - 1 - c, 0))
    f_pd = pl.BlockSpec((None, None, rows, RW_C), lambda bi, c: (0, bi, c, 0))
    r_pd = pl.BlockSpec((None, None, rows, RW_C), lambda bi, c: (1, bi, nc - 1 - c, 0))
    out = jax.ShapeDtypeStruct((bsz, s_len, RW_C), F32)
    r3, v3, kk3, lw4, kd4, b4 = shp(r), shp(v), shp(kk), shp2(lw), shp2(kd), shp2(b)
    yf, yr = pl.pallas_call(
        functools.partial(_scan_kernel, chunk=CHUNK),
        grid=(bsz, nc),
        in_specs=[f_sh, f_sh, f_sh, f_pd, f_pd, f_pd, r_sh, r_sh, r_sh, r_pd, r_pd, r_pd],
        out_specs=[f_sh, r_sh],
        out_shape=[out, out],
        scratch_shapes=[pltpu.VMEM((2, RW_C // (2 * RW_N), 2 * RW_N, 2 * RW_N), F32)],
        compiler_params=_params(("parallel", "arbitrary"), VMEM_TILES_ONLY),
        name="scan",
    )(r3, v3, kk3, lw4, kd4, b4, r3, v3, kk3, lw4, kd4, b4)
    return yf.reshape(bsz * s_len, RW_C), yr.reshape(bsz * s_len, RW_C)


def _daprep_kernel(x_ref, g_ref, w_ref, cf_ref, sn_ref, sp_ref, qn_ref, kn_ref, bd_ref, q_o, k_o, v_o):
    bd = bd_ref[...]
    cf, sn, sp = cf_ref[...], sn_ref[...], sp_ref[...]
    p = _dot(_rms(x_ref[...], g_ref[...]).astype(BF16), w_ref[...])

    def norm_rope(x, g):
        ms = _group_sum(x * x, bd) * (1.0 / DA_DH)
        xn = x * lax.rsqrt(ms + NORM_EPS) * g
        half = ROT_DIM // 2
        return xn * cf + pltpu.roll(xn, DA_C - half, 1) * sn + pltpu.roll(xn, half, 1) * sp

    q = norm_rope(p[:, 0:DA_C], qn_ref[...]) * (DA_DH ** -0.5 * LOG2E)
    k = norm_rope(p[:, DA_C:2 * DA_C], kn_ref[...])
    q_o[...] = q.astype(BF16)
    k_o[...] = k.astype(BF16)
    v = p[:, 2 * DA_C:3 * DA_C].astype(BF16)
    hw = 2 * DA_DH
    for h in range(DA_HEADS):
        v_o[:, 2 * h * hw:(2 * h + 1) * hw] = v[:, h * hw:(h + 1) * hw]
        v_o[:, (2 * h + 1) * hw:(2 * h + 2) * hw] = jnp.ones((v.shape[0], hw), BF16)


def _rope_lane_tables(s_len):
    half = ROT_DIM // 2
    inv_freq = ROPE_THETA ** (-jnp.arange(0, ROT_DIM, 2, dtype=F32) / ROT_DIM)
    ang = jnp.arange(s_len, dtype=F32)[:, None] * inv_freq[None, :]
    cos, sin = jnp.cos(ang), jnp.sin(ang)
    j = jnp.arange(DA_C) % DA_DH
    f = j % half
    cf = jnp.where(j < ROT_DIM, cos[:, f], 1.0)
    sn = jnp.where(j < half, -sin[:, f], 0.0)
    sp = jnp.where((j >= half) & (j < ROT_DIM), sin[:, f], 0.0)
    return cf, sn, sp


def _daprep(x1, bsz, s_len, norm_g, w_da, q_norm, k_norm, bd):
    t = x1.shape[0]
    tm = min(ATT_TK, s_len)
    per_seq = s_len // tm
    cf, sn, sp = _rope_lane_tables(s_len)
    const2 = lambda i: (0, 0)
    tok = pl.BlockSpec((tm, DA_C), lambda i: (i, 0))
    tab = pl.BlockSpec((tm, DA_C), lambda i: (i % per_seq, 0))
    vec = pl.BlockSpec((1, DA_C), const2)
    out = jax.ShapeDtypeStruct((t, DA_C), BF16)
    reps = DA_C // DA_DH
    return pl.pallas_call(
        _daprep_kernel,
        grid=(t // tm,),
        in_specs=[pl.BlockSpec((tm, D_MODEL), lambda i: (i, 0)), pl.BlockSpec((1, D_MODEL), const2),
                  pl.BlockSpec((D_MODEL, 3 * DA_C), const2), tab, tab, tab, vec, vec,
                  pl.BlockSpec((DA_C, DA_C), const2)],
        out_specs=[tok, tok, pl.BlockSpec((tm, 2 * DA_C), lambda i: (i, 0))],
        out_shape=[out, out, jax.ShapeDtypeStruct((t, 2 * DA_C), BF16)],
        compiler_params=_params(("parallel",), VMEM_WEIGHTS_RESIDENT),
        name="daprep",
    )(x1, norm_g.reshape(1, D_MODEL), w_da.astype(BF16), cf, sn, sp,
      jnp.tile(q_norm, reps).reshape(1, DA_C), jnp.tile(k_norm, reps).reshape(1, DA_C), bd)


def _attn_kernel(q_ref, k_ref, v_ref, lq1_ref, lk1_ref, lq2_ref, lk2_ref, sub_ref, o_ref,
                 acc_ref, sa_ref, sb_ref, pa_ref, pb_ref, *, tq, tk):
    hw = 2 * DA_DH
    q = q_ref[...]
    lane = lax.broadcasted_iota(jnp.int32, q.shape, 1)
    zero = jnp.zeros_like(q)
    qs = jnp.concatenate([jnp.where(lane < DA_DH, q, zero), jnp.where(lane >= DA_DH, q, zero)], axis=0)
    nk = k_ref.shape[0] // tk

    def row_max(s):
        part = s[:, 0:hw]
        for j in range(1, tk // hw):
            part = jnp.maximum(part, s[:, j * hw:(j + 1) * hw])
        col = jnp.max(jnp.transpose(part), axis=0, keepdims=True)
        return jnp.transpose(jnp.broadcast_to(col, (hw, 2 * tq)))

    def scores_into(s_ref, i):
        s = _dot_nt(qs, k_ref[i * tk:(i + 1) * tk, :])
        s_ref[...] = s.astype(BF16)
        return row_max(s)

    def softmax(s_ref, p_ref, m_prev, m_blk):
        m_new = jnp.maximum(m_prev, m_blk.astype(BF16).astype(F32))
        m_b = m_new.astype(BF16)
        for j in range(tk // hw):
            p_ref[:, j * hw:(j + 1) * hw] = jnp.exp2(s_ref[:, j * hw:(j + 1) * hw] - m_b)
        return m_new, jnp.exp2(m_prev - m_new)

    def accumulate(p_ref, i, corr):
        pv = _dot(p_ref[...], v_ref[i * tk:(i + 1) * tk, :])
        acc_ref[...] = pv if i == 0 else jnp.concatenate([corr, corr], axis=1) * acc_ref[...] + pv

    s_refs = (sa_ref, sb_ref)
    p_refs = (pa_ref, pb_ref)
    m_blk = scores_into(s_refs[0], 0)
    m = jnp.full((2 * tq, hw), -jnp.inf, F32)
    corr = None
    for i in range(nk):
        m_next = scores_into(s_refs[(i + 1) % 2], i + 1) if i + 1 < nk else None
        if i > 0:
            accumulate(p_refs[(i - 1) % 2], i - 1, corr)
        m, corr = softmax(s_refs[i % 2], p_refs[i % 2], m, m_blk)
        m_blk = m_next
    accumulate(p_refs[(nk - 1) % 2], nk - 1, corr)

    o2 = acc_ref[:, 0:hw] / acc_ref[:, hw:2 * hw]
    lam = (jnp.exp(jnp.sum(lq1_ref[...] * lk1_ref[...], axis=1, keepdims=True))
           - jnp.exp(jnp.sum(lq2_ref[...] * lk2_ref[...], axis=1, keepdims=True))) + LAM_INIT
    o = o2[0:tq, :] - lam * o2[tq:2 * tq, :]
    o_ref[...] = _rms(o, sub_ref[...]) * (1.0 - LAM_INIT)


def _attn(q, k, vt, bsz, s_len, lq1, lk1, lq2, lk2, subln):
    tq = min(max(ATT_TQ, ATT_STEP_WORK // s_len), 2 * ATT_TQ, s_len)
    tk = min(ATT_TK, s_len)
    nk = s_len // tk
    hw = 2 * DA_DH
    shp = lambda a: a.reshape(bsz, s_len, DA_C)
    lvec = pl.BlockSpec((1, DA_DH), lambda b, h, qi: (0, 0))
    o = pl.pallas_call(
        functools.partial(_attn_kernel, tq=tq, tk=tk),
        grid=(bsz, DA_HEADS, s_len // tq),
        in_specs=[
            pl.BlockSpec((None, tq, hw), lambda b, h, qi: (b, qi, h)),
            pl.BlockSpec((None, s_len, hw), lambda b, h, qi: (b, 0, h)),
            pl.BlockSpec((None, s_len, 2 * hw), lambda b, h, qi: (b, 0, h)),
            lvec, lvec, lvec, lvec,
            pl.BlockSpec((1, hw), lambda b, h, qi: (0, 0)),
        ],
        out_specs=pl.BlockSpec((None, tq, hw), lambda b, h, qi: (b, qi, h)),
        out_shape=jax.ShapeDtypeStruct((bsz, s_len, DA_C), F32),
        scratch_shapes=[pltpu.VMEM((2 * tq, 2 * hw), F32),
                        pltpu.VMEM((2 * tq, tk), BF16), pltpu.VMEM((2 * tq, tk), BF16),
                        pltpu.VMEM((2 * tq, tk), BF16), pltpu.VMEM((2 * tq, tk), BF16)],
        compiler_params=_params(("parallel", "parallel", "arbitrary"), VMEM_TILES_ONLY),
        name="attn",
    )(shp(q), shp(k), vt.reshape(bsz, s_len, 2 * DA_C), lq1.reshape(1, DA_DH), lk1.reshape(1, DA_DH), lq2.reshape(1, DA_DH),
      lk2.reshape(1, DA_DH), subln.reshape(1, hw))
    return o.reshape(bsz * s_len, DA_C)


def _mixout_ffn_kernel(x_ref, yf_ref, yr_ref, g_ref, bonus_ref, da_ref, lng_ref, lnb_ref, bd_ref, wout_ref,
                       fg_ref, wgu_ref, wd_ref, final_ref, o_ref):
    bd = bd_ref[...]
    y = yf_ref[...] + yr_ref[...]
    mu = _group_sum(y, bd) * (1.0 / RW_N)
    yc = y - mu
    var = _group_sum(yc * yc, bd) * (1.0 / RW_N)
    yn = yc * lax.rsqrt(var + GN_EPS) * lng_ref[...] + lnb_ref[...]
    y_rw = ((yn + bonus_ref[...]) * g_ref[...]).astype(BF16)
    mixed = _dot(y_rw, wout_ref[0:RW_C, :]) + _dot(da_ref[...].astype(BF16), wout_ref[RW_C:RW_C + DA_C, :])
    x2 = x_ref[...] + mixed
    o_ref[...] = _rms(_half_swiglu(x2, fg_ref, wgu_ref, wd_ref), final_ref[...])


def _mixout_ffn(x1, yf, yr, g, bonus, o_da, ln_g, ln_b, bd, w_out, norm_g, w_gu, w_down, final_g):
    t = x1.shape[0]
    tm = min(TOKEN_TILE, t)
    tok = pl.BlockSpec((tm, RW_C), lambda i: (i, 0))
    return pl.pallas_call(
        _mixout_ffn_kernel,
        grid=(t // tm,),
        in_specs=[
            pl.BlockSpec((tm, D_MODEL), lambda i: (i, 0)),
            tok, tok, tok, tok, tok,
            _resident((1, RW_C)), _resident((1, RW_C)), _resident((RW_C, RW_C)), _resident((RW_C + DA_C, D_MODEL)),
            _resident((1, D_MODEL)), _resident((D_MODEL, 2 * D_FF)), _resident((D_FF, D_MODEL)),
            _resident((1, D_MODEL)),
        ],
        out_specs=pl.BlockSpec((tm, D_MODEL), lambda i: (i, 0)),
        out_shape=jax.ShapeDtypeStruct((t, D_MODEL), F32),
        compiler_params=_params(("parallel",), VMEM_WEIGHTS_RESIDENT),
        name="mixout_ffn",
    )(x1, yf, yr, g, bonus, o_da, ln_g.reshape(1, RW_C), ln_b.reshape(1, RW_C), bd, w_out.astype(BF16),
      norm_g.reshape(1, D_MODEL), w_gu.astype(BF16), w_down.astype(BF16), final_g.reshape(1, D_MODEL))


def _encoder_layer(x, p):
    bsz, s_len, _ = x.shape
    t = bsz * s_len
    xf = x.reshape(t, D_MODEL)
    lane = jnp.arange(RW_C) // RW_N
    bd = (lane[:, None] == lane[None, :]).astype(BF16)

    x1 = _ffn(xf, p["ffn1_norm"], p["ffn1_w_gu"], p["ffn1_w_down"])
    r, v, kk, g, bonus, lw, kd, b = _rwprep(
        x1, s_len, p["mix_norm"], p["w_in"][:, :RW_COLS], p["conv_w"], p["rw_w0"], p["rw_w_up"], p["rw_a0"],
        p["rw_a_up"], p["rw_g_up"], p["rw_k_k"], p["rw_k_a"], p["rw_r_k"], bd)
    yf, yr = _scan(r, v, kk, lw, kd, b, bsz, s_len)
    q, k, va = _daprep(x1, bsz, s_len, p["mix_norm"], p["w_in"][:, RW_COLS:], p["da_q_norm"], p["da_k_norm"], bd)
    o_da = _attn(q, k, va, bsz, s_len, p["da_lq1"], p["da_lk1"], p["da_lq2"], p["da_lk2"], p["da_subln"])
    out = _mixout_ffn(x1, yf, yr, g, bonus, o_da, p["rw_ln_g"], p["rw_ln_b"], bd, p["w_out"],
                      p["ffn2_norm"], p["ffn2_w_gu"], p["ffn2_w_down"], p["final_norm"])
    return out.reshape(bsz, s_len, D_MODEL)


def kernel(x_prompt, x_sample, ffn1_norm, ffn1_w_gu, ffn1_w_down, mix_norm, w_in, conv_w, rw_w0, rw_w_up, rw_a0, rw_a_up, rw_g_up, rw_k_k, rw_k_a, rw_r_k, rw_ln_g, rw_ln_b, da_q_norm, da_k_norm, da_lq1, da_lk1, da_lq2, da_lk2, da_subln, w_out, ffn2_norm, ffn2_w_gu, ffn2_w_down, final_norm):
    names = ("ffn1_norm", "ffn1_w_gu", "ffn1_w_down", "mix_norm", "w_in", "conv_w", "rw_w0", "rw_w_up", "rw_a0",
             "rw_a_up", "rw_g_up", "rw_k_k", "rw_k_a", "rw_r_k", "rw_ln_g", "rw_ln_b", "da_q_norm", "da_k_norm",
             "da_lq1", "da_lk1", "da_lq2", "da_lk2", "da_subln", "w_out", "ffn2_norm", "ffn2_w_gu", "ffn2_w_down",
             "final_norm")
    vals = (ffn1_norm, ffn1_w_gu, ffn1_w_down, mix_norm, w_in, conv_w, rw_w0, rw_w_up, rw_a0, rw_a_up, rw_g_up,
            rw_k_k, rw_k_a, rw_r_k, rw_ln_g, rw_ln_b, da_q_norm, da_k_norm, da_lq1, da_lk1, da_lq2, da_lk2,
            da_subln, w_out, ffn2_norm, ffn2_w_gu, ffn2_w_down, final_norm)
    assert all(a.shape[0] == 1 for a in vals), "single-layer (depth 1) parameter stacks expected"
    p = {n: a[0] for n, a in zip(names, vals)}
    return (_encoder_layer(x_prompt, p), _encoder_layer(x_sample, p))
```

```python
import functools
import math

import jax
import jax.numpy as jnp
from jax import lax
from jax.experimental import pallas as pl
from jax.experimental.pallas import tpu as pltpu

F32 = jnp.float32
BF16 = jnp.bfloat16

D_MODEL = 1024
D_FF = 2816
RW_HEADS = 8
RW_N = 64
RW_C = RW_HEADS * RW_N
LORA_W = 64
LORA_A = 64
LORA_G = 128
DECAY_SCALE = 0.606531
GN_EPS = 64e-5
DA_HEADS = 4
DA_DH = 64
DA_C = DA_HEADS * 2 * DA_DH
ROT_DIM = DA_DH // 4
ROPE_THETA = 500000.0
NORM_EPS = 1e-6
RW_COLS = 3 * RW_C + 2 * LORA_W + 2 * LORA_A + LORA_G
LAM_INIT = 0.8 - 0.6 * math.exp(-0.3 * 0)
LOG2E = math.log2(math.e)

V7X_VMEM_BYTES = 64 * 1024 * 1024
SUBLANES = 8
VMEM_WEIGHTS_RESIDENT = V7X_VMEM_BYTES * 7 // 8
VMEM_TILES_ONLY = V7X_VMEM_BYTES // 2

FF_CHUNK = 256
N_FF_CHUNKS = D_FF // FF_CHUNK
TOKEN_TILE = 512
PREP_TILE = 512
CHUNK = 64
SCAN_CHUNKS_PER_STEP = 4
ATT_TQ = 512
ATT_STEP_WORK = 4 * 1024 * 1024
ONES_ROWS = 16
VT_ROWS = 2 * DA_DH + ONES_ROWS
ATT_TK = 512


def _params(sem, vmem_bytes):
    return pltpu.CompilerParams(dimension_semantics=sem, vmem_limit_bytes=vmem_bytes)


def _dot(a, b):
    return jnp.dot(a, b, preferred_element_type=F32)


def _dot_nt(a, b):
    return lax.dot_general(a, b, (((1,), (1,)), ((), ())), preferred_element_type=F32)


def _split2(x):
    hi = x.astype(BF16)
    lo = (x - hi.astype(F32)).astype(BF16)
    return hi, lo


def _split3(x):
    hi = x.astype(BF16)
    r1 = x - hi.astype(F32)
    mid = r1.astype(BF16)
    lo = (r1 - mid.astype(F32)).astype(BF16)
    return hi, mid, lo


def _group_sum(x, bd):
    hi, lo = _split2(x)
    return _dot(hi, bd) + _dot(lo, bd)


def _rms(x, g, eps=NORM_EPS):
    return x * lax.rsqrt(jnp.mean(x * x, axis=-1, keepdims=True) + eps) * g


def _half_swiglu(x, g_ref, wgu_ref, wd_ref):
    h = _rms(x, g_ref[...]).astype(BF16)
    acc = jnp.zeros_like(x)
    for j in range(N_FF_CHUNKS):
        lo = j * FF_CHUNK
        gate = _dot(h, wgu_ref[:, lo:lo + FF_CHUNK])
        up = _dot(h, wgu_ref[:, D_FF + lo:D_FF + lo + FF_CHUNK])
        act = (gate * jax.nn.sigmoid(gate) * up).astype(BF16)
        acc = acc + _dot(act, wd_ref[lo:lo + FF_CHUNK, :])
    return x + 0.5 * acc


def _ffn_kernel(x_ref, g_ref, wgu_ref, wd_ref, o_ref):
    o_ref[...] = _half_swiglu(x_ref[...], g_ref, wgu_ref, wd_ref)


def _resident(shape):
    return pl.BlockSpec(shape, lambda i: (0,) * len(shape), pipeline_mode=pl.Buffered(1))


def _ffn(x, norm_g, w_gu, w_down):
    t = x.shape[0]
    tm = min(TOKEN_TILE, t)
    return pl.pallas_call(
        _ffn_kernel,
        grid=(t // tm,),
        in_specs=[
            pl.BlockSpec((tm, D_MODEL), lambda i: (i, 0)),
            _resident((1, D_MODEL)),
            _resident((D_MODEL, 2 * D_FF)),
            _resident((D_FF, D_MODEL)),
        ],
        out_specs=pl.BlockSpec((tm, D_MODEL), lambda i: (i, 0)),
        out_shape=jax.ShapeDtypeStruct((t, D_MODEL), F32),
        compiler_params=_params(("parallel",), VMEM_WEIGHTS_RESIDENT),
        name="ffn",
    )(x, norm_g.reshape(1, D_MODEL), w_gu.astype(BF16), w_down.astype(BF16))


def _rwprep_kernel(x_ref, xp_ref, xn_ref, g_ref, w_ref, conv_ref, w0_ref, wup_ref, a0_ref, aup_ref, gup_ref,
                   kk_ref, ka_ref, rk_ref, bd_ref,
                   r_o, v_o, kk_o, g_o, bonus_o, lw_o, kd_o, b_o, *, tm, s_len):
    i = pl.program_id(0)
    xa = jnp.concatenate([x_ref[...], xp_ref[...], xn_ref[...]], axis=0)
    ua = _dot(_rms(xa, g_ref[...]).astype(BF16), w_ref[...])
    u = ua[0:tm, :]
    start = i * tm
    first = (start % s_len) == 0
    last = ((start + tm) % s_len) == 0
    prev_row = jnp.where(first, 0.0, ua[tm + SUBLANES - 1:tm + SUBLANES, :])
    next_row = jnp.where(last, 0.0, ua[tm + SUBLANES:tm + SUBLANES + 1, :])
    row = lax.broadcasted_iota(jnp.int32, (tm, 1), 0)
    prev = jnp.where(row == 0, prev_row, pltpu.roll(u, 1, 0))
    nxt = jnp.where(row == tm - 1, next_row, pltpu.roll(u, tm - 1, 0))
    c = prev * conv_ref[0:1, :] + u * conv_ref[1:2, :] + nxt * conv_ref[2:3, :]

    o_w = 3 * RW_C
    o_a = o_w + 2 * LORA_W
    o_g = o_a + 2 * LORA_A
    r = c[:, 0:RW_C]
    k = c[:, RW_C:2 * RW_C]
    v = c[:, 2 * RW_C:3 * RW_C]
    tw = jnp.tanh(c[:, o_w:o_a]).astype(BF16)
    ad = c[:, o_a:o_g].astype(BF16)
    gd = c[:, o_g:RW_COLS]
    bd = bd_ref[...]

    kkv = k * kk_ref[...]
    norm = jnp.sqrt(_group_sum(kkv * kkv, bd))
    kk = kkv / jnp.maximum(norm, 1e-12)
    g = _dot(jax.nn.sigmoid(gd).astype(BF16), gup_ref[...])

    ksum = jnp.zeros_like(k)
    for d in range(2):
        lw = -DECAY_SCALE * jax.nn.sigmoid(w0_ref[d:d + 1, :] + _dot(tw, wup_ref[d]))
        a = jax.nn.sigmoid(a0_ref[d:d + 1, :] + _dot(ad, aup_ref[d]))
        kd = k * (1.0 + (a - 1.0) * ka_ref[...])
        lw_o[d] = lw
        kd_o[d] = kd
        b_o[d] = kk * a
        ksum = ksum + kd
    coef = _group_sum(r * ksum * rk_ref[...], bd)
    r_o[...] = r
    v_o[...] = v
    kk_o[...] = kk
    g_o[...] = g
    bonus_o[...] = coef * v


def _pad_dir_rows(w_up):
    z = jnp.zeros_like(w_up[0])
    return jnp.stack([jnp.concatenate([w_up[0], z], 0), jnp.concatenate([z, w_up[1]], 0)]).astype(BF16)


def _rwprep(x1, s_len, norm_g, w_rw, conv_w, w0, w_up, a0, a_up, g_up, k_k, k_a, r_k, bd):
    t = x1.shape[0]
    tm = min(PREP_TILE, s_len)
    nb8 = t // SUBLANES
    per = tm // SUBLANES
    const2 = lambda i: (0, 0)
    const3 = lambda i: (0, 0, 0)
    tok = pl.BlockSpec((tm, RW_C), lambda i: (i, 0))
    tok2 = pl.BlockSpec((2, tm, RW_C), lambda i: (0, i, 0))
    vec = pl.BlockSpec((1, RW_C), const2)
    one = jax.ShapeDtypeStruct((t, RW_C), F32)
    two = jax.ShapeDtypeStruct((2, t, RW_C), F32)
    return pl.pallas_call(
        functools.partial(_rwprep_kernel, tm=tm, s_len=s_len),
        grid=(t // tm,),
        in_specs=[
            pl.BlockSpec((tm, D_MODEL), lambda i: (i, 0)),
            pl.BlockSpec((SUBLANES, D_MODEL), lambda i: (jnp.maximum(i * per - 1, 0), 0)),
            pl.BlockSpec((SUBLANES, D_MODEL), lambda i: (jnp.minimum((i + 1) * per, nb8 - 1), 0)),
            pl.BlockSpec((1, D_MODEL), const2),
            pl.BlockSpec((D_MODEL, RW_COLS), const2),
            pl.BlockSpec((3, RW_COLS), const2),
            pl.BlockSpec((2, RW_C), const2),
            pl.BlockSpec((2, 2 * LORA_W, RW_C), const3),
            pl.BlockSpec((2, RW_C), const2),
            pl.BlockSpec((2, 2 * LORA_A, RW_C), const3),
            pl.BlockSpec((LORA_G, RW_C), const2),
            vec, vec, vec,
            pl.BlockSpec((RW_C, RW_C), const2),
        ],
        out_specs=[tok, tok, tok, tok, tok, tok2, tok2, tok2],
        out_shape=[one, one, one, one, one, two, two, two],
        compiler_params=_params(("parallel",), VMEM_WEIGHTS_RESIDENT),
        name="rwprep",
    )(x1, x1, x1, norm_g.reshape(1, D_MODEL), w_rw.astype(BF16), conv_w, w0, _pad_dir_rows(w_up), a0, _pad_dir_rows(a_up),
      g_up.astype(BF16), k_k.reshape(1, RW_C), k_a.reshape(1, RW_C), r_k.reshape(1, RW_C), bd)


def _scan_kernel(rf_ref, vf_ref, kkf_ref, lwf_ref, kdf_ref, bf_ref, rr_ref, vr_ref, kkr_ref, lwr_ref, kdr_ref, br_ref,
                 yf_ref, yr_ref, h_ref, *, chunk):
    L = chunk
    L2 = 2 * L
    PW = 2 * RW_N
    n_pairs = RW_C // PW
    n_sub = lwf_ref.shape[0] // L
    assert L2 == PW, "interaction matrices and lane pairs share the 128-lane tile"

    @pl.when(pl.program_id(1) == 0)
    def _():
        h_ref[...] = jnp.zeros_like(h_ref)

    row = lax.broadcasted_iota(jnp.int32, (L, L), 0)
    col = lax.broadcasted_iota(jnp.int32, (L, L), 1)
    row2 = lax.broadcasted_iota(jnp.int32, (L, L2), 0)
    col2 = lax.broadcasted_iota(jnp.int32, (L, L2), 1) & (L - 1)
    eye2 = (row2 == col2).astype(F32)
    first_head = lax.broadcasted_iota(jnp.int32, (L, PW), 1) < RW_N

    def stack(z):
        zero = jnp.zeros_like(z)
        return jnp.concatenate([jnp.where(first_head, z, zero), jnp.where(first_head, zero, z)], axis=0)

    chains = []
    dirs = ((rf_ref, vf_ref, kkf_ref, lwf_ref, kdf_ref, bf_ref, yf_ref, 1),
            (rr_ref, vr_ref, kkr_ref, lwr_ref, kdr_ref, br_ref, yr_ref, -1))
    for d, (r_ref, v_ref, kk_ref, lw_ref, kd_ref, b_ref, y_ref, sgn) in enumerate(dirs):
        incl2 = (row2 - col2) * sgn >= 0
        strict2 = (row2 - col2) * sgn > 0
        tri = ((row - col) * sgn >= 0).astype(BF16)
        for step, j in enumerate(range(n_sub) if sgn > 0 else reversed(range(n_sub))):
            rows = slice(j * L, (j + 1) * L)
            lw = lw_ref[rows, :]
            hi, mid, lo = _split3(lw)
            c_in = _dot(tri, hi) + _dot(tri, mid) + _dot(tri, lo)
            c_tot = jnp.sum(lw, axis=0, keepdims=True)
            e_neg = jnp.exp(-c_in)
            e_end = jnp.exp(c_tot - c_in)
            kk = kk_ref[rows, :]
            b = b_ref[rows, :]
            kd = kd_ref[rows, :]
            a_t = -kk * jnp.exp(c_in - lw)
            r_t = r_ref[rows, :] * jnp.exp(c_in)
            b_t = b * e_neg
            k_t = kd * e_neg
            b_p = b * e_end
            k_p = kd * e_end
            p_end = jnp.exp(c_tot)
            v = v_ref[rows, :]
            for p in range(n_pairs):
                ls = slice(p * PW, (p + 1) * PW)
                a_n = a_t[:, ls].astype(BF16)
                chains.append(dict(
                    d=d, p=p, step=step, rows=rows, ls=ls, y_ref=y_ref, incl=incl2, strict=strict2,
                    ar=jnp.concatenate([a_n, r_t[:, ls].astype(BF16)], axis=0),
                    bk=jnp.concatenate([stack(b_t[:, ls].astype(BF16)), stack(k_t[:, ls].astype(BF16))], axis=0),
                    aa=stack(a_n), vv=stack(v[:, ls].astype(BF16)),
                    bpT=jnp.transpose(stack(b_p[:, ls])).astype(BF16),
                    kpT=jnp.transpose(stack(k_p[:, ls])).astype(BF16),
                    pe_col=jnp.transpose(jnp.broadcast_to(p_end[:, ls], (PW, PW)))))

    for c in chains:
        x = _dot_nt(c["ar"], c["bk"])
        c["a_ab"] = jnp.where(c["strict"], x[:L, :L2], 0.0)
        c["a_ak"] = jnp.where(c["strict"], x[:L, L2:], 0.0).astype(BF16)
        c["a_rb"] = jnp.where(c["incl"], x[L:, :L2], 0.0).astype(BF16)
        c["a_rk"] = jnp.where(c["incl"], x[L:, L2:], 0.0).astype(BF16)
        c["inv"] = eye2 + c["a_ab"]
        c["pw"] = c["a_ab"].astype(BF16)
    n_sq = int(math.log2(L)) - 1
    for k in range(n_sq + 1):
        for c in chains:
            pws = stack(c["pw"])
            if k == 0:
                c["pw"] = _dot(c["pw"], pws).astype(BF16)
            elif k < n_sq:
                both = _dot(jnp.concatenate([c["pw"], c["inv"].astype(BF16)], axis=0), pws)
                c["pw"] = both[:L, :].astype(BF16)
                c["inv"] = c["inv"] + both[L:, :]
            else:
                c["inv"] = c["inv"] + _dot(c["inv"].astype(BF16), pws)
    for c in chains:
        c["akv"] = stack(_dot(c["a_ak"], c["vv"]).astype(BF16))
    for c in chains:
        inv = c["inv"].astype(BF16)
        c["wi"] = jnp.concatenate([_dot(inv, c["aa"]).astype(BF16), inv], axis=1)
        c["bkT"] = jnp.concatenate([c["bpT"], c["kpT"]], axis=1)
        c["rab"] = jnp.concatenate([c["ar"][L:, :], c["a_rb"], c["a_rk"]], axis=1)
    state = {(d, p): h_ref[d, p] for d in range(2) for p in range(n_pairs)}
    for step in range(n_sub):
        now = [c for c in chains if c["step"] == step]
        for c in now:
            c["h"] = state[c["d"], c["p"]]
            c["hb"] = c["h"].astype(BF16)
            u = _dot(c["wi"], jnp.concatenate([c["hb"], c["akv"]], axis=0))
            c["uv"] = jnp.concatenate([stack(u.astype(BF16)), c["vv"]], axis=0)
        for c in now:
            state[c["d"], c["p"]] = c["pe_col"] * c["h"] + _dot(c["bkT"], c["uv"])
        for c in now:
            c["y_ref"][c["rows"], c["ls"]] = _dot(c["rab"], jnp.concatenate([c["hb"], c["uv"]], axis=0))
    for (d, p), h in state.items():
        h_ref[d, p] = h


def _scan(r, v, kk, lw, kd, b, bsz, s_len):
    rows = CHUNK * SCAN_CHUNKS_PER_STEP
    nc = s_len // rows
    shp = lambda a: a.reshape(bsz, s_len, RW_C)
    shp2 = lambda a: a.reshape(2, bsz, s_len, RW_C)
    f_sh = pl.BlockSpec((None, rows, RW_C), lambda bi, c: (bi, c, 0))
    r_sh = pl.BlockSpec((None, rows, RW_C), lambda bi, c: (bi, nc - 1 - c, 0))
    f_pd = pl.BlockSpec((None, None, rows, RW_C), lambda bi, c: (0, bi, c, 0))
    r_pd = pl.BlockSpec((None, None, rows, RW_C), lambda bi, c: (1, bi, nc - 1 - c, 0))
    out = jax.ShapeDtypeStruct((bsz, s_len, RW_C), F32)
    r3, v3, kk3, lw4, kd4, b4 = shp(r), shp(v), shp(kk), shp2(lw), shp2(kd), shp2(b)
    yf, yr = pl.pallas_call(
        functools.partial(_scan_kernel, chunk=CHUNK),
        grid=(bsz, nc),
        in_specs=[f_sh, f_sh, f_sh, f_pd, f_pd, f_pd, r_sh, r_sh, r_sh, r_pd, r_pd, r_pd],
        out_specs=[f_sh, r_sh],
        out_shape=[out, out],
        scratch_shapes=[pltpu.VMEM((2, RW_C // (2 * RW_N), 2 * RW_N, 2 * RW_N), F32)],
        compiler_params=_params(("parallel", "arbitrary"), VMEM_TILES_ONLY),
        name="scan",
    )(r3, v3, kk3, lw4, kd4, b4, r3, v3, kk3, lw4, kd4, b4)
    return yf.reshape(bsz * s_len, RW_C), yr.reshape(bsz * s_len, RW_C)


def _daprep_kernel(x_ref, g_ref, w_ref, cf_ref, sn_ref, sp_ref, qn_ref, kn_ref, bd_ref, q_o, k_o, v_o):
    bd = bd_ref[...]
    cf, sn, sp = cf_ref[...], sn_ref[...], sp_ref[...]
    p = _dot(_rms(x_ref[...], g_ref[...]).astype(BF16), w_ref[...])

    def norm_rope(x, g):
        ms = _group_sum(x * x, bd) * (1.0 / DA_DH)
        xn = x * lax.rsqrt(ms + NORM_EPS) * g
        half = ROT_DIM // 2
        return xn * cf + pltpu.roll(xn, DA_C - half, 1) * sn + pltpu.roll(xn, half, 1) * sp

    q = norm_rope(p[:, 0:DA_C], qn_ref[...]) * (DA_DH ** -0.5 * LOG2E)
    k = norm_rope(p[:, DA_C:2 * DA_C], kn_ref[...])
    q_o[...] = q.astype(BF16)
    k_o[...] = k.astype(BF16)
    vt = jnp.transpose(p[:, 2 * DA_C:3 * DA_C]).astype(BF16)
    hw = 2 * DA_DH
    for h in range(DA_HEADS):
        v_o[h, 0:hw, :] = vt[h * hw:(h + 1) * hw, :]
        v_o[h, hw:hw + ONES_ROWS, :] = jnp.ones((ONES_ROWS, vt.shape[1]), BF16)


def _rope_lane_tables(s_len):
    half = ROT_DIM // 2
    inv_freq = ROPE_THETA ** (-jnp.arange(0, ROT_DIM, 2, dtype=F32) / ROT_DIM)
    ang = jnp.arange(s_len, dtype=F32)[:, None] * inv_freq[None, :]
    cos, sin = jnp.cos(ang), jnp.sin(ang)
    j = jnp.arange(DA_C) % DA_DH
    f = j % half
    cf = jnp.where(j < ROT_DIM, cos[:, f], 1.0)
    sn = jnp.where(j < half, -sin[:, f], 0.0)
    sp = jnp.where((j >= half) & (j < ROT_DIM), sin[:, f], 0.0)
    return cf, sn, sp


def _daprep(x1, bsz, s_len, norm_g, w_da, q_norm, k_norm, bd):
    t = x1.shape[0]
    tm = min(ATT_TK, s_len)
    per_seq = s_len // tm
    cf, sn, sp = _rope_lane_tables(s_len)
    const2 = lambda i: (0, 0)
    tok = pl.BlockSpec((tm, DA_C), lambda i: (i, 0))
    tab = pl.BlockSpec((tm, DA_C), lambda i: (i % per_seq, 0))
    vec = pl.BlockSpec((1, DA_C), const2)
    out = jax.ShapeDtypeStruct((t, DA_C), BF16)
    reps = DA_C // DA_DH
    return pl.pallas_call(
        _daprep_kernel,
        grid=(t // tm,),
        in_specs=[pl.BlockSpec((tm, D_MODEL), lambda i: (i, 0)), pl.BlockSpec((1, D_MODEL), const2),
                  pl.BlockSpec((D_MODEL, 3 * DA_C), const2), tab, tab, tab, vec, vec,
                  pl.BlockSpec((DA_C, DA_C), const2)],
        out_specs=[tok, tok, pl.BlockSpec((None, None, DA_HEADS, VT_ROWS, tm),
                                          lambda i: (i // per_seq, i % per_seq, 0, 0, 0))],
        out_shape=[out, out, jax.ShapeDtypeStruct((bsz, per_seq, DA_HEADS, VT_ROWS, tm), BF16)],
        compiler_params=_params(("parallel",), VMEM_WEIGHTS_RESIDENT),
        name="daprep",
    )(x1, norm_g.reshape(1, D_MODEL), w_da.astype(BF16), cf, sn, sp,
      jnp.tile(q_norm, reps).reshape(1, DA_C), jnp.tile(k_norm, reps).reshape(1, DA_C), bd)


def _attn_kernel(q_ref, k_ref, vt_ref, lq1_ref, lk1_ref, lq2_ref, lk2_ref, sub_ref, o_ref,
                 acc_ref, sa_ref, sb_ref, pa_ref, pb_ref, *, tq, tk):
    q = q_ref[...]
    lane = lax.broadcasted_iota(jnp.int32, q.shape, 1)
    zero = jnp.zeros_like(q)
    qs = jnp.concatenate([jnp.where(lane < DA_DH, q, zero), jnp.where(lane >= DA_DH, q, zero)], axis=0)
    nk = k_ref.shape[0] // tk

    def scores(i):
        return _dot_nt(k_ref[i * tk:(i + 1) * tk, :], qs)

    def scores_into(s_ref, i):
        s = scores(i)
        s_ref[...] = s
        return jnp.max(s, axis=0, keepdims=True)

    def softmax(s_ref, p_ref, m_prev, m_blk):
        m_new = jnp.maximum(m_prev, m_blk)
        p_ref[...] = jnp.exp2((s_ref[...] - m_new).astype(BF16))
        return m_new, jnp.exp2(m_prev - m_new)

    def accumulate(p_ref, i, corr):
        pv = _dot(vt_ref[i], p_ref[...])
        acc_ref[...] = pv if i == 0 else corr * acc_ref[...] + pv

    s_refs = (sa_ref, sb_ref)
    p_refs = (pa_ref, pb_ref)
    m_blk = scores_into(s_refs[0], 0)
    m = jnp.full((1, 2 * tq), -jnp.inf, F32)
    corr = None
    for i in range(nk):
        m_next = scores_into(s_refs[(i + 1) % 2], i + 1) if i + 1 < nk else None
        if i > 0:
            accumulate(p_refs[(i - 1) % 2], i - 1, corr)
        m, corr = softmax(s_refs[i % 2], p_refs[i % 2], m, m_blk)
        m_blk = m_next
    accumulate(p_refs[(nk - 1) % 2], nk - 1, corr)

    hw = 2 * DA_DH
    o2 = acc_ref[0:hw, :] / acc_ref[hw:hw + 1, :]
    lam = (jnp.exp(jnp.sum(lq1_ref[...] * lk1_ref[...], axis=1, keepdims=True))
           - jnp.exp(jnp.sum(lq2_ref[...] * lk2_ref[...], axis=1, keepdims=True))) + LAM_INIT
    o = jnp.transpose(o2[:, 0:tq] - lam * o2[:, tq:2 * tq])
    o_ref[...] = _rms(o, sub_ref[...]) * (1.0 - LAM_INIT)


def _attn(q, k, vt, bsz, s_len, lq1, lk1, lq2, lk2, subln):
    tq = min(max(ATT_TQ, ATT_STEP_WORK // s_len), 2 * ATT_TQ, s_len)
    tk = min(ATT_TK, s_len)
    nk = s_len // tk
    hw = 2 * DA_DH
    shp = lambda a: a.reshape(bsz, s_len, DA_C)
    lvec = pl.BlockSpec((1, DA_DH), lambda b, h, qi: (0, 0))
    o = pl.pallas_call(
        functools.partial(_attn_kernel, tq=tq, tk=tk),
        grid=(bsz, DA_HEADS, s_len // tq),
        in_specs=[
            pl.BlockSpec((None, tq, hw), lambda b, h, qi: (b, qi, h)),
            pl.BlockSpec((None, s_len, hw), lambda b, h, qi: (b, 0, h)),
            pl.BlockSpec((None, nk, None, VT_ROWS, tk), lambda b, h, qi: (b, 0, h, 0, 0)),
            lvec, lvec, lvec, lvec,
            pl.BlockSpec((1, hw), lambda b, h, qi: (0, 0)),
        ],
        out_specs=pl.BlockSpec((None, tq, hw), lambda b, h, qi: (b, qi, h)),
        out_shape=jax.ShapeDtypeStruct((bsz, s_len, DA_C), F32),
        scratch_shapes=[pltpu.VMEM((VT_ROWS, 2 * tq), F32),
                        pltpu.VMEM((tk, 2 * tq), F32), pltpu.VMEM((tk, 2 * tq), F32),
                        pltpu.VMEM((tk, 2 * tq), BF16), pltpu.VMEM((tk, 2 * tq), BF16)],
        compiler_params=_params(("parallel", "parallel", "arbitrary"), VMEM_TILES_ONLY),
        name="attn",
    )(shp(q), shp(k), vt, lq1.reshape(1, DA_DH), lk1.reshape(1, DA_DH), lq2.reshape(1, DA_DH),
      lk2.reshape(1, DA_DH), subln.reshape(1, hw))
    return o.reshape(bsz * s_len, DA_C)


def _mixout_ffn_kernel(x_ref, yf_ref, yr_ref, g_ref, bonus_ref, da_ref, lng_ref, lnb_ref, bd_ref, wout_ref,
                       fg_ref, wgu_ref, wd_ref, final_ref, o_ref):
    bd = bd_ref[...]
    y = yf_ref[...] + yr_ref[...]
    mu = _group_sum(y, bd) * (1.0 / RW_N)
    yc = y - mu
    var = _group_sum(yc * yc, bd) * (1.0 / RW_N)
    yn = yc * lax.rsqrt(var + GN_EPS) * lng_ref[...] + lnb_ref[...]
    y_rw = ((yn + bonus_ref[...]) * g_ref[...]).astype(BF16)
    mixed = _dot(y_rw, wout_ref[0:RW_C, :]) + _dot(da_ref[...].astype(BF16), wout_ref[RW_C:RW_C + DA_C, :])
    x2 = x_ref[...] + mixed
    o_ref[...] = _rms(_half_swiglu(x2, fg_ref, wgu_ref, wd_ref), final_ref[...])


def _mixout_ffn(x1, yf, yr, g, bonus, o_da, ln_g, ln_b, bd, w_out, norm_g, w_gu, w_down, final_g):
    t = x1.shape[0]
    tm = min(TOKEN_TILE, t)
    tok = pl.BlockSpec((tm, RW_C), lambda i: (i, 0))
    return pl.pallas_call(
        _mixout_ffn_kernel,
        grid=(t // tm,),
        in_specs=[
            pl.BlockSpec((tm, D_MODEL), lambda i: (i, 0)),
            tok, tok, tok, tok, tok,
            _resident((1, RW_C)), _resident((1, RW_C)), _resident((RW_C, RW_C)), _resident((RW_C + DA_C, D_MODEL)),
            _resident((1, D_MODEL)), _resident((D_MODEL, 2 * D_FF)), _resident((D_FF, D_MODEL)),
            _resident((1, D_MODEL)),
        ],
        out_specs=pl.BlockSpec((tm, D_MODEL), lambda i: (i, 0)),
        out_shape=jax.ShapeDtypeStruct((t, D_MODEL), F32),
        compiler_params=_params(("parallel",), VMEM_WEIGHTS_RESIDENT),
        name="mixout_ffn",
    )(x1, yf, yr, g, bonus, o_da, ln_g.reshape(1, RW_C), ln_b.reshape(1, RW_C), bd, w_out.astype(BF16),
      norm_g.reshape(1, D_MODEL), w_gu.astype(BF16), w_down.astype(BF16), final_g.reshape(1, D_MODEL))


def _encoder_layer(x, p):
    bsz, s_len, _ = x.shape
    t = bsz * s_len
    xf = x.reshape(t, D_MODEL)
    lane = jnp.arange(RW_C) // RW_N
    bd = (lane[:, None] == lane[None, :]).astype(BF16)

    x1 = _ffn(xf, p["ffn1_norm"], p["ffn1_w_gu"], p["ffn1_w_down"])
    r, v, kk, g, bonus, lw, kd, b = _rwprep(
        x1, s_len, p["mix_norm"], p["w_in"][:, :RW_COLS], p["conv_w"], p["rw_w0"], p["rw_w_up"], p["rw_a0"],
        p["rw_a_up"], p["rw_g_up"], p["rw_k_k"], p["rw_k_a"], p["rw_r_k"], bd)
    yf, yr = _scan(r, v, kk, lw, kd, b, bsz, s_len)
    q, k, va = _daprep(x1, bsz, s_len, p["mix_norm"], p["w_in"][:, RW_COLS:], p["da_q_norm"], p["da_k_norm"], bd)
    o_da = _attn(q, k, va, bsz, s_len, p["da_lq1"], p["da_lk1"], p["da_lq2"], p["da_lk2"], p["da_subln"])
    out = _mixout_ffn(x1, yf, yr, g, bonus, o_da, p["rw_ln_g"], p["rw_ln_b"], bd, p["w_out"],
                      p["ffn2_norm"], p["ffn2_w_gu"], p["ffn2_w_down"], p["final_norm"])
    return out.reshape(bsz, s_len, D_MODEL)


def kernel(x_prompt, x_sample, ffn1_norm, ffn1_w_gu, ffn1_w_down, mix_norm, w_in, conv_w, rw_w0, rw_w_up, rw_a0, rw_a_up, rw_g_up, rw_k_k, rw_k_a, rw_r_k, rw_ln_g, rw_ln_b, da_q_norm, da_k_norm, da_lq1, da_lk1, da_lq2, da_lk2, da_subln, w_out, ffn2_norm, ffn2_w_gu, ffn2_w_down, final_norm):
    names = ("ffn1_norm", "ffn1_w_gu", "ffn1_w_down", "mix_norm", "w_in", "conv_w", "rw_w0", "rw_w_up", "rw_a0",
             "rw_a_up", "rw_g_up", "rw_k_k", "rw_k_a", "rw_r_k", "rw_ln_g", "rw_ln_b", "da_q_norm", "da_k_norm",
             "da_lq1", "da_lk1", "da_lq2", "da_lk2", "da_subln", "w_out", "ffn2_norm", "ffn2_w_gu", "ffn2_w_down",
             "final_norm")
    vals = (ffn1_norm, ffn1_w_gu, ffn1_w_down, mix_norm, w_in, conv_w, rw_w0, rw_w_up, rw_a0, rw_a_up, rw_g_up,
            rw_k_k, rw_k_a, rw_r_k, rw_ln_g, rw_ln_b, da_q_norm, da_k_norm, da_lq1, da_lk1, da_lq2, da_lk2,
            da_subln, w_out, ffn2_norm, ffn2_w_gu, ffn2_w_down, final_norm)
    assert all(a.shape[0] == 1 for a in vals), "single-layer (depth 1) parameter stacks expected"
    p = {n: a[0] for n, a in zip(names, vals)}
    return (_encoder_layer(x_prompt, p), _encoder_layer(x_sample, p))
```

```python
import functools
import math

import jax
import jax.numpy as jnp
from jax import lax
from jax.experimental import pallas as pl
from jax.experimental.pallas import tpu as pltpu

F32 = jnp.float32
BF16 = jnp.bfloat16

D_MODEL = 1024
D_FF = 2816
RW_HEADS = 8
RW_N = 64
RW_C = RW_HEADS * RW_N
LORA_W = 64
LORA_A = 64
LORA_G = 128
DECAY_SCALE = 0.606531
GN_EPS = 64e-5
DA_HEADS = 4
DA_DH = 64
DA_C = DA_HEADS * 2 * DA_DH
ROT_DIM = DA_DH // 4
ROPE_THETA = 500000.0
NORM_EPS = 1e-6
RW_COLS = 3 * RW_C + 2 * LORA_W + 2 * LORA_A + LORA_G
LAM_INIT = 0.8 - 0.6 * math.exp(-0.3 * 0)
LOG2E = math.log2(math.e)

V7X_VMEM_BYTES = 64 * 1024 * 1024
SUBLANES = 8
VMEM_WEIGHTS_RESIDENT = V7X_VMEM_BYTES * 7 // 8
VMEM_TILES_ONLY = V7X_VMEM_BYTES // 2

FF_CHUNK = 256
N_FF_CHUNKS = D_FF // FF_CHUNK
TOKEN_TILE = 512
PREP_TILE = 512
CHUNK = 64
SCAN_CHUNKS_PER_STEP = 4
ATT_TQ = 256
ATT_STEP_WORK = 2 * 1024 * 1024
ONES_ROWS = 16
VT_ROWS = 2 * DA_DH + ONES_ROWS
ATT_TK = 512


def _params(sem, vmem_bytes):
    return pltpu.CompilerParams(dimension_semantics=sem, vmem_limit_bytes=vmem_bytes)


def _dot(a, b):
    return jnp.dot(a, b, preferred_element_type=F32)


def _dot_nt(a, b):
    return lax.dot_general(a, b, (((1,), (1,)), ((), ())), preferred_element_type=F32)


def _split2(x):
    hi = x.astype(BF16)
    lo = (x - hi.astype(F32)).astype(BF16)
    return hi, lo


def _split3(x):
    hi = x.astype(BF16)
    r1 = x - hi.astype(F32)
    mid = r1.astype(BF16)
    lo = (r1 - mid.astype(F32)).astype(BF16)
    return hi, mid, lo


def _group_sum(x, bd):
    hi, lo = _split2(x)
    return _dot(hi, bd) + _dot(lo, bd)


def _rms(x, g, eps=NORM_EPS):
    return x * lax.rsqrt(jnp.mean(x * x, axis=-1, keepdims=True) + eps) * g


def _half_swiglu(x, g_ref, wgu_ref, wd_ref):
    h = _rms(x, g_ref[...]).astype(BF16)
    acc = jnp.zeros_like(x)
    for j in range(N_FF_CHUNKS):
        lo = j * FF_CHUNK
        gate = _dot(h, wgu_ref[:, lo:lo + FF_CHUNK])
        up = _dot(h, wgu_ref[:, D_FF + lo:D_FF + lo + FF_CHUNK])
        act = (gate * jax.nn.sigmoid(gate) * up).astype(BF16)
        acc = acc + _dot(act, wd_ref[lo:lo + FF_CHUNK, :])
    return x + 0.5 * acc


def _ffn_kernel(x_ref, g_ref, wgu_ref, wd_ref, o_ref):
    o_ref[...] = _half_swiglu(x_ref[...], g_ref, wgu_ref, wd_ref)


def _resident(shape):
    return pl.BlockSpec(shape, lambda i: (0,) * len(shape), pipeline_mode=pl.Buffered(1))


def _ffn(x, norm_g, w_gu, w_down):
    t = x.shape[0]
    tm = min(TOKEN_TILE, t)
    return pl.pallas_call(
        _ffn_kernel,
        grid=(t // tm,),
        in_specs=[
            pl.BlockSpec((tm, D_MODEL), lambda i: (i, 0)),
            _resident((1, D_MODEL)),
            _resident((D_MODEL, 2 * D_FF)),
            _resident((D_FF, D_MODEL)),
        ],
        out_specs=pl.BlockSpec((tm, D_MODEL), lambda i: (i, 0)),
        out_shape=jax.ShapeDtypeStruct((t, D_MODEL), F32),
        compiler_params=_params(("parallel",), VMEM_WEIGHTS_RESIDENT),
        name="ffn",
    )(x, norm_g.reshape(1, D_MODEL), w_gu.astype(BF16), w_down.astype(BF16))


def _rwprep_kernel(x_ref, xp_ref, xn_ref, g_ref, w_ref, conv_ref, w0_ref, wup_ref, a0_ref, aup_ref, gup_ref,
                   kk_ref, ka_ref, rk_ref, bd_ref,
                   r_o, v_o, kk_o, g_o, bonus_o, lw_o, kd_o, b_o, *, tm, s_len):
    i = pl.program_id(0)
    xa = jnp.concatenate([x_ref[...], xp_ref[...], xn_ref[...]], axis=0)
    ua = _dot(_rms(xa, g_ref[...]).astype(BF16), w_ref[...])
    u = ua[0:tm, :]
    start = i * tm
    first = (start % s_len) == 0
    last = ((start + tm) % s_len) == 0
    prev_row = jnp.where(first, 0.0, ua[tm + SUBLANES - 1:tm + SUBLANES, :])
    next_row = jnp.where(last, 0.0, ua[tm + SUBLANES:tm + SUBLANES + 1, :])
    row = lax.broadcasted_iota(jnp.int32, (tm, 1), 0)
    prev = jnp.where(row == 0, prev_row, pltpu.roll(u, 1, 0))
    nxt = jnp.where(row == tm - 1, next_row, pltpu.roll(u, tm - 1, 0))
    c = prev * conv_ref[0:1, :] + u * conv_ref[1:2, :] + nxt * conv_ref[2:3, :]

    o_w = 3 * RW_C
    o_a = o_w + 2 * LORA_W
    o_g = o_a + 2 * LORA_A
    r = c[:, 0:RW_C]
    k = c[:, RW_C:2 * RW_C]
    v = c[:, 2 * RW_C:3 * RW_C]
    tw = jnp.tanh(c[:, o_w:o_a]).astype(BF16)
    ad = c[:, o_a:o_g].astype(BF16)
    gd = c[:, o_g:RW_COLS]
    bd = bd_ref[...]

    kkv = k * kk_ref[...]
    norm = jnp.sqrt(_group_sum(kkv * kkv, bd))
    kk = kkv / jnp.maximum(norm, 1e-12)
    g = _dot(jax.nn.sigmoid(gd).astype(BF16), gup_ref[...])

    ksum = jnp.zeros_like(k)
    for d in range(2):
        lw = -DECAY_SCALE * jax.nn.sigmoid(w0_ref[d:d + 1, :] + _dot(tw, wup_ref[d]))
        a = jax.nn.sigmoid(a0_ref[d:d + 1, :] + _dot(ad, aup_ref[d]))
        kd = k * (1.0 + (a - 1.0) * ka_ref[...])
        lw_o[d] = lw
        kd_o[d] = kd
        b_o[d] = kk * a
        ksum = ksum + kd
    coef = _group_sum(r * ksum * rk_ref[...], bd)
    r_o[...] = r
    v_o[...] = v
    kk_o[...] = kk
    g_o[...] = g
    bonus_o[...] = coef * v


def _pad_dir_rows(w_up):
    z = jnp.zeros_like(w_up[0])
    return jnp.stack([jnp.concatenate([w_up[0], z], 0), jnp.concatenate([z, w_up[1]], 0)]).astype(BF16)


def _rwprep(x1, s_len, norm_g, w_rw, conv_w, w0, w_up, a0, a_up, g_up, k_k, k_a, r_k, bd):
    t = x1.shape[0]
    tm = min(PREP_TILE, s_len)
    nb8 = t // SUBLANES
    per = tm // SUBLANES
    const2 = lambda i: (0, 0)
    const3 = lambda i: (0, 0, 0)
    tok = pl.BlockSpec((tm, RW_C), lambda i: (i, 0))
    tok2 = pl.BlockSpec((2, tm, RW_C), lambda i: (0, i, 0))
    vec = pl.BlockSpec((1, RW_C), const2)
    one = jax.ShapeDtypeStruct((t, RW_C), F32)
    two = jax.ShapeDtypeStruct((2, t, RW_C), F32)
    return pl.pallas_call(
        functools.partial(_rwprep_kernel, tm=tm, s_len=s_len),
        grid=(t // tm,),
        in_specs=[
            pl.BlockSpec((tm, D_MODEL), lambda i: (i, 0)),
            pl.BlockSpec((SUBLANES, D_MODEL), lambda i: (jnp.maximum(i * per - 1, 0), 0)),
            pl.BlockSpec((SUBLANES, D_MODEL), lambda i: (jnp.minimum((i + 1) * per, nb8 - 1), 0)),
            pl.BlockSpec((1, D_MODEL), const2),
            pl.BlockSpec((D_MODEL, RW_COLS), const2),
            pl.BlockSpec((3, RW_COLS), const2),
            pl.BlockSpec((2, RW_C), const2),
            pl.BlockSpec((2, 2 * LORA_W, RW_C), const3),
            pl.BlockSpec((2, RW_C), const2),
            pl.BlockSpec((2, 2 * LORA_A, RW_C), const3),
            pl.BlockSpec((LORA_G, RW_C), const2),
            vec, vec, vec,
            pl.BlockSpec((RW_C, RW_C), const2),
        ],
        out_specs=[tok, tok, tok, tok, tok, tok2, tok2, tok2],
        out_shape=[one, one, one, one, one, two, two, two],
        compiler_params=_params(("parallel",), VMEM_WEIGHTS_RESIDENT),
        name="rwprep",
    )(x1, x1, x1, norm_g.reshape(1, D_MODEL), w_rw.astype(BF16), conv_w, w0, _pad_dir_rows(w_up), a0, _pad_dir_rows(a_up),
      g_up.astype(BF16), k_k.reshape(1, RW_C), k_a.reshape(1, RW_C), r_k.reshape(1, RW_C), bd)


def _scan_kernel(rf_ref, vf_ref, kkf_ref, lwf_ref, kdf_ref, bf_ref, rr_ref, vr_ref, kkr_ref, lwr_ref, kdr_ref, br_ref,
                 yf_ref, yr_ref, h_ref, *, chunk):
    L = chunk
    L2 = 2 * L
    PW = 2 * RW_N
    n_pairs = RW_C // PW
    n_sub = lwf_ref.shape[0] // L
    assert L2 == PW, "interaction matrices and lane pairs share the 128-lane tile"

    @pl.when(pl.program_id(1) == 0)
    def _():
        h_ref[...] = jnp.zeros_like(h_ref)

    row = lax.broadcasted_iota(jnp.int32, (L, L), 0)
    col = lax.broadcasted_iota(jnp.int32, (L, L), 1)
    row2 = lax.broadcasted_iota(jnp.int32, (L, L2), 0)
    col2 = lax.broadcasted_iota(jnp.int32, (L, L2), 1) & (L - 1)
    eye2 = (row2 == col2).astype(F32)
    first_head = lax.broadcasted_iota(jnp.int32, (L, PW), 1) < RW_N

    def stack(z):
        zero = jnp.zeros_like(z)
        return jnp.concatenate([jnp.where(first_head, z, zero), jnp.where(first_head, zero, z)], axis=0)

    chains = []
    dirs = ((rf_ref, vf_ref, kkf_ref, lwf_ref, kdf_ref, bf_ref, yf_ref, 1),
            (rr_ref, vr_ref, kkr_ref, lwr_ref, kdr_ref, br_ref, yr_ref, -1))
    for d, (r_ref, v_ref, kk_ref, lw_ref, kd_ref, b_ref, y_ref, sgn) in enumerate(dirs):
        incl2 = (row2 - col2) * sgn >= 0
        strict2 = (row2 - col2) * sgn > 0
        tri = ((row - col) * sgn >= 0).astype(BF16)
        for step, j in enumerate(range(n_sub) if sgn > 0 else reversed(range(n_sub))):
            rows = slice(j * L, (j + 1) * L)
            lw = lw_ref[rows, :]
            hi, mid, lo = _split3(lw)
            c_in = _dot(tri, hi) + _dot(tri, mid) + _dot(tri, lo)
            c_tot = jnp.sum(lw, axis=0, keepdims=True)
            e_neg = jnp.exp(-c_in)
            e_end = jnp.exp(c_tot - c_in)
            kk = kk_ref[rows, :]
            b = b_ref[rows, :]
            kd = kd_ref[rows, :]
            a_t = -kk * jnp.exp(c_in - lw)
            r_t = r_ref[rows, :] * jnp.exp(c_in)
            b_t = b * e_neg
            k_t = kd * e_neg
            b_p = b * e_end
            k_p = kd * e_end
            p_end = jnp.exp(c_tot)
            v = v_ref[rows, :]
            for p in range(n_pairs):
                ls = slice(p * PW, (p + 1) * PW)
                a_n = a_t[:, ls].astype(BF16)
                chains.append(dict(
                    d=d, p=p, step=step, rows=rows, ls=ls, y_ref=y_ref, incl=incl2, strict=strict2,
                    ar=jnp.concatenate([a_n, r_t[:, ls].astype(BF16)], axis=0),
                    bk=jnp.concatenate([stack(b_t[:, ls].astype(BF16)), stack(k_t[:, ls].astype(BF16))], axis=0),
                    aa=stack(a_n), vv=stack(v[:, ls].astype(BF16)),
                    bpT=jnp.transpose(stack(b_p[:, ls])).astype(BF16),
                    kpT=jnp.transpose(stack(k_p[:, ls])).astype(BF16),
                    pe_col=jnp.transpose(jnp.broadcast_to(p_end[:, ls], (PW, PW)))))

    for c in chains:
        x = _dot_nt(c["ar"], c["bk"])
        c["a_ab"] = jnp.where(c["strict"], x[:L, :L2], 0.0)
        c["a_ak"] = jnp.where(c["strict"], x[:L, L2:], 0.0).astype(BF16)
        c["a_rb"] = jnp.where(c["incl"], x[L:, :L2], 0.0).astype(BF16)
        c["a_rk"] = jnp.where(c["incl"], x[L:, L2:], 0.0).astype(BF16)
        c["inv"] = eye2 + c["a_ab"]
        c["pw"] = c["a_ab"].astype(BF16)
    n_sq = int(math.log2(L)) - 1
    for k in range(n_sq + 1):
        for c in chains:
            pws = stack(c["pw"])
            if k == 0:
                c["pw"] = _dot(c["pw"], pws).astype(BF16)
            elif k < n_sq:
                both = _dot(jnp.concatenate([c["pw"], c["inv"].astype(BF16)], axis=0), pws)
                c["pw"] = both[:L, :].astype(BF16)
                c["inv"] = c["inv"] + both[L:, :]
            else:
                c["inv"] = c["inv"] + _dot(c["inv"].astype(BF16), pws)
    for c in chains:
        c["akv"] = stack(_dot(c["a_ak"], c["vv"]).astype(BF16))
    for c in chains:
        inv = c["inv"].astype(BF16)
        c["wi"] = jnp.concatenate([_dot(inv, c["aa"]).astype(BF16), inv], axis=1)
        c["bkT"] = jnp.concatenate([c["bpT"], c["kpT"]], axis=1)
        c["rab"] = jnp.concatenate([c["ar"][L:, :], c["a_rb"], c["a_rk"]], axis=1)
    state = {(d, p): h_ref[d, p] for d in range(2) for p in range(n_pairs)}
    for step in range(n_sub):
        now = [c for c in chains if c["step"] == step]
        for c in now:
            c["h"] = state[c["d"], c["p"]]
            c["hb"] = c["h"].astype(BF16)
            u = _dot(c["wi"], jnp.concatenate([c["hb"], c["akv"]], axis=0))
            c["uv"] = jnp.concatenate([stack(u.astype(BF16)), c["vv"]], axis=0)
        for c in now:
            state[c["d"], c["p"]] = c["pe_col"] * c["h"] + _dot(c["bkT"], c["uv"])
        for c in now:
            c["y_ref"][c["rows"], c["ls"]] = _dot(c["rab"], jnp.concatenate([c["hb"], c["uv"]], axis=0))
    for (d, p), h in state.items():
        h_ref[d, p] = h


def _scan(r, v, kk, lw, kd, b, bsz, s_len):
    rows = CHUNK * SCAN_CHUNKS_PER_STEP
    nc = s_len // rows
    shp = lambda a: a.reshape(bsz, s_len, RW_C)
    shp2 = lambda a: a.reshape(2, bsz, s_len, RW_C)
    f_sh = pl.BlockSpec((None, rows, RW_C), lambda bi, c: (bi, c, 0))
    r_sh = pl.BlockSpec((None, rows, RW_C), lambda bi, c: (bi, nc - 1 - c, 0))
    f_pd = pl.BlockSpec((None, None, rows, RW_C), lambda bi, c: (0, bi, c, 0))
    r_pd = pl.BlockSpec((None, None, rows, RW_C), lambda bi, c: (1, bi, nc - 1 - c, 0))
    out = jax.ShapeDtypeStruct((bsz, s_len, RW_C), F32)
    r3, v3, kk3, lw4, kd4, b4 = shp(r), shp(v), shp(kk), shp2(lw), shp2(kd), shp2(b)
    yf, yr = pl.pallas_call(
        functools.partial(_scan_kernel, chunk=CHUNK),
        grid=(bsz, nc),
        in_specs=[f_sh, f_sh, f_sh, f_pd, f_pd, f_pd, r_sh, r_sh, r_sh, r_pd, r_pd, r_pd],
        out_specs=[f_sh, r_sh],
        out_shape=[out, out],
        scratch_shapes=[pltpu.VMEM((2, RW_C // (2 * RW_N), 2 * RW_N, 2 * RW_N), F32)],
        compiler_params=_params(("parallel", "arbitrary"), VMEM_TILES_ONLY),
        name="scan",
    )(r3, v3, kk3, lw4, kd4, b4, r3, v3, kk3, lw4, kd4, b4)
    return yf.reshape(bsz * s_len, RW_C), yr.reshape(bsz * s_len, RW_C)


def _daprep_kernel(x_ref, g_ref, w_ref, cf_ref, sn_ref, sp_ref, qn_ref, kn_ref, bd_ref, q_o, k_o, v_o):
    bd = bd_ref[...]
    cf, sn, sp = cf_ref[...], sn_ref[...], sp_ref[...]
    p = _dot(_rms(x_ref[...], g_ref[...]).astype(BF16), w_ref[...])

    def norm_rope(x, g):
        ms = _group_sum(x * x, bd) * (1.0 / DA_DH)
        xn = x * lax.rsqrt(ms + NORM_EPS) * g
        half = ROT_DIM // 2
        return xn * cf + pltpu.roll(xn, DA_C - half, 1) * sn + pltpu.roll(xn, half, 1) * sp

    q = norm_rope(p[:, 0:DA_C], qn_ref[...]) * (DA_DH ** -0.5 * LOG2E)
    k = norm_rope(p[:, DA_C:2 * DA_C], kn_ref[...])
    q_o[...] = q.astype(BF16)
    k_o[...] = k.astype(BF16)
    vt = jnp.transpose(p[:, 2 * DA_C:3 * DA_C]).astype(BF16)
    hw = 2 * DA_DH
    for h in range(DA_HEADS):
        v_o[h, 0:hw, :] = vt[h * hw:(h + 1) * hw, :]
        v_o[h, hw:hw + ONES_ROWS, :] = jnp.ones((ONES_ROWS, vt.shape[1]), BF16)


def _rope_lane_tables(s_len):
    half = ROT_DIM // 2
    inv_freq = ROPE_THETA ** (-jnp.arange(0, ROT_DIM, 2, dtype=F32) / ROT_DIM)
    ang = jnp.arange(s_len, dtype=F32)[:, None] * inv_freq[None, :]
    cos, sin = jnp.cos(ang), jnp.sin(ang)
    j = jnp.arange(DA_C) % DA_DH
    f = j % half
    cf = jnp.where(j < ROT_DIM, cos[:, f], 1.0)
    sn = jnp.where(j < half, -sin[:, f], 0.0)
    sp = jnp.where((j >= half) & (j < ROT_DIM), sin[:, f], 0.0)
    return cf, sn, sp


def _daprep(x1, bsz, s_len, norm_g, w_da, q_norm, k_norm, bd):
    t = x1.shape[0]
    tm = min(ATT_TK, s_len)
    per_seq = s_len // tm
    cf, sn, sp = _rope_lane_tables(s_len)
    const2 = lambda i: (0, 0)
    tok = pl.BlockSpec((tm, DA_C), lambda i: (i, 0))
    tab = pl.BlockSpec((tm, DA_C), lambda i: (i % per_seq, 0))
    vec = pl.BlockSpec((1, DA_C), const2)
    out = jax.ShapeDtypeStruct((t, DA_C), BF16)
    reps = DA_C // DA_DH
    return pl.pallas_call(
        _daprep_kernel,
        grid=(t // tm,),
        in_specs=[pl.BlockSpec((tm, D_MODEL), lambda i: (i, 0)), pl.BlockSpec((1, D_MODEL), const2),
                  pl.BlockSpec((D_MODEL, 3 * DA_C), const2), tab, tab, tab, vec, vec,
                  pl.BlockSpec((DA_C, DA_C), const2)],
        out_specs=[tok, tok, pl.BlockSpec((None, None, DA_HEADS, VT_ROWS, tm),
                                          lambda i: (i // per_seq, i % per_seq, 0, 0, 0))],
        out_shape=[out, out, jax.ShapeDtypeStruct((bsz, per_seq, DA_HEADS, VT_ROWS, tm), BF16)],
        compiler_params=_params(("parallel",), VMEM_WEIGHTS_RESIDENT),
        name="daprep",
    )(x1, norm_g.reshape(1, D_MODEL), w_da.astype(BF16), cf, sn, sp,
      jnp.tile(q_norm, reps).reshape(1, DA_C), jnp.tile(k_norm, reps).reshape(1, DA_C), bd)


def _attn_kernel(q_ref, k_ref, vt_ref, lq1_ref, lk1_ref, lq2_ref, lk2_ref, sub_ref, o_ref,
                 acc_ref, sa_ref, sb_ref, pa_ref, pb_ref, *, tq, tk):
    q = q_ref[...]
    lane = lax.broadcasted_iota(jnp.int32, q.shape, 1)
    zero = jnp.zeros_like(q)
    qs = jnp.concatenate([jnp.where(lane < DA_DH, q, zero), jnp.where(lane >= DA_DH, q, zero)], axis=0)
    nk = k_ref.shape[0] // tk

    def scores(i):
        return _dot_nt(k_ref[i * tk:(i + 1) * tk, :], qs)

    def scores_into(s_ref, i):
        s = scores(i)
        s_ref[...] = s
        return jnp.max(s, axis=0, keepdims=True)

    def softmax(s_ref, p_ref, m_prev, m_blk):
        m_new = jnp.maximum(m_prev, m_blk)
        p_ref[...] = jnp.exp2((s_ref[...] - m_new).astype(BF16))
        return m_new, jnp.exp2(m_prev - m_new)

    def accumulate(p_ref, i, corr):
        pv = _dot(vt_ref[i], p_ref[...])
        acc_ref[...] = pv if i == 0 else corr * acc_ref[...] + pv

    s_refs = (sa_ref, sb_ref)
    p_refs = (pa_ref, pb_ref)
    m_blk = scores_into(s_refs[0], 0)
    m = jnp.full((1, 2 * tq), -jnp.inf, F32)
    corr = None
    for i in range(nk):
        m_next = scores_into(s_refs[(i + 1) % 2], i + 1) if i + 1 < nk else None
        if i > 0:
            accumulate(p_refs[(i - 1) % 2], i - 1, corr)
        m, corr = softmax(s_refs[i % 2], p_refs[i % 2], m, m_blk)
        m_blk = m_next
    accumulate(p_refs[(nk - 1) % 2], nk - 1, corr)

    hw = 2 * DA_DH
    o2 = acc_ref[0:hw, :] / acc_ref[hw:hw + 1, :]
    lam = (jnp.exp(jnp.sum(lq1_ref[...] * lk1_ref[...], axis=1, keepdims=True))
           - jnp.exp(jnp.sum(lq2_ref[...] * lk2_ref[...], axis=1, keepdims=True))) + LAM_INIT
    o = jnp.transpose(o2[:, 0:tq] - lam * o2[:, tq:2 * tq])
    o_ref[...] = _rms(o, sub_ref[...]) * (1.0 - LAM_INIT)


def _attn(q, k, vt, bsz, s_len, lq1, lk1, lq2, lk2, subln):
    tq = min(max(ATT_TQ, ATT_STEP_WORK // s_len), 4 * ATT_TQ, s_len)
    tk = min(ATT_TK, s_len)
    nk = s_len // tk
    hw = 2 * DA_DH
    shp = lambda a: a.reshape(bsz, s_len, DA_C)
    lvec = pl.BlockSpec((1, DA_DH), lambda b, h, qi: (0, 0))
    o = pl.pallas_call(
        functools.partial(_attn_kernel, tq=tq, tk=tk),
        grid=(bsz, DA_HEADS, s_len // tq),
        in_specs=[
            pl.BlockSpec((None, tq, hw), lambda b, h, qi: (b, qi, h)),
            pl.BlockSpec((None, s_len, hw), lambda b, h, qi: (b, 0, h)),
            pl.BlockSpec((None, nk, None, VT_ROWS, tk), lambda b, h, qi: (b, 0, h, 0, 0)),
            lvec, lvec, lvec, lvec,
            pl.BlockSpec((1, hw), lambda b, h, qi: (0, 0)),
        ],
        out_specs=pl.BlockSpec((None, tq, hw), lambda b, h, qi: (b, qi, h)),
        out_shape=jax.ShapeDtypeStruct((bsz, s_len, DA_C), F32),
        scratch_shapes=[pltpu.VMEM((VT_ROWS, 2 * tq), F32),
                        pltpu.VMEM((tk, 2 * tq), F32), pltpu.VMEM((tk, 2 * tq), F32),
                        pltpu.VMEM((tk, 2 * tq), BF16), pltpu.VMEM((tk, 2 * tq), BF16)],
        compiler_params=_params(("parallel", "parallel", "arbitrary"), VMEM_TILES_ONLY),
        name="attn",
    )(shp(q), shp(k), vt, lq1.reshape(1, DA_DH), lk1.reshape(1, DA_DH), lq2.reshape(1, DA_DH),
      lk2.reshape(1, DA_DH), subln.reshape(1, hw))
    return o.reshape(bsz * s_len, DA_C)


def _mixout_ffn_kernel(x_ref, yf_ref, yr_ref, g_ref, bonus_ref, da_ref, lng_ref, lnb_ref, bd_ref, wout_ref,
                       fg_ref, wgu_ref, wd_ref, final_ref, o_ref):
    bd = bd_ref[...]
    y = yf_ref[...] + yr_ref[...]
    mu = _group_sum(y, bd) * (1.0 / RW_N)
    yc = y - mu
    var = _group_sum(yc * yc, bd) * (1.0 / RW_N)
    yn = yc * lax.rsqrt(var + GN_EPS) * lng_ref[...] + lnb_ref[...]
    y_rw = ((yn + bonus_ref[...]) * g_ref[...]).astype(BF16)
    mixed = _dot(y_rw, wout_ref[0:RW_C, :]) + _dot(da_ref[...].astype(BF16), wout_ref[RW_C:RW_C + DA_C, :])
    x2 = x_ref[...] + mixed
    o_ref[...] = _rms(_half_swiglu(x2, fg_ref, wgu_ref, wd_ref), final_ref[...])


def _mixout_ffn(x1, yf, yr, g, bonus, o_da, ln_g, ln_b, bd, w_out, norm_g, w_gu, w_down, final_g):
    t = x1.shape[0]
    tm = min(TOKEN_TILE, t)
    tok = pl.BlockSpec((tm, RW_C), lambda i: (i, 0))
    return pl.pallas_call(
        _mixout_ffn_kernel,
        grid=(t // tm,),
        in_specs=[
            pl.BlockSpec((tm, D_MODEL), lambda i: (i, 0)),
            tok, tok, tok, tok, tok,
            _resident((1, RW_C)), _resident((1, RW_C)), _resident((RW_C, RW_C)), _resident((RW_C + DA_C, D_MODEL)),
            _resident((1, D_MODEL)), _resident((D_MODEL, 2 * D_FF)), _resident((D_FF, D_MODEL)),
            _resident((1, D_MODEL)),
        ],
        out_specs=pl.BlockSpec((tm, D_MODEL), lambda i: (i, 0)),
        out_shape=jax.ShapeDtypeStruct((t, D_MODEL), F32),
        compiler_params=_params(("parallel",), VMEM_WEIGHTS_RESIDENT),
        name="mixout_ffn",
    )(x1, yf, yr, g, bonus, o_da, ln_g.reshape(1, RW_C), ln_b.reshape(1, RW_C), bd, w_out.astype(BF16),
      norm_g.reshape(1, D_MODEL), w_gu.astype(BF16), w_down.astype(BF16), final_g.reshape(1, D_MODEL))


def _encoder_layer(x, p):
    bsz, s_len, _ = x.shape
    t = bsz * s_len
    xf = x.reshape(t, D_MODEL)
    lane = jnp.arange(RW_C) // RW_N
    bd = (lane[:, None] == lane[None, :]).astype(BF16)

    x1 = _ffn(xf, p["ffn1_norm"], p["ffn1_w_gu"], p["ffn1_w_down"])
    r, v, kk, g, bonus, lw, kd, b = _rwprep(
        x1, s_len, p["mix_norm"], p["w_in"][:, :RW_COLS], p["conv_w"], p["rw_w0"], p["rw_w_up"], p["rw_a0"],
        p["rw_a_up"], p["rw_g_up"], p["rw_k_k"], p["rw_k_a"], p["rw_r_k"], bd)
    yf, yr = _scan(r, v, kk, lw, kd, b, bsz, s_len)
    q, k, va = _daprep(x1, bsz, s_len, p["mix_norm"], p["w_in"][:, RW_COLS:], p["da_q_norm"], p["da_k_norm"], bd)
    o_da = _attn(q, k, va, bsz, s_len, p["da_lq1"], p["da_lk1"], p["da_lq2"], p["da_lk2"], p["da_subln"])
    out = _mixout_ffn(x1, yf, yr, g, bonus, o_da, p["rw_ln_g"], p["rw_ln_b"], bd, p["w_out"],
                      p["ffn2_norm"], p["ffn2_w_gu"], p["ffn2_w_down"], p["final_norm"])
    return out.reshape(bsz, s_len, D_MODEL)


def kernel(x_prompt, x_sample, ffn1_norm, ffn1_w_gu, ffn1_w_down, mix_norm, w_in, conv_w, rw_w0, rw_w_up, rw_a0, rw_a_up, rw_g_up, rw_k_k, rw_k_a, rw_r_k, rw_ln_g, rw_ln_b, da_q_norm, da_k_norm, da_lq1, da_lk1, da_lq2, da_lk2, da_subln, w_out, ffn2_norm, ffn2_w_gu, ffn2_w_down, final_norm):
    names = ("ffn1_norm", "ffn1_w_gu", "ffn1_w_down", "mix_norm", "w_in", "conv_w", "rw_w0", "rw_w_up", "rw_a0",
             "rw_a_up", "rw_g_up", "rw_k_k", "rw_k_a", "rw_r_k", "rw_ln_g", "rw_ln_b", "da_q_norm", "da_k_norm",
             "da_lq1", "da_lk1", "da_lq2", "da_lk2", "da_subln", "w_out", "ffn2_norm", "ffn2_w_gu", "ffn2_w_down",
             "final_norm")
    vals = (ffn1_norm, ffn1_w_gu, ffn1_w_down, mix_norm, w_in, conv_w, rw_w0, rw_w_up, rw_a0, rw_a_up, rw_g_up,
            rw_k_k, rw_k_a, rw_r_k, rw_ln_g, rw_ln_b, da_q_norm, da_k_norm, da_lq1, da_lk1, da_lq2, da_lk2,
            da_subln, w_out, ffn2_norm, ffn2_w_gu, ffn2_w_down, final_norm)
    assert all(a.shape[0] == 1 for a in vals), "single-layer (depth 1) parameter stacks expected"
    p = {n: a[0] for n, a in zip(names, vals)}
    return (_encoder_layer(x_prompt, p), _encoder_layer(x_sample, p))
```
